```python
import math
import jax, jax.numpy as jnp
from jax import lax
import numpy as np

D_MODEL = 1024
BATCH = 4
SEQ = 8192
DEPTH = 1
DEC_BATCH = 128
DEC_SEQ = 1
PAST_LEN = 8192
PAGE_SIZE = 128

HEAD_DIM = 64
MIX_W = D_MODEL
ATT_W = MIX_W // 2
RWKV_W = MIX_W - ATT_W
H_ATT = ATT_W // HEAD_DIM
H_RWKV = RWKV_W // HEAD_DIM
DILATED_BRANCHES = ((128, 1), (512, 4), (2048, 16))
MAX_WINDOW = max(w for w, _ in DILATED_BRANCHES)
N_BUCKETS = 32
BUCKET_MAX_DIST = MAX_WINDOW
LORA_W = 64
LORA_A = 64
SHIFT_W = 3 * RWKV_W + LORA_W + LORA_A
IN_W = 4 * ATT_W + SHIFT_W + RWKV_W
Q_BLOCK = 128
NORM_EPS = 1e-6
GN_EPS = HEAD_DIM * 1e-5

kernel_name = 'dilated_rwkv7_hybrid_step'


def rmsnorm(x, w):
    xf = x.astype(jnp.float32)
    xf = xf * lax.rsqrt(jnp.mean(xf * xf, axis=-1, keepdims=True) + NORM_EPS)
    return (xf * w.astype(jnp.float32)).astype(x.dtype)


def split_cols(z, widths):
    return jnp.split(z, [int(i) for i in np.cumsum(widths)[:-1]], axis=-1)


def t5_bucket(dist):
    max_exact = N_BUCKETS // 2
    nf = jnp.maximum(dist, max_exact).astype(jnp.float32)
    large = max_exact + (jnp.log(nf / max_exact) / math.log(BUCKET_MAX_DIST / max_exact)
                         * (N_BUCKETS - max_exact)).astype(jnp.int32)
    large = jnp.minimum(large, N_BUCKETS - 1)
    return jnp.where(dist < max_exact, dist, large)


def dilated_attention(q, k_ext, v_ext, q_idx, rel_bias):
    scale = HEAD_DIM ** -0.5
    lses, outs = [], []
    for window, dil in DILATED_BRANCHES:
        dist = dil * jnp.arange(window // dil + 1, dtype=jnp.int32)
        bias = rel_bias[t5_bucket(dist)].T.astype(jnp.float32)
        idx = q_idx[:, None] - dist[None, :]
        valid = idx >= 0
        idx = jnp.maximum(idx, 0)
        kg = jnp.take(k_ext, idx, axis=1)
        vg = jnp.take(v_ext, idx, axis=1)
        logits = jnp.einsum('bqhd,bqkhd->bhqk', q, kg).astype(jnp.float32) * scale + bias[None, :, None, :]
        logits = jnp.where(valid[None, None], logits, -jnp.inf)
        lse = jax.nn.logsumexp(logits, axis=-1)
        p = jnp.exp(logits - lse[..., None])
        outs.append(jnp.einsum('bhqk,bqkhd->bqhd', p.astype(vg.dtype), vg).astype(jnp.float32))
        lses.append(lse)
    wts = jax.nn.softmax(jnp.stack(lses), axis=0)
    out = jnp.einsum('nbhq,nbqhd->bqhd', wts, jnp.stack(outs))
    return out.astype(q.dtype)


def attention_mixer(q, k_ext, v_ext, q_start, rel_bias):
    B, T, H, Dh = q.shape
    if T > Q_BLOCK and T % Q_BLOCK == 0:
        def blk(j):
            s = j * Q_BLOCK
            qb = lax.dynamic_slice_in_dim(q, s, Q_BLOCK, axis=1)
            q_idx = q_start + s + jnp.arange(Q_BLOCK, dtype=jnp.int32)
            return dilated_attention(qb, k_ext, v_ext, q_idx, rel_bias)
        out = lax.map(blk, jnp.arange(T // Q_BLOCK, dtype=jnp.int32))
        return jnp.moveaxis(out, 0, 1).reshape(B, T, H, Dh)
    q_idx = q_start + jnp.arange(T, dtype=jnp.int32)
    return dilated_attention(q, k_ext, v_ext, q_idx, rel_bias)


def rwkv7_mixer(u, prev_row, wkv0, mu_shift, w0, w_lora_b, a0, a_lora_b, k_k, k_a, r_k, ln_x_w, ln_x_b):
    B, T, _ = u.shape
    f32 = jnp.float32
    u_prev = jnp.concatenate([prev_row[:, None].astype(u.dtype), u[:, :-1]], axis=1)
    um = u + (u_prev - u) * mu_shift
    r, k, v, xw, xa = split_cols(um, (RWKV_W, RWKV_W, RWKV_W, LORA_W, LORA_A))
    w_log = -jax.nn.softplus(-(w0 + jnp.tanh(xw) @ w_lora_b).astype(f32)) - 0.5
    decay = jnp.exp(-jnp.exp(w_log))
    a = jax.nn.sigmoid((a0 + xa @ a_lora_b).astype(f32))
    hd = lambda t: t.astype(f32).reshape(B, T, H_RWKV, HEAD_DIM)
    kk = hd(k * k_k)
    kk = kk / jnp.maximum(jnp.sqrt(jnp.sum(kk * kk, axis=-1, keepdims=True)), 1e-12)
    k = k.astype(f32) * (1 + (a - 1) * k_a.astype(f32))
    r, k, v, decay, a = hd(r), hd(k), hd(v), hd(decay), hd(a)

    def step(S, inp):
        r_t, k_t, v_t, w_t, kk_t, a_t = inp
        sa = jnp.einsum('bhij,bhj->bhi', S, -kk_t)
        S = S * w_t[:, :, None, :] + sa[..., None] * (kk_t * a_t)[:, :, None, :] + v_t[..., None] * k_t[:, :, None, :]
        return S, jnp.einsum('bhij,bhj->bhi', S, r_t)

    xs = tuple(jnp.moveaxis(t, 1, 0) for t in (r, k, v, decay, kk, a))
    wkv_T, ys = lax.scan(step, wkv0.astype(f32), xs)
    y = jnp.moveaxis(ys, 0, 1)
    mean = jnp.mean(y, axis=-1, keepdims=True)
    var = jnp.mean(jnp.square(y - mean), axis=-1, keepdims=True)
    y = ((y - mean) * lax.rsqrt(var + GN_EPS)).reshape(B, T, RWKV_W) * ln_x_w.astype(f32) + ln_x_b.astype(f32)
    bonus = jnp.sum(r * k * r_k.astype(f32), axis=-1, keepdims=True) * v
    y = y + bonus.reshape(B, T, RWKV_W)
    return y.astype(u.dtype), wkv_T.astype(wkv0.dtype), u[:, -1]


def hybrid_layer(x, c, k_past, v_past, wkv0, prev_row, rel_bias, norm_w, ada_w, ada_b, w_in, mu_shift,
                 w0, w_lora_b, a0, a_lora_b, k_k, k_a, r_k, ln_x_w, ln_x_b, w_out):
    B, T, _ = x.shape
    mod = jnp.einsum('bd,de->be', jax.nn.silu(c), ada_w) + ada_b
    shift, scale, gate = jnp.split(mod, 3, axis=-1)
    h = rmsnorm(x, norm_w) * (1 + scale[:, None]) + shift[:, None]
    z = jnp.einsum('btd,de->bte', h, w_in)
    q, k, v, g_att, u, g_rwkv = split_cols(z, (ATT_W, ATT_W, ATT_W, ATT_W, SHIFT_W, RWKV_W))
    heads = lambda t: t.reshape(B, T, H_ATT, HEAD_DIM)
    q, k, v = heads(q), heads(k), heads(v)
    k_ext = k if k_past is None else jnp.concatenate([k_past.astype(k.dtype), k], axis=1)
    v_ext = v if v_past is None else jnp.concatenate([v_past.astype(v.dtype), v], axis=1)
    q_start = k_ext.shape[1] - T
    att = attention_mixer(q, k_ext, v_ext, q_start, rel_bias).reshape(B, T, ATT_W)
    rw, wkv_T, last_row = rwkv7_mixer(u, prev_row, wkv0, mu_shift, w0, w_lora_b, a0, a_lora_b,
                                      k_k, k_a, r_k, ln_x_w, ln_x_b)
    mixed = jnp.concatenate([att * jax.nn.silu(g_att), rw * jax.nn.silu(g_rwkv)], axis=-1)
    x = x + gate[:, None] * jnp.einsum('bte,ed->btd', mixed, w_out)
    return x, k, v, wkv_T, last_row


def setup_inputs(seed: int = 0) -> dict:
    key = jax.random.key(seed)
    ks = jax.random.split(key, 26)
    f32 = jnp.float32
    nrm = lambda k, shape, s: s * jax.random.normal(k, shape, f32)
    wbuf = min(MAX_WINDOW, PAST_LEN)
    return {
        'x_prompt': nrm(ks[0], (BATCH, SEQ, D_MODEL), 1.0),
        'x_sample': nrm(ks[1], (DEC_BATCH, DEC_SEQ, D_MODEL), 1.0),
        'cache_win_k': nrm(ks[2], (DEPTH, DEC_BATCH, wbuf, H_ATT, HEAD_DIM), 1.0),
        'cache_win_v': nrm(ks[3], (DEPTH, DEC_BATCH, wbuf, H_ATT, HEAD_DIM), 1.0),
        'state_wkv': nrm(ks[4], (DEPTH, DEC_BATCH, H_RWKV, HEAD_DIM, HEAD_DIM), 0.3),
        'state_shift': nrm(ks[5], (DEPTH, DEC_BATCH, SHIFT_W), 1.0),
        'c_prompt': nrm(ks[6], (BATCH, D_MODEL), 1.0),
        'c_sample': nrm(ks[7], (DEC_BATCH, D_MODEL), 1.0),
        'rel_bias': nrm(ks[8], (N_BUCKETS, H_ATT), 0.5),
        'norm_w': 1.0 + nrm(ks[9], (DEPTH, D_MODEL), 0.1),
        'ada_w': nrm(ks[10], (DEPTH, D_MODEL, 3 * D_MODEL), 0.5 * D_MODEL ** -0.5),
        'ada_b': nrm(ks[11], (DEPTH, 3 * D_MODEL), 0.02),
        'w_in': nrm(ks[12], (DEPTH, D_MODEL, IN_W), D_MODEL ** -0.5),
        'mu_shift': jax.random.uniform(ks[13], (DEPTH, SHIFT_W), f32),
        'w0': jax.random.uniform(ks[14], (DEPTH, RWKV_W), f32, -2.0, 1.0),
        'w_lora_b': nrm(ks[15], (DEPTH, LORA_W, RWKV_W), 0.5 * LORA_W ** -0.5),
        'a0': nrm(ks[16], (DEPTH, RWKV_W), 0.1),
        'a_lora_b': nrm(ks[17], (DEPTH, LORA_A, RWKV_W), 0.5 * LORA_A ** -0.5),
        'k_k': 0.85 + nrm(ks[18], (DEPTH, RWKV_W), 0.05),
        'k_a': 1.0 + nrm(ks[19], (DEPTH, RWKV_W), 0.05),
        'r_k': nrm(ks[20], (DEPTH, H_RWKV, HEAD_DIM), 0.1),
        'ln_x_w': 1.0 + nrm(ks[21], (DEPTH, RWKV_W), 0.1),
        'ln_x_b': nrm(ks[22], (DEPTH, RWKV_W), 0.02),
        'w_out': nrm(ks[23], (DEPTH, MIX_W, D_MODEL), MIX_W ** -0.5),
        'final_norm_w': 1.0 + nrm(ks[24], (D_MODEL,), 0.1),
    }


def reference(x_prompt, x_sample, cache_win_k, cache_win_v, state_wkv, state_shift, c_prompt, c_sample,
              rel_bias, norm_w, ada_w, ada_b, w_in, mu_shift, w0, w_lora_b, a0, a_lora_b, k_k, k_a, r_k,
              ln_x_w, ln_x_b, w_out, final_norm_w):
    keep = min(MAX_WINDOW, x_prompt.shape[1])
    xp, xs = x_prompt, x_sample
    wkp, wvp, wks, wvs, svp, svs, shp, shs = [], [], [], [], [], [], [], []
    for l in range(DEPTH):
        p_l = (norm_w[l], ada_w[l], ada_b[l], w_in[l], mu_shift[l], w0[l], w_lora_b[l], a0[l], a_lora_b[l],
               k_k[l], k_a[l], r_k[l], ln_x_w[l], ln_x_b[l], w_out[l])
        wkv_zero = jnp.zeros((xp.shape[0], H_RWKV, HEAD_DIM, HEAD_DIM), jnp.float32)
        prev_zero = jnp.zeros((xp.shape[0], SHIFT_W), xp.dtype)
        xp, kp, vp, Sp, lp = hybrid_layer(xp, c_prompt, None, None, wkv_zero, prev_zero, rel_bias, *p_l)
        xs, ksm, vsm, Ss, ls = hybrid_layer(xs, c_sample, cache_win_k[l], cache_win_v[l], state_wkv[l],
                                            state_shift[l], rel_bias, *p_l)
        wkp.append(kp[:, -keep:])
        wvp.append(vp[:, -keep:])
        wks.append(ksm)
        wvs.append(vsm)
        svp.append(Sp)
        svs.append(Ss)
        shp.append(lp)
        shs.append(ls)
    y_prompt = rmsnorm(xp, final_norm_w)
    y_sample = rmsnorm(xs, final_norm_w)
    return (y_prompt, y_sample, jnp.stack(wkp), jnp.stack(wvp), jnp.stack(wks), jnp.stack(wvs),
            jnp.stack(svp), jnp.stack(svs), jnp.stack(shp), jnp.stack(shs))
```

```python
import functools
import math

import numpy as np
import jax
import jax.numpy as jnp
from jax import lax
from jax.experimental import pallas as pl
from jax.experimental.pallas import tpu as pltpu

F32 = jnp.float32
BF16 = jnp.bfloat16
HIGHEST = lax.Precision.HIGHEST

HEAD_DIM = 64
DILATED_BRANCHES = ((128, 1), (512, 4), (2048, 16))
MAX_WINDOW = max(w for w, _ in DILATED_BRANCHES)
KEYS_PER_BRANCH = 128
N_BUCKETS = 32
BUCKET_MAX_DIST = MAX_WINDOW
LORA_W = 64
NORM_EPS = 1e-6
GN_EPS = HEAD_DIM * 1e-5

LANES = 128
VMEM_LIMIT_BYTES = 56 * 1024 * 1024

ROW_TILE = 256
ATT_BLOCK = 128
ATT_SUPER = MAX_WINDOW
RWKV_CHUNK = 64


def _cparams(*sem):
    return pltpu.CompilerParams(dimension_semantics=sem, vmem_limit_bytes=VMEM_LIMIT_BYTES)


def _silu(x):
    return x * jax.nn.sigmoid(x)


def _nt(a, b, precision=None):
    return lax.dot_general(a, b, (((1,), (1,)), ((), ())), precision=precision, preferred_element_type=F32)


def _tn(a, b, precision=None):
    return lax.dot_general(a, b, (((0,), (0,)), ((), ())), precision=precision, preferred_element_type=F32)


def _nn(a, b, precision=None):
    return jnp.dot(a, b, precision=precision, preferred_element_type=F32)


def _ada_kernel(c_ref, w_ref, b_ref, o_ref):
    s = _silu(c_ref[...])
    o_ref[...] = _nn(s, w_ref[...], HIGHEST) + b_ref[...]


def _ada_mod(c_all, ada_w, ada_b):
    n, d = c_all.shape
    e = ada_w.shape[1]
    tn = 512
    return pl.pallas_call(
        _ada_kernel,
        grid=(e // tn,),
        in_specs=[pl.BlockSpec((n, d), lambda j: (0, 0)),
                  pl.BlockSpec((d, tn), lambda j: (0, j)),
                  pl.BlockSpec((1, tn), lambda j: (0, j))],
        out_specs=pl.BlockSpec((n, tn), lambda j: (0, j)),
        out_shape=jax.ShapeDtypeStruct((n, e), F32),
        compiler_params=_cparams("arbitrary"),
        name="ada_mod",
    )(c_all, ada_w, ada_b.reshape(1, e))


def _inproj_kernel(x_ref, shift_ref, scale_ref, nw_ref, w_ref,
                   q_ref, k_ref, v_ref, g_ref, u_ref, gr_ref, *, att_w, shift_w):
    x = x_ref[...]
    xn = x * lax.rsqrt(jnp.mean(x * x, axis=-1, keepdims=True) + NORM_EPS) * nw_ref[...]
    h = xn * (1.0 + scale_ref[...]) + shift_ref[...]
    z = _nn(h.astype(BF16), w_ref[...])
    npair = att_w // LANES
    for p in range(npair):
        q_ref[p] = z[:, p * LANES:(p + 1) * LANES] * (HEAD_DIM ** -0.5)
        k_ref[p] = z[:, att_w + p * LANES:att_w + (p + 1) * LANES]
        v_ref[p] = z[:, 2 * att_w + p * LANES:2 * att_w + (p + 1) * LANES]
        g_ref[p] = z[:, 3 * att_w + p * LANES:3 * att_w + (p + 1) * LANES]
    u_ref[...] = z[:, 4 * att_w:4 * att_w + shift_w]
    gr_ref[...] = z[:, 4 * att_w + shift_w:]


def _inproj(x, shift, scale, norm_w, w_in_bf16, att_w, shift_w):
    b, t, d = x.shape
    in_w = w_in_bf16.shape[1]
    rw = in_w - 4 * att_w - shift_w
    npair = att_w // LANES
    tm = min(ROW_TILE, t)
    per_row = shift.shape[1] != 1
    mod_spec = (pl.BlockSpec((None, tm, d), lambda i, j: (i, j, 0)) if per_row
                else pl.BlockSpec((None, 1, d), lambda i, j: (i, 0, 0)))
    pair_spec = pl.BlockSpec((None, npair, tm, LANES), lambda i, j: (i, 0, j, 0))
    pair_shape = jax.ShapeDtypeStruct((b, npair, t, LANES), F32)
    return pl.pallas_call(
        functools.partial(_inproj_kernel, att_w=att_w, shift_w=shift_w),
        grid=(b, t // tm),
        in_specs=[pl.BlockSpec((None, tm, d), lambda i, j: (i, j, 0)), mod_spec, mod_spec,
                  pl.BlockSpec((1, d), lambda i, j: (0, 0)),
                  pl.BlockSpec((d, in_w), lambda i, j: (0, 0))],
        out_specs=[pair_spec, pair_spec, pair_spec, pair_spec,
                   pl.BlockSpec((None, tm, shift_w), lambda i, j: (i, j, 0)),
                   pl.BlockSpec((None, tm, rw), lambda i, j: (i, j, 0))],
        out_shape=[pair_shape, pair_shape, pair_shape, pair_shape,
                   jax.ShapeDtypeStruct((b, t, shift_w), F32),
                   jax.ShapeDtypeStruct((b, t, rw), F32)],
        compiler_params=_cparams("arbitrary", "arbitrary"),
        name="in_proj",
    )(x, shift, scale, norm_w.reshape(1, d), w_in_bf16)


def _t5_bucket_np(dist):
    max_exact = N_BUCKETS // 2
    nf = np.maximum(dist, max_exact).astype(np.float32)
    large = max_exact + (np.log(nf / np.float32(max_exact)) / np.float32(math.log(BUCKET_MAX_DIST / max_exact))
                         * np.float32(N_BUCKETS - max_exact)).astype(np.int32)
    large = np.minimum(large, N_BUCKETS - 1)
    return np.where(dist < max_exact, dist, large)


def _branch_bias(rel_bias):
    out = []
    for window, dil in DILATED_BRANCHES:
        dist = dil * np.arange(window // dil + 1, dtype=np.int32)
        out.append(rel_bias[_t5_bucket_np(dist)].T.astype(F32))
    return jnp.stack(out)


def _prompt_bias_tiles(rel_bias):
    bias = _branch_bias(rel_bias)
    i = np.arange(ATT_BLOCK)[:, None]
    c = np.arange(2 * ATT_BLOCK)[None, :]
    j = ATT_BLOCK + i - c
    valid = (j >= 0) & (j <= KEYS_PER_BRANCH)
    tile = jnp.where(valid[None, None], bias[:, :, np.clip(j, 0, KEYS_PER_BRANCH)], -jnp.inf)
    first = jnp.where((c >= ATT_BLOCK)[None, None], tile, -jnp.inf)
    return jnp.stack([tile, first], axis=1)


def _att_prompt_kernel(q_ref, k_ref, v_ref, bias_ref, o_ref, m_ref, l_ref, acc_ref, *, super_rows):
    sb = pl.program_id(2)
    m_ref[...] = jnp.full(m_ref.shape, -jnp.inf, F32)
    l_ref[...] = jnp.zeros(l_ref.shape, F32)
    acc_ref[...] = jnp.zeros(acc_ref.shape, F32)
    lane = lax.broadcasted_iota(jnp.int32, (1, LANES), 1)
    head0 = lane < HEAD_DIM

    for bi, (_, dil) in enumerate(DILATED_BRANCHES):
        nblk = super_rows // (ATT_BLOCK * dil)

        def tile(idx, carry, bi=bi, dil=dil, nblk=nblk):
            res = idx // nblk
            blk = idx - res * nblk
            loc = res + dil * ATT_BLOCK * blk
            glob = sb * super_rows + loc
            prev = glob - dil * ATT_BLOCK
            first = prev < 0
            pstart = jnp.where(first, glob, prev)
            rows = lambda s: (pl.ds(s, ATT_BLOCK, stride=dil) if dil > 1 else pl.ds(s, ATT_BLOCK))
            qt = q_ref[rows(loc), :]
            kcat = jnp.concatenate([k_ref[rows(pstart), :], k_ref[rows(glob), :]], axis=0).astype(BF16)
            vcat = jnp.concatenate([v_ref[rows(pstart), :], v_ref[rows(glob), :]], axis=0).astype(BF16)
            q2 = jnp.concatenate([jnp.where(head0, qt, 0.0), jnp.where(head0, 0.0, qt)], axis=0).astype(BF16)
            s = _nt(q2, kcat)
            s = s + bias_ref[bi, first.astype(jnp.int32)].reshape(2 * ATT_BLOCK, 2 * ATT_BLOCK)
            mt = jnp.max(s, axis=-1, keepdims=True)
            p = jnp.exp(s - mt)
            lt = jnp.sum(p, axis=-1, keepdims=True)
            pv = _nn(p.astype(BF16), vcat)
            o_t = jnp.where(head0, pv[:ATT_BLOCK], pv[ATT_BLOCK:])
            m_t = jnp.where(head0, mt[:ATT_BLOCK], mt[ATT_BLOCK:])
            l_t = jnp.where(head0, lt[:ATT_BLOCK], lt[ATT_BLOCK:])
            m_old = m_ref[rows(loc), :]
            m_new = jnp.maximum(m_old, m_t)
            a_old = jnp.exp(m_old - m_new)
            a_t = jnp.exp(m_t - m_new)
            m_ref[rows(loc), :] = m_new
            l_ref[rows(loc), :] = l_ref[rows(loc), :] * a_old + l_t * a_t
            acc_ref[rows(loc), :] = acc_ref[rows(loc), :] * a_old + o_t * a_t
            return carry

        lax.fori_loop(0, dil * nblk, tile, 0)

    o_ref[...] = acc_ref[...] / l_ref[...]


def _att_prompt(q, k, v, bias_tiles):
    b, npair, t, _ = q.shape
    sr = ATT_SUPER
    nb = len(DILATED_BRANCHES)
    return pl.pallas_call(
        functools.partial(_att_prompt_kernel, super_rows=sr),
        grid=(b, npair, t // sr),
        in_specs=[pl.BlockSpec((None, None, sr, LANES), lambda i, p, s: (i, p, s, 0)),
                  pl.BlockSpec((None, None, t, LANES), lambda i, p, s: (i, p, 0, 0)),
                  pl.BlockSpec((None, None, t, LANES), lambda i, p, s: (i, p, 0, 0)),
                  pl.BlockSpec((nb, 2, 2, ATT_BLOCK, 2 * ATT_BLOCK), lambda i, p, s: (0, 0, p, 0, 0))],
        out_specs=pl.BlockSpec((None, None, sr, LANES), lambda i, p, s: (i, p, s, 0)),
        out_shape=jax.ShapeDtypeStruct((b, npair, t, LANES), F32),
        scratch_shapes=[pltpu.VMEM((sr, LANES), F32)] * 3,
        compiler_params=_cparams("arbitrary", "arbitrary", "arbitrary"),
        name="att_prompt",
    )(q, k, v, bias_tiles)


def _att_sample_kernel(q_ref, kn_ref, vn_ref, k1_ref, v1_ref, k4_ref, v4_ref, k16_ref, v16_ref,
                       bias_ref, b0_ref, e_ref, et_ref, o_ref):
    bt = q_ref.shape[0]
    e = e_ref[...]
    et = et_ref[...]
    nb = len(DILATED_BRANCHES)
    s0_all = _nn((q_ref[...] * kn_ref[...]).astype(BF16), e)
    for r in range(bt):
        q = q_ref[r:r + 1, :]
        s_self = [s0_all[r:r + 1, :] + b0_ref[n] for n in range(nb)]
        logits = [_nn((kr[r] * q).astype(BF16), e) + bias_ref[n]
                  for n, kr in enumerate((k1_ref, k4_ref, k16_ref))]
        m = functools.reduce(jnp.maximum, [jnp.max(s, axis=0, keepdims=True) for s in logits] + s_self)
        ps = [jnp.exp(s - m) for s in logits]
        p0 = functools.reduce(jnp.add, [jnp.exp(s - m) for s in s_self])
        inv = 1.0 / (functools.reduce(jnp.add, [jnp.sum(p, axis=0, keepdims=True) for p in ps]) + p0)
        p0e = _nn(jnp.broadcast_to(p0 * inv, (8, p0.shape[1])).astype(BF16), et)[0:1, :]
        out = p0e * vn_ref[r:r + 1, :]
        for p, vr in zip(ps, (v1_ref, v4_ref, v16_ref)):
            out = out + jnp.sum(_nn((p * inv).astype(BF16), et) * vr[r], axis=0, keepdims=True)
        o_ref[r:r + 1, :] = out


def _att_sample(q, k_new, v_new, cache_k, cache_v, rel_bias):
    b, w = q.shape
    l = cache_k.shape[1]
    nh = w // HEAD_DIM
    kpb = KEYS_PER_BRANCH
    bt = 8
    bias = _branch_bias(rel_bias)
    bias_rows = jnp.flip(bias[:, :, 1:], axis=2).transpose(0, 2, 1)
    bias0 = bias[:, :, 0][:, None, :]
    hid = np.arange(w) // HEAD_DIM
    e = jnp.asarray(hid[:, None] == np.arange(nh)[None, :], BF16)
    et = jnp.asarray(hid[None, :] == np.arange(nh)[:, None], BF16)

    v1 = lambda c: c.reshape(b, l // kpb, kpb, w)
    v4 = lambda c: c.reshape(b, l // (4 * kpb), kpb, 4 * w)
    v16 = lambda c: c.reshape(b, kpb, (l // kpb) * w)
    s1 = pl.BlockSpec((bt, None, kpb, w), lambda i: (i, l // kpb - 1, 0, 0))
    s4 = pl.BlockSpec((bt, None, kpb, w), lambda i: (i, l // (4 * kpb) - 1, 0, 0))
    s16 = pl.BlockSpec((bt, kpb, w), lambda i: (i, 0, 0))
    row = pl.BlockSpec((bt, w), lambda i: (i, 0))
    full = lambda a: pl.BlockSpec(a.shape, lambda i: (0,) * a.ndim)
    return pl.pallas_call(
        _att_sample_kernel,
        grid=(b // bt,),
        in_specs=[row, row, row, s1, s1, s4, s4, s16, s16, full(bias_rows), full(bias0), full(e), full(et)],
        out_specs=row,
        out_shape=jax.ShapeDtypeStruct((b, w), F32),
        compiler_params=_cparams("arbitrary"),
        name="att_sample",
    )(q, k_new, v_new, v1(cache_k), v1(cache_v), v4(cache_k), v4(cache_v), v16(cache_k), v16(cache_v),
      bias_rows, bias0, e, et)


def _rwkv_features(u, u_prev, mu, w0, wlb, a0, alb, kk_scale, ka, rw):
    um = u + (u_prev - u) * mu
    r = um[:, :rw]
    k = um[:, rw:2 * rw]
    v = um[:, 2 * rw:3 * rw]
    xw = um[:, 3 * rw:3 * rw + LORA_W]
    xa = um[:, 3 * rw + LORA_W:3 * rw + 2 * LORA_W]
    wl = w0 + _nn(jnp.tanh(xw).astype(BF16), wlb)
    w_log = -jax.nn.softplus(-wl) - 0.5
    logw = -jnp.exp(w_log)
    a = jax.nn.sigmoid(a0 + _nn(xa.astype(BF16), alb))
    kk_raw = k * kk_scale
    k2 = k * (1.0 + (a - 1.0) * ka)
    return r, k2, v, logw, kk_raw, a


def _head_normalize(kk_raw_h):
    n = jnp.sqrt(jnp.sum(kk_raw_h * kk_raw_h, axis=-1, keepdims=True))
    return kk_raw_h / jnp.maximum(n, 1e-12)


def _group_norm_bonus(y, r, k2, v, rk, lw, lb):
    mean = jnp.mean(y, axis=-1, keepdims=True)
    var = jnp.mean(jnp.square(y - mean), axis=-1, keepdims=True)
    yn = (y - mean) * lax.rsqrt(var + GN_EPS) * lw + lb
    return yn + jnp.sum(r * k2 * rk, axis=-1, keepdims=True) * v


def _rwkv_prompt_kernel(u_ref, prev0_ref, s0_ref, mu_ref, w0_ref, wlb_ref, a0_ref, alb_ref, kk_ref, ka_ref,
                        rk_ref, lw_ref, lb_ref, y_ref, sout_ref, s_scr, carry_scr, *, rw, prec):
    c = pl.program_id(1)
    nc = pl.num_programs(1)
    ch = u_ref.shape[0]
    nh = rw // HEAD_DIM
    hd = HEAD_DIM

    @pl.when(c == 0)
    def _():
        s_scr[...] = s0_ref[...]
        carry_scr[...] = prev0_ref[...]

    u = u_ref[...]
    rowi = lax.broadcasted_iota(jnp.int32, (ch, 1), 0)
    u_prev = jnp.where(rowi == 0, carry_scr[...], pltpu.roll(u, 1, 0))
    carry_scr[...] = u[ch - 1:ch, :]
    r, k2, v, logw, kk_raw, a = _rwkv_features(u, u_prev, mu_ref[...], w0_ref[...], wlb_ref[...], a0_ref[...],
                                               alb_ref[...], kk_ref[...], ka_ref[...], rw)

    ti = lax.broadcasted_iota(jnp.int32, (ch, ch), 0)
    si = lax.broadcasted_iota(jnp.int32, (ch, ch), 1)
    tri_incl = (ti >= si).astype(F32)
    cum = _nn(tri_incl, logw, HIGHEST)
    t2 = lax.broadcasted_iota(jnp.int32, (2 * ch, 2 * ch), 0)
    s2 = lax.broadcasted_iota(jnp.int32, (2 * ch, 2 * ch), 1)
    tt = jnp.where(t2 >= ch, t2 - ch, t2)
    ss = jnp.where(s2 >= ch, s2 - ch, s2)
    score_mask = (tt - ss) >= jnp.where(t2 >= ch, 0, 1)
    eye = (ti == si).astype(F32)
    zeros_cv = jnp.zeros((ch, hd), F32)

    for h in range(nh):
        sl = slice(h * hd, (h + 1) * hd)
        cum_h = cum[:, sl]
        lw_h = logw[:, sl]
        ctot = cum_h[ch - 1:ch, :]
        kk_h = _head_normalize(kk_raw[:, sl])
        b_h = kk_h * a[:, sl]
        e_neg = jnp.exp(-cum_h)
        at = -kk_h * jnp.exp(cum_h - lw_h)
        bt = b_h * e_neg
        kt = k2[:, sl] * e_neg
        rt = r[:, sl] * jnp.exp(cum_h)
        e_end = jnp.exp(ctot - cum_h)
        bh = b_h * e_end
        kh = k2[:, sl] * e_end
        pc = jnp.exp(ctot)
        v_h = v[:, sl]

        sc = _nt(jnp.concatenate([at, rt], axis=0), jnp.concatenate([bt, kt], axis=0), prec)
        sc = jnp.where(score_mask, sc, 0.0)
        top = sc[:ch]
        bot = sc[ch:]
        a_ab = top[:, :ch]
        aak_v = _nn(top, jnp.concatenate([zeros_cv, v_h], axis=0), prec)
        tm = eye + a_ab
        ap = a_ab
        npow = 1
        while npow * 2 < ch:
            ap = _nn(ap, ap, prec)
            tm = tm + _nn(ap, tm, prec)
            npow *= 2
        w_m = _nn(tm, at, prec)
        u0 = _nn(tm, aak_v, prec)
        wz = jnp.concatenate([w_m, zeros_cv], axis=0)
        uv = jnp.concatenate([u0, v_h], axis=0)
        s_h = s_scr[h]
        qhat = rt + _nn(bot, wz, prec)
        y_h = _nt(qhat, s_h, prec) + _nn(bot, uv, prec)
        bk = jnp.concatenate([bh, kh], axis=0)
        phi_t = _tn(wz, bk, prec)
        g0_t = _tn(uv, bk, prec)
        s_scr[h] = s_h * pc + _nn(s_h, phi_t, prec) + g0_t
        y_ref[:, sl] = _group_norm_bonus(y_h, r[:, sl], k2[:, sl], v_h, rk_ref[:, sl], lw_ref[:, sl], lb_ref[:, sl])

    @pl.when(c == nc - 1)
    def _():
        sout_ref[...] = s_scr[...]


def _rwkv_prompt(u, prev0, s0, p, prec=HIGHEST):
    b, t, sw = u.shape
    rw = p["w0"].shape[-1]
    nh = rw // HEAD_DIM
    ch = RWKV_CHUNK
    vec = lambda a: pl.BlockSpec(a.shape, lambda i, c: (0,) * a.ndim)
    params = [p["mu"], p["w0"], p["wlb"], p["a0"], p["alb"], p["kk"], p["ka"], p["rk"], p["lw"], p["lb"]]
    state_spec = pl.BlockSpec((None, nh, HEAD_DIM, HEAD_DIM), lambda i, c: (i, 0, 0, 0))
    return pl.pallas_call(
        functools.partial(_rwkv_prompt_kernel, rw=rw, prec=prec),
        grid=(b, t // ch),
        in_specs=[pl.BlockSpec((None, ch, sw), lambda i, c: (i, c, 0)),
                  pl.BlockSpec((None, 1, sw), lambda i, c: (i, 0, 0)),
                  state_spec] + [vec(a) for a in params],
        out_specs=[pl.BlockSpec((None, ch, rw), lambda i, c: (i, c, 0)), state_spec],
        out_shape=[jax.ShapeDtypeStruct((b, t, rw), F32),
                   jax.ShapeDtypeStruct((b, nh, HEAD_DIM, HEAD_DIM), F32)],
        scratch_shapes=[pltpu.VMEM((nh, HEAD_DIM, HEAD_DIM), F32), pltpu.VMEM((1, sw), F32)],
        compiler_params=_cparams("arbitrary", "arbitrary"),
        name="rwkv_prompt",
    )(u, prev0, s0, *params)


def _rwkv_feat_kernel(u_ref, prev_ref, mu_ref, w0_ref, wlb_ref, a0_ref, alb_ref, kk_ref, ka_ref,
                      r_ref, k_ref, v_ref, w_ref, kkr_ref, a_ref, *, rw):
    r, k2, v, logw, kk_raw, a = _rwkv_features(u_ref[...], prev_ref[...], mu_ref[...], w0_ref[...], wlb_ref[...],
                                               a0_ref[...], alb_ref[...], kk_ref[...], ka_ref[...], rw)
    r_ref[...] = r
    k_ref[...] = k2
    v_ref[...] = v
    w_ref[...] = jnp.exp(logw)
    kkr_ref[...] = kk_raw
    a_ref[...] = a


def _rwkv_step_kernel(s_ref, r_ref, k_ref, v_ref, w_ref, kkr_ref, a_ref, rk_ref, lw_ref, lb_ref,
                      y_ref, sout_ref):
    nh = s_ref.shape[0]
    hd = HEAD_DIM
    ri = lax.broadcasted_iota(jnp.int32, (hd, hd), 0)
    ci = lax.broadcasted_iota(jnp.int32, (hd, hd), 1)
    eye = ri == ci
    to_col = lambda row: jnp.sum(jnp.where(eye, row, 0.0), axis=1, keepdims=True)
    to_row = lambda col: jnp.sum(jnp.where(eye, col, 0.0), axis=0, keepdims=True)
    for h in range(nh):
        sl = slice(h * hd, (h + 1) * hd)
        s = s_ref[h]
        kk = _head_normalize(kkr_ref[:, sl])
        r, k2, v = r_ref[:, sl], k_ref[:, sl], v_ref[:, sl]
        sa = jnp.sum(s * (-kk), axis=1, keepdims=True)
        s_new = s * w_ref[:, sl] + sa * (kk * a_ref[:, sl]) + to_col(v) * k2
        sout_ref[h] = s_new
        y = to_row(jnp.sum(s_new * r, axis=1, keepdims=True))
        y_ref[:, sl] = _group_norm_bonus(y, r, k2, v, rk_ref[:, sl], lw_ref[:, sl], lb_ref[:, sl])


def _rwkv_sample(u, prev, s0, p):
    b, sw = u.shape
    rw = p["w0"].shape[-1]
    nh = rw // HEAD_DIM
    full = lambda a: pl.BlockSpec(a.shape, lambda i: (0,) * a.ndim)
    fparams = [p["mu"], p["w0"], p["wlb"], p["a0"], p["alb"], p["kk"], p["ka"]]
    feats = pl.pallas_call(
        functools.partial(_rwkv_feat_kernel, rw=rw),
        grid=(1,),
        in_specs=[full(u), full(prev)] + [full(a) for a in fparams],
        out_specs=[pl.BlockSpec((b, rw), lambda i: (0, 0))] * 6,
        out_shape=[jax.ShapeDtypeStruct((b, rw), F32)] * 6,
        compiler_params=_cparams("arbitrary"),
        name="rwkv_sample_features",
    )(u, prev, *fparams)
    feats = [f.reshape(b, 1, rw) for f in feats]
    row = pl.BlockSpec((None, 1, rw), lambda i: (i, 0, 0))
    state_spec = pl.BlockSpec((None, nh, HEAD_DIM, HEAD_DIM), lambda i: (i, 0, 0, 0))
    y, s_new = pl.pallas_call(
        _rwkv_step_kernel,
        grid=(b,),
        in_specs=[state_spec] + [row] * 6 + [full(p["rk"]), full(p["lw"]), full(p["lb"])],
        out_specs=[row, state_spec],
        out_shape=[jax.ShapeDtypeStruct((b, 1, rw), F32),
                   jax.ShapeDtypeStruct((b, nh, HEAD_DIM, HEAD_DIM), F32)],
        compiler_params=_cparams("arbitrary"),
        name="rwkv_sample_step",
    )(s0, *feats, p["rk"], p["lw"], p["lb"])
    return y.reshape(b, rw), s_new


def _outproj_kernel(x_ref, att_ref, g_ref, rw_ref, gr_ref, gate_ref, w_ref, fnw_ref, o_ref, *, final_norm):
    npair = att_ref.shape[0]
    acc = _nn((rw_ref[...] * _silu(gr_ref[...])).astype(BF16), w_ref[npair * LANES:, :])
    for p in range(npair):
        m = att_ref[p] * _silu(g_ref[p])
        acc = acc + _nn(m.astype(BF16), w_ref[p * LANES:(p + 1) * LANES, :])
    xo = x_ref[...] + gate_ref[...] * acc
    if final_norm:
        xo = xo * lax.rsqrt(jnp.mean(xo * xo, axis=-1, keepdims=True) + NORM_EPS) * fnw_ref[...]
    o_ref[...] = xo


def _outproj(x, att, g_att, rwk, g_rwkv, gate, w_out_bf16, final_norm_w, final_norm):
    b, t, d = x.shape
    npair = att.shape[1]
    rw = rwk.shape[-1]
    tm = min(ROW_TILE, t)
    per_row = gate.shape[1] != 1
    gate_spec = (pl.BlockSpec((None, tm, d), lambda i, j: (i, j, 0)) if per_row
                 else pl.BlockSpec((None, 1, d), lambda i, j: (i, 0, 0)))
    pair_spec = pl.BlockSpec((None, npair, tm, LANES), lambda i, j: (i, 0, j, 0))
    return pl.pallas_call(
        functools.partial(_outproj_kernel, final_norm=final_norm),
        grid=(b, t // tm),
        in_specs=[pl.BlockSpec((None, tm, d), lambda i, j: (i, j, 0)), pair_spec, pair_spec,
                  pl.BlockSpec((None, tm, rw), lambda i, j: (i, j, 0)),
                  pl.BlockSpec((None, tm, rw), lambda i, j: (i, j, 0)),
                  gate_spec,
                  pl.BlockSpec(w_out_bf16.shape, lambda i, j: (0, 0)),
                  pl.BlockSpec((1, d), lambda i, j: (0, 0))],
        out_specs=pl.BlockSpec((None, tm, d), lambda i, j: (i, j, 0)),
        out_shape=jax.ShapeDtypeStruct((b, t, d), F32),
        compiler_params=_cparams("arbitrary", "arbitrary"),
        name="out_proj",
    )(x, att, g_att, rwk, g_rwkv, gate, w_out_bf16, final_norm_w.reshape(1, d))


def _pairs_to_heads(a):
    b, npair, t, _ = a.shape
    return a.transpose(0, 2, 1, 3).reshape(b, t, npair * (LANES // HEAD_DIM), HEAD_DIM)


def kernel(x_prompt, x_sample, cache_win_k, cache_win_v, state_wkv, state_shift, c_prompt, c_sample, rel_bias, norm_w, ada_w, ada_b, w_in, mu_shift, w0, w_lora_b, a0, a_lora_b, k_k, k_a, r_k, ln_x_w, ln_x_b, w_out, final_norm_w):
    depth = norm_w.shape[0]
    bp, tp, d = x_prompt.shape
    bs, ts, _ = x_sample.shape
    assert ts == 1, "the sample group decodes one token per sequence"
    rw = w0.shape[-1]
    att_w = w_out.shape[1] - rw
    shift_w = mu_shift.shape[-1]
    nh_att = att_w // HEAD_DIM
    nh_rw = rw // HEAD_DIM
    assert tp % ATT_SUPER == 0 and tp % RWKV_CHUNK == 0 and bs % 8 == 0
    assert cache_win_k.shape[2] == MAX_WINDOW
    keep = min(MAX_WINDOW, tp)

    bias_tiles = _prompt_bias_tiles(rel_bias)
    c_all = jnp.concatenate([c_prompt, c_sample], axis=0)
    npad = -c_all.shape[0] % 8
    c_all = jnp.pad(c_all, ((0, npad), (0, 0)))

    xp = x_prompt
    xs = x_sample.reshape(1, bs, d)
    outs = [[] for _ in range(8)]
    for l in range(depth):
        mod = _ada_mod(c_all, ada_w[l], ada_b[l])
        shift, scale, gate = jnp.split(mod, 3, axis=-1)
        w_in_b = w_in[l].astype(BF16)
        w_out_b = w_out[l].astype(BF16)
        p = dict(mu=mu_shift[l].reshape(1, -1), w0=w0[l].reshape(1, -1), wlb=w_lora_b[l].astype(BF16),
                 a0=a0[l].reshape(1, -1), alb=a_lora_b[l].astype(BF16), kk=k_k[l].reshape(1, -1),
                 ka=k_a[l].reshape(1, -1), rk=r_k[l].reshape(1, -1), lw=ln_x_w[l].reshape(1, -1),
                 lb=ln_x_b[l].reshape(1, -1))
        last = l == depth - 1

        pm = lambda a: a[:bp].reshape(bp, 1, d)
        q, k, v, g_att, u, g_rwkv = _inproj(xp, pm(shift), pm(scale), norm_w[l], w_in_b, att_w, shift_w)
        att = _att_prompt(q, k, v, bias_tiles)
        y_rw, s_p = _rwkv_prompt(u, jnp.zeros((bp, 1, shift_w), F32),
                                 jnp.zeros((bp, nh_rw, HEAD_DIM, HEAD_DIM), F32), p)
        xp = _outproj(xp, att, g_att, y_rw, g_rwkv, pm(gate), w_out_b, final_norm_w, last)
        outs[0].append(_pairs_to_heads(k[:, :, tp - keep:, :]))
        outs[1].append(_pairs_to_heads(v[:, :, tp - keep:, :]))
        outs[4].append(s_p)
        outs[6].append(u[:, -1])

        sm = lambda a: a[bp:bp + bs].reshape(1, bs, d)
        q, k, v, g_att, u, g_rwkv = _inproj(xs, sm(shift), sm(scale), norm_w[l], w_in_b, att_w, shift_w)
        nat = lambda a: a[0].transpose(1, 0, 2).reshape(bs, att_w)
        k_new, v_new = nat(k), nat(v)
        att_s = _att_sample(nat(q), k_new, v_new, cache_win_k[l].reshape(bs, MAX_WINDOW, att_w),
                            cache_win_v[l].reshape(bs, MAX_WINDOW, att_w), rel_bias)
        att_s = att_s.reshape(bs, att_w // LANES, LANES).transpose(1, 0, 2)[None]
        y_rw, s_s = _rwkv_sample(u[0], state_shift[l], state_wkv[l], p)
        xs = _outproj(xs, att_s, g_att, y_rw[None], g_rwkv, sm(gate), w_out_b, final_norm_w, last)
        outs[2].append(k_new.reshape(bs, 1, nh_att, HEAD_DIM))
        outs[3].append(v_new.reshape(bs, 1, nh_att, HEAD_DIM))
        outs[5].append(s_s)
        outs[7].append(u[0])

    stack = lambda i: jnp.stack(outs[i])
    return (xp, xs.reshape(bs, 1, d), stack(0), stack(1), stack(2), stack(3), stack(4), stack(5), stack(6),
            stack(7))
```

```python
import functools
import math

import numpy as np
import jax
import jax.numpy as jnp
from jax import lax
from jax.experimental import pallas as pl
from jax.experimental.pallas import tpu as pltpu

F32 = jnp.float32
BF16 = jnp.bfloat16
HIGHEST = lax.Precision.HIGHEST

HEAD_DIM = 64
DILATED_BRANCHES = ((128, 1), (512, 4), (2048, 16))
MAX_WINDOW = max(w for w, _ in DILATED_BRANCHES)
KEYS_PER_BRANCH = 128
N_BUCKETS = 32
BUCKET_MAX_DIST = MAX_WINDOW
LORA_W = 64
NORM_EPS = 1e-6
GN_EPS = HEAD_DIM * 1e-5

LANES = 128
VMEM_LIMIT_BYTES = 56 * 1024 * 1024

ROW_TILE = 256
ATT_BLOCK = 128
ATT_SUPER = MAX_WINDOW
RWKV_CHUNK = 64


def _cparams(*sem):
    return pltpu.CompilerParams(dimension_semantics=sem, vmem_limit_bytes=VMEM_LIMIT_BYTES)


def _silu(x):
    return x * jax.nn.sigmoid(x)


def _nt(a, b, precision=None):
    return lax.dot_general(a, b, (((1,), (1,)), ((), ())), precision=precision, preferred_element_type=F32)


def _tn(a, b, precision=None):
    return lax.dot_general(a, b, (((0,), (0,)), ((), ())), precision=precision, preferred_element_type=F32)


def _nn(a, b, precision=None):
    return jnp.dot(a, b, precision=precision, preferred_element_type=F32)


_NN, _NT, _TN = ((1,), (0,)), ((1,), (1,)), ((0,), (0,))


def _split_bf16(x):
    hi = x.astype(BF16)
    return hi, (x - hi.astype(F32)).astype(BF16)


def _dot(a, b, dims, mode):
    dn = (dims, ((), ()))
    if mode == "f32":
        return lax.dot_general(a, b, dn, precision=HIGHEST, preferred_element_type=F32)
    d = lambda x, y: lax.dot_general(x, y, dn, preferred_element_type=F32)
    if mode == "bf16":
        return d(a.astype(BF16), b.astype(BF16))
    assert mode == "bf16x3", mode
    ah, al = _split_bf16(a)
    bh, bl = _split_bf16(b)
    return d(ah, bh) + (d(ah, bl) + d(al, bh))


def _ada_kernel(c_ref, w_ref, b_ref, o_ref):
    s = _silu(c_ref[...])
    o_ref[...] = _nn(s, w_ref[...], HIGHEST) + b_ref[...]


def _ada_mod(c_all, ada_w, ada_b):
    n, d = c_all.shape
    e = ada_w.shape[1]
    tn = 512
    return pl.pallas_call(
        _ada_kernel,
        grid=(e // tn,),
        in_specs=[pl.BlockSpec((n, d), lambda j: (0, 0)),
                  pl.BlockSpec((d, tn), lambda j: (0, j)),
                  pl.BlockSpec((1, tn), lambda j: (0, j))],
        out_specs=pl.BlockSpec((n, tn), lambda j: (0, j)),
        out_shape=jax.ShapeDtypeStruct((n, e), F32),
        compiler_params=_cparams("arbitrary"),
        name="ada_mod",
    )(c_all, ada_w, ada_b.reshape(1, e))


def _inproj_kernel(x_ref, shift_ref, scale_ref, nw_ref, w_ref,
                   q_ref, k_ref, v_ref, g_ref, u_ref, gr_ref, *, att_w, shift_w):
    x = x_ref[...]
    xn = x * lax.rsqrt(jnp.mean(x * x, axis=-1, keepdims=True) + NORM_EPS) * nw_ref[...]
    h = xn * (1.0 + scale_ref[...]) + shift_ref[...]
    z = _nn(h.astype(BF16), w_ref[...])
    npair = att_w // LANES
    for p in range(npair):
        q_ref[p] = z[:, p * LANES:(p + 1) * LANES] * (HEAD_DIM ** -0.5)
        k_ref[p] = z[:, att_w + p * LANES:att_w + (p + 1) * LANES]
        v_ref[p] = z[:, 2 * att_w + p * LANES:2 * att_w + (p + 1) * LANES]
        g_ref[p] = z[:, 3 * att_w + p * LANES:3 * att_w + (p + 1) * LANES]
    u_ref[...] = z[:, 4 * att_w:4 * att_w + shift_w]
    gr_ref[...] = z[:, 4 * att_w + shift_w:]


def _inproj(x, shift, scale, norm_w, w_in_bf16, att_w, shift_w):
    b, t, d = x.shape
    in_w = w_in_bf16.shape[1]
    rw = in_w - 4 * att_w - shift_w
    npair = att_w // LANES
    tm = min(ROW_TILE, t)
    per_row = shift.shape[1] != 1
    mod_spec = (pl.BlockSpec((None, tm, d), lambda i, j: (i, j, 0)) if per_row
                else pl.BlockSpec((None, 1, d), lambda i, j: (i, 0, 0)))
    pair_spec = pl.BlockSpec((None, npair, tm, LANES), lambda i, j: (i, 0, j, 0))
    pair_shape = jax.ShapeDtypeStruct((b, npair, t, LANES), F32)
    return pl.pallas_call(
        functools.partial(_inproj_kernel, att_w=att_w, shift_w=shift_w),
        grid=(b, t // tm),
        in_specs=[pl.BlockSpec((None, tm, d), lambda i, j: (i, j, 0)), mod_spec, mod_spec,
                  pl.BlockSpec((1, d), lambda i, j: (0, 0)),
                  pl.BlockSpec((d, in_w), lambda i, j: (0, 0))],
        out_specs=[pair_spec, pair_spec, pair_spec, pair_spec,
                   pl.BlockSpec((None, tm, shift_w), lambda i, j: (i, j, 0)),
                   pl.BlockSpec((None, tm, rw), lambda i, j: (i, j, 0))],
        out_shape=[pair_shape, pair_shape, pair_shape, pair_shape,
                   jax.ShapeDtypeStruct((b, t, shift_w), F32),
                   jax.ShapeDtypeStruct((b, t, rw), F32)],
        compiler_params=_cparams("arbitrary", "arbitrary"),
        name="in_proj",
    )(x, shift, scale, norm_w.reshape(1, d), w_in_bf16)


def _t5_bucket_np(dist):
    max_exact = N_BUCKETS // 2
    nf = np.maximum(dist, max_exact).astype(np.float32)
    large = max_exact + (np.log(nf / np.float32(max_exact)) / np.float32(math.log(BUCKET_MAX_DIST / max_exact))
                         * np.float32(N_BUCKETS - max_exact)).astype(np.int32)
    large = np.minimum(large, N_BUCKETS - 1)
    return np.where(dist < max_exact, dist, large)


def _branch_bias(rel_bias):
    out = []
    for window, dil in DILATED_BRANCHES:
        dist = dil * np.arange(window // dil + 1, dtype=np.int32)
        out.append(rel_bias[_t5_bucket_np(dist)].T.astype(F32))
    return jnp.stack(out)


def _prompt_bias_tiles(rel_bias):
    bias = _branch_bias(rel_bias)
    i = np.arange(ATT_BLOCK)[:, None]
    c = np.arange(2 * ATT_BLOCK)[None, :]
    j = ATT_BLOCK + i - c
    valid = (j >= 0) & (j <= KEYS_PER_BRANCH)
    tile = jnp.where(valid[None, None], bias[:, :, np.clip(j, 0, KEYS_PER_BRANCH)], -jnp.inf)
    first = jnp.where((c >= ATT_BLOCK)[None, None], tile, -jnp.inf)
    return jnp.stack([tile, first], axis=1)


def _att_prompt_kernel(q_ref, k_ref, v_ref, bias_ref, o_ref, m_ref, l_ref, acc_ref, *, super_rows):
    sb = pl.program_id(2)
    m_ref[...] = jnp.full(m_ref.shape, -jnp.inf, F32)
    l_ref[...] = jnp.zeros(l_ref.shape, F32)
    acc_ref[...] = jnp.zeros(acc_ref.shape, F32)
    lane = lax.broadcasted_iota(jnp.int32, (1, LANES), 1)
    head0 = lane < HEAD_DIM

    for bi, (_, dil) in enumerate(DILATED_BRANCHES):
        nblk = super_rows // (ATT_BLOCK * dil)

        def tile(idx, carry, bi=bi, dil=dil, nblk=nblk):
            res = idx // nblk
            blk = idx - res * nblk
            loc = res + dil * ATT_BLOCK * blk
            glob = sb * super_rows + loc
            prev = glob - dil * ATT_BLOCK
            first = prev < 0
            pstart = jnp.where(first, glob, prev)
            rows = lambda s: (pl.ds(s, ATT_BLOCK, stride=dil) if dil > 1 else pl.ds(s, ATT_BLOCK))
            qt = q_ref[rows(loc), :]
            kcat = jnp.concatenate([k_ref[rows(pstart), :], k_ref[rows(glob), :]], axis=0).astype(BF16)
            vcat = jnp.concatenate([v_ref[rows(pstart), :], v_ref[rows(glob), :]], axis=0).astype(BF16)
            q2 = jnp.concatenate([jnp.where(head0, qt, 0.0), jnp.where(head0, 0.0, qt)], axis=0).astype(BF16)
            s = _nt(q2, kcat)
            s = s + bias_ref[bi, first.astype(jnp.int32)].reshape(2 * ATT_BLOCK, 2 * ATT_BLOCK)
            mt = jnp.max(s, axis=-1, keepdims=True)
            p = jnp.exp(s - mt)
            lt = jnp.sum(p, axis=-1, keepdims=True)
            pv = _nn(p.astype(BF16), vcat)
            o_t = jnp.where(head0, pv[:ATT_BLOCK], pv[ATT_BLOCK:])
            m_t = jnp.where(head0, mt[:ATT_BLOCK], mt[ATT_BLOCK:])
            l_t = jnp.where(head0, lt[:ATT_BLOCK], lt[ATT_BLOCK:])
            m_old = m_ref[rows(loc), :]
            m_new = jnp.maximum(m_old, m_t)
            a_old = jnp.exp(m_old - m_new)
            a_t = jnp.exp(m_t - m_new)
            m_ref[rows(loc), :] = m_new
            l_ref[rows(loc), :] = l_ref[rows(loc), :] * a_old + l_t * a_t
            acc_ref[rows(loc), :] = acc_ref[rows(loc), :] * a_old + o_t * a_t
            return carry

        lax.fori_loop(0, dil * nblk, tile, 0)

    o_ref[...] = acc_ref[...] / l_ref[...]


def _att_prompt(q, k, v, bias_tiles):
    b, npair, t, _ = q.shape
    sr = ATT_SUPER
    nb = len(DILATED_BRANCHES)
    return pl.pallas_call(
        functools.partial(_att_prompt_kernel, super_rows=sr),
        grid=(b, npair, t // sr),
        in_specs=[pl.BlockSpec((None, None, sr, LANES), lambda i, p, s: (i, p, s, 0)),
                  pl.BlockSpec((None, None, t, LANES), lambda i, p, s: (i, p, 0, 0)),
                  pl.BlockSpec((None, None, t, LANES), lambda i, p, s: (i, p, 0, 0)),
                  pl.BlockSpec((nb, 2, 2, ATT_BLOCK, 2 * ATT_BLOCK), lambda i, p, s: (0, 0, p, 0, 0))],
        out_specs=pl.BlockSpec((None, None, sr, LANES), lambda i, p, s: (i, p, s, 0)),
        out_shape=jax.ShapeDtypeStruct((b, npair, t, LANES), F32),
        scratch_shapes=[pltpu.VMEM((sr, LANES), F32)] * 3,
        compiler_params=_cparams("arbitrary", "arbitrary", "arbitrary"),
        name="att_prompt",
    )(q, k, v, bias_tiles)


def _att_sample_kernel(q_ref, kn_ref, vn_ref, k1_ref, v1_ref, k4_ref, v4_ref, k16_ref, v16_ref,
                       bias_ref, b0_ref, o_ref):
    bt = q_ref.shape[0]
    nb = len(DILATED_BRANCHES)
    for r in range(bt):
        q = q_ref[r]
        s0 = jnp.sum(q * kn_ref[r], axis=-1, keepdims=True)
        s_self = [s0 + b0_ref[n] for n in range(nb)]
        logits = [jnp.sum(kr[r] * q[None], axis=-1, keepdims=True) + bias_ref[n]
                  for n, kr in enumerate((k1_ref, k4_ref, k16_ref))]
        m = functools.reduce(jnp.maximum, [jnp.max(s, axis=0) for s in logits] + s_self)
        ps = [jnp.exp(s - m[None]) for s in logits]
        p0 = functools.reduce(jnp.add, [jnp.exp(s - m) for s in s_self])
        den = functools.reduce(jnp.add, [jnp.sum(p, axis=0) for p in ps]) + p0
        out = p0 * vn_ref[r]
        for p, vr in zip(ps, (v1_ref, v4_ref, v16_ref)):
            out = out + jnp.sum(p * vr[r], axis=0)
        o_ref[r] = out / den


def _att_sample(q, k_new, v_new, cache_k, cache_v, rel_bias):
    b, nh, hd = q.shape
    l = cache_k.shape[1]
    kpb = KEYS_PER_BRANCH
    bt = 4
    bias = _branch_bias(rel_bias)
    bias_rows = jnp.flip(bias[:, :, 1:], axis=2).transpose(0, 2, 1)[..., None]
    bias0 = bias[:, :, 0][..., None]
    v1 = lambda c: c.reshape(b, l // kpb, kpb, nh, hd)
    v4 = lambda c: c.reshape(b, l // (4 * kpb), kpb, 4, nh, hd)
    v16 = lambda c: c.reshape(b, kpb, l // kpb, nh, hd)
    s1 = pl.BlockSpec((bt, None, kpb, nh, hd), lambda i: (i, l // kpb - 1, 0, 0, 0))
    s4 = pl.BlockSpec((bt, None, kpb, None, nh, hd), lambda i: (i, l // (4 * kpb) - 1, 0, 0, 0, 0))
    s16 = pl.BlockSpec((bt, kpb, None, nh, hd), lambda i: (i, 0, 0, 0, 0))
    row = pl.BlockSpec((bt, nh, hd), lambda i: (i, 0, 0))
    full = lambda a: pl.BlockSpec(a.shape, lambda i: (0,) * a.ndim)
    return pl.pallas_call(
        _att_sample_kernel,
        grid=(b // bt,),
        in_specs=[row, row, row, s1, s1, s4, s4, s16, s16, full(bias_rows), full(bias0)],
        out_specs=row,
        out_shape=jax.ShapeDtypeStruct((b, nh, hd), F32),
        compiler_params=_cparams("arbitrary"),
        name="att_sample",
    )(q, k_new, v_new, v1(cache_k), v1(cache_v), v4(cache_k), v4(cache_v), v16(cache_k), v16(cache_v),
      bias_rows, bias0)


def _rwkv_features(u, u_prev, mu, w0, wlb, a0, alb, kk_scale, ka, rw):
    um = u + (u_prev - u) * mu
    r = um[:, :rw]
    k = um[:, rw:2 * rw]
    v = um[:, 2 * rw:3 * rw]
    xw = um[:, 3 * rw:3 * rw + LORA_W]
    xa = um[:, 3 * rw + LORA_W:3 * rw + 2 * LORA_W]
    wl = w0 + _nn(jnp.tanh(xw).astype(BF16), wlb)
    w_log = -jax.nn.softplus(-wl) - 0.5
    logw = -jnp.exp(w_log)
    a = jax.nn.sigmoid(a0 + _nn(xa.astype(BF16), alb))
    kk_raw = k * kk_scale
    k2 = k * (1.0 + (a - 1.0) * ka)
    return r, k2, v, logw, kk_raw, a


def _head_normalize(kk_raw_h):
    n = jnp.sqrt(jnp.sum(kk_raw_h * kk_raw_h, axis=-1, keepdims=True))
    return kk_raw_h / jnp.maximum(n, 1e-12)


def _group_norm_bonus(y, r, k2, v, rk, lw, lb):
    mean = jnp.mean(y, axis=-1, keepdims=True)
    var = jnp.mean(jnp.square(y - mean), axis=-1, keepdims=True)
    yn = (y - mean) * lax.rsqrt(var + GN_EPS) * lw + lb
    return yn + jnp.sum(r * k2 * rk, axis=-1, keepdims=True) * v


def _rwkv_prompt_kernel(u_ref, prev0_ref, s0_ref, mu_ref, w0_ref, wlb_ref, a0_ref, alb_ref, kk_ref, ka_ref,
                        rk_ref, lw_ref, lb_ref, y_ref, sout_ref, s_scr, carry_scr, *, rw, prec):
    c = pl.program_id(1)
    nc = pl.num_programs(1)
    ch = u_ref.shape[0]
    nh = rw // HEAD_DIM
    hd = HEAD_DIM

    @pl.when(c == 0)
    def _():
        s_scr[...] = s0_ref[...]
        carry_scr[...] = prev0_ref[...]

    u = u_ref[...]
    rowi = lax.broadcasted_iota(jnp.int32, (ch, 1), 0)
    u_prev = jnp.where(rowi == 0, carry_scr[...], pltpu.roll(u, 1, 0))
    carry_scr[...] = u[ch - 1:ch, :]
    r, k2, v, logw, kk_raw, a = _rwkv_features(u, u_prev, mu_ref[...], w0_ref[...], wlb_ref[...], a0_ref[...],
                                               alb_ref[...], kk_ref[...], ka_ref[...], rw)

    ti = lax.broadcasted_iota(jnp.int32, (ch, ch), 0)
    si = lax.broadcasted_iota(jnp.int32, (ch, ch), 1)
    tri_incl = (ti >= si).astype(F32)
    cum = _nn(tri_incl, logw, HIGHEST)
    t2 = lax.broadcasted_iota(jnp.int32, (2 * ch, 2 * ch), 0)
    s2 = lax.broadcasted_iota(jnp.int32, (2 * ch, 2 * ch), 1)
    tt = jnp.where(t2 >= ch, t2 - ch, t2)
    ss = jnp.where(s2 >= ch, s2 - ch, s2)
    score_mask = (tt - ss) >= jnp.where(t2 >= ch, 0, 1)
    eye = (ti == si).astype(F32)
    zeros_cv = jnp.zeros((ch, hd), F32)

    heads = range(nh)
    sls = [slice(h * hd, (h + 1) * hd) for h in heads]
    at, rt, bt, kt, bk, pc, vh = [], [], [], [], [], [], []
    for h in heads:
        sl = sls[h]
        cum_h = cum[:, sl]
        ctot = cum_h[ch - 1:ch, :]
        kk_h = _head_normalize(kk_raw[:, sl])
        b_h = kk_h * a[:, sl]
        e_neg = jnp.exp(-cum_h)
        e_end = jnp.exp(ctot - cum_h)
        at.append(-kk_h * jnp.exp(cum_h - logw[:, sl]))
        rt.append(r[:, sl] * jnp.exp(cum_h))
        bt.append(b_h * e_neg)
        kt.append(k2[:, sl] * e_neg)
        bk.append(jnp.concatenate([b_h * e_end, k2[:, sl] * e_end], axis=0))
        pc.append(jnp.exp(ctot))
        vh.append(v[:, sl])

    sc = [jnp.where(score_mask,
                    _dot(jnp.concatenate([at[h], rt[h]], axis=0), jnp.concatenate([bt[h], kt[h]], axis=0),
                         _NT, prec["score"]), 0.0) for h in heads]
    top = [s[:ch] for s in sc]
    bot = [s[ch:] for s in sc]
    aak_v = [_dot(top[h], jnp.concatenate([zeros_cv, vh[h]], axis=0), _NN, prec["aakv"]) for h in heads]
    ap = [t[:, :ch] for t in top]
    tm = [eye + x for x in ap]
    npow = 1
    while npow * 2 < ch:
        ap = [_dot(x, x, _NN, prec["inv"]) for x in ap]
        tm = [tm[h] + _dot(ap[h], tm[h], _NN, prec["inv"]) for h in heads]
        npow *= 2
    w_m = [_dot(tm[h], at[h], _NN, prec["wu"]) for h in heads]
    u0 = [_dot(tm[h], aak_v[h], _NN, prec["wu"]) for h in heads]
    wz = [jnp.concatenate([w_m[h], zeros_cv], axis=0) for h in heads]
    uv = [jnp.concatenate([u0[h], vh[h]], axis=0) for h in heads]
    s_old = [s_scr[h] for h in heads]
    qhat = [rt[h] + _dot(bot[h], wz[h], _NN, prec["y"]) for h in heads]
    y_c = [_dot(bot[h], uv[h], _NN, prec["y"]) for h in heads]
    phi_t = [_dot(wz[h], bk[h], _TN, prec["state"]) for h in heads]
    g0_t = [_dot(uv[h], bk[h], _TN, prec["state"]) for h in heads]
    y_s = [_dot(qhat[h], s_old[h], _NT, prec["y"]) for h in heads]
    s_phi = [_dot(s_old[h], phi_t[h], _NN, prec["state"]) for h in heads]
    for h in heads:
        sl = sls[h]
        s_scr[h] = s_old[h] * pc[h] + s_phi[h] + g0_t[h]
        y_ref[:, sl] = _group_norm_bonus(y_s[h] + y_c[h], r[:, sl], k2[:, sl], vh[h], rk_ref[:, sl],
                                         lw_ref[:, sl], lb_ref[:, sl])

    @pl.when(c == nc - 1)
    def _():
        sout_ref[...] = s_scr[...]


RWKV_PRECISION = dict(score="bf16", aakv="bf16", inv="bf16", wu="bf16", y="bf16", state="bf16")


def _rwkv_prompt(u, prev0, s0, p, prec=None):
    prec = RWKV_PRECISION if prec is None else prec
    b, t, sw = u.shape
    rw = p["w0"].shape[-1]
    nh = rw // HEAD_DIM
    ch = RWKV_CHUNK
    vec = lambda a: pl.BlockSpec(a.shape, lambda i, c: (0,) * a.ndim)
    params = [p["mu"], p["w0"], p["wlb"], p["a0"], p["alb"], p["kk"], p["ka"], p["rk"], p["lw"], p["lb"]]
    state_spec = pl.BlockSpec((None, nh, HEAD_DIM, HEAD_DIM), lambda i, c: (i, 0, 0, 0))
    return pl.pallas_call(
        functools.partial(_rwkv_prompt_kernel, rw=rw, prec=prec),
        grid=(b, t // ch),
        in_specs=[pl.BlockSpec((None, ch, sw), lambda i, c: (i, c, 0)),
                  pl.BlockSpec((None, 1, sw), lambda i, c: (i, 0, 0)),
                  state_spec] + [vec(a) for a in params],
        out_specs=[pl.BlockSpec((None, ch, rw), lambda i, c: (i, c, 0)), state_spec],
        out_shape=[jax.ShapeDtypeStruct((b, t, rw), F32),
                   jax.ShapeDtypeStruct((b, nh, HEAD_DIM, HEAD_DIM), F32)],
        scratch_shapes=[pltpu.VMEM((nh, HEAD_DIM, HEAD_DIM), F32), pltpu.VMEM((1, sw), F32)],
        compiler_params=_cparams("arbitrary", "arbitrary"),
        name="rwkv_prompt",
    )(u, prev0, s0, *params)


def _rwkv_feat_kernel(u_ref, prev_ref, mu_ref, w0_ref, wlb_ref, a0_ref, alb_ref, kk_ref, ka_ref,
                      r_ref, k_ref, v_ref, w_ref, kkr_ref, a_ref, *, rw):
    r, k2, v, logw, kk_raw, a = _rwkv_features(u_ref[...], prev_ref[...], mu_ref[...], w0_ref[...], wlb_ref[...],
                                               a0_ref[...], alb_ref[...], kk_ref[...], ka_ref[...], rw)
    r_ref[...] = r
    k_ref[...] = k2
    v_ref[...] = v
    w_ref[...] = jnp.exp(logw)
    kkr_ref[...] = kk_raw
    a_ref[...] = a


def _rwkv_step_kernel(s_ref, r_ref, k_ref, v_ref, w_ref, kkr_ref, a_ref, rk_ref, lw_ref, lb_ref,
                      y_ref, sout_ref):
    nh = s_ref.shape[0]
    hd = HEAD_DIM
    ri = lax.broadcasted_iota(jnp.int32, (hd, hd), 0)
    ci = lax.broadcasted_iota(jnp.int32, (hd, hd), 1)
    eye = ri == ci
    to_col = lambda row: jnp.sum(jnp.where(eye, row, 0.0), axis=1, keepdims=True)
    to_row = lambda col: jnp.sum(jnp.where(eye, col, 0.0), axis=0, keepdims=True)
    for h in range(nh):
        sl = slice(h * hd, (h + 1) * hd)
        s = s_ref[h]
        kk = _head_normalize(kkr_ref[:, sl])
        r, k2, v = r_ref[:, sl], k_ref[:, sl], v_ref[:, sl]
        sa = jnp.sum(s * (-kk), axis=1, keepdims=True)
        s_new = s * w_ref[:, sl] + sa * (kk * a_ref[:, sl]) + to_col(v) * k2
        sout_ref[h] = s_new
        y = to_row(jnp.sum(s_new * r, axis=1, keepdims=True))
        y_ref[:, sl] = _group_norm_bonus(y, r, k2, v, rk_ref[:, sl], lw_ref[:, sl], lb_ref[:, sl])


def _rwkv_sample(u, prev, s0, p):
    b, sw = u.shape
    rw = p["w0"].shape[-1]
    nh = rw // HEAD_DIM
    full = lambda a: pl.BlockSpec(a.shape, lambda i: (0,) * a.ndim)
    fparams = [p["mu"], p["w0"], p["wlb"], p["a0"], p["alb"], p["kk"], p["ka"]]
    feats = pl.pallas_call(
        functools.partial(_rwkv_feat_kernel, rw=rw),
        grid=(1,),
        in_specs=[full(u), full(prev)] + [full(a) for a in fparams],
        out_specs=[pl.BlockSpec((b, rw), lambda i: (0, 0))] * 6,
        out_shape=[jax.ShapeDtypeStruct((b, rw), F32)] * 6,
        compiler_params=_cparams("arbitrary"),
        name="rwkv_sample_features",
    )(u, prev, *fparams)
    feats = [f.reshape(b, 1, rw) for f in feats]
    row = pl.BlockSpec((None, 1, rw), lambda i: (i, 0, 0))
    state_spec = pl.BlockSpec((None, nh, HEAD_DIM, HEAD_DIM), lambda i: (i, 0, 0, 0))
    y, s_new = pl.pallas_call(
        _rwkv_step_kernel,
        grid=(b,),
        in_specs=[state_spec] + [row] * 6 + [full(p["rk"]), full(p["lw"]), full(p["lb"])],
        out_specs=[row, state_spec],
        out_shape=[jax.ShapeDtypeStruct((b, 1, rw), F32),
                   jax.ShapeDtypeStruct((b, nh, HEAD_DIM, HEAD_DIM), F32)],
        compiler_params=_cparams("arbitrary"),
        name="rwkv_sample_step",
    )(s0, *feats, p["rk"], p["lw"], p["lb"])
    return y.reshape(b, rw), s_new


def _outproj_kernel(x_ref, att_ref, g_ref, rw_ref, gr_ref, gate_ref, w_ref, fnw_ref, o_ref, *, final_norm):
    npair = att_ref.shape[0]
    acc = _nn((rw_ref[...] * _silu(gr_ref[...])).astype(BF16), w_ref[npair * LANES:, :])
    for p in range(npair):
        m = att_ref[p] * _silu(g_ref[p])
        acc = acc + _nn(m.astype(BF16), w_ref[p * LANES:(p + 1) * LANES, :])
    xo = x_ref[...] + gate_ref[...] * acc
    if final_norm:
        xo = xo * lax.rsqrt(jnp.mean(xo * xo, axis=-1, keepdims=True) + NORM_EPS) * fnw_ref[...]
    o_ref[...] = xo


def _outproj(x, att, g_att, rwk, g_rwkv, gate, w_out_bf16, final_norm_w, final_norm):
    b, t, d = x.shape
    npair = att.shape[1]
    rw = rwk.shape[-1]
    tm = min(ROW_TILE, t)
    per_row = gate.shape[1] != 1
    gate_spec = (pl.BlockSpec((None, tm, d), lambda i, j: (i, j, 0)) if per_row
                 else pl.BlockSpec((None, 1, d), lambda i, j: (i, 0, 0)))
    pair_spec = pl.BlockSpec((None, npair, tm, LANES), lambda i, j: (i, 0, j, 0))
    return pl.pallas_call(
        functools.partial(_outproj_kernel, final_norm=final_norm),
        grid=(b, t // tm),
        in_specs=[pl.BlockSpec((None, tm, d), lambda i, j: (i, j, 0)), pair_spec, pair_spec,
                  pl.BlockSpec((None, tm, rw), lambda i, j: (i, j, 0)),
                  pl.BlockSpec((None, tm, rw), lambda i, j: (i, j, 0)),
                  gate_spec,
                  pl.BlockSpec(w_out_bf16.shape, lambda i, j: (0, 0)),
                  pl.BlockSpec((1, d), lambda i, j: (0, 0))],
        out_specs=pl.BlockSpec((None, tm, d), lambda i, j: (i, j, 0)),
        out_shape=jax.ShapeDtypeStruct((b, t, d), F32),
        compiler_params=_cparams("arbitrary", "arbitrary"),
        name="out_proj",
    )(x, att, g_att, rwk, g_rwkv, gate, w_out_bf16, final_norm_w.reshape(1, d))


def _pairs_to_heads(a):
    b, npair, t, _ = a.shape
    return a.transpose(0, 2, 1, 3).reshape(b, t, npair * (LANES // HEAD_DIM), HEAD_DIM)


def kernel(x_prompt, x_sample, cache_win_k, cache_win_v, state_wkv, state_shift, c_prompt, c_sample, rel_bias, norm_w, ada_w, ada_b, w_in, mu_shift, w0, w_lora_b, a0, a_lora_b, k_k, k_a, r_k, ln_x_w, ln_x_b, w_out, final_norm_w):
    depth = norm_w.shape[0]
    bp, tp, d = x_prompt.shape
    bs, ts, _ = x_sample.shape
    assert ts == 1, "the sample group decodes one token per sequence"
    rw = w0.shape[-1]
    att_w = w_out.shape[1] - rw
    shift_w = mu_shift.shape[-1]
    nh_att = att_w // HEAD_DIM
    nh_rw = rw // HEAD_DIM
    assert tp % ATT_SUPER == 0 and tp % RWKV_CHUNK == 0 and bs % 8 == 0
    assert cache_win_k.shape[2] == MAX_WINDOW
    keep = min(MAX_WINDOW, tp)

    bias_tiles = _prompt_bias_tiles(rel_bias)
    c_all = jnp.concatenate([c_prompt, c_sample], axis=0)
    npad = -c_all.shape[0] % 8
    c_all = jnp.pad(c_all, ((0, npad), (0, 0)))

    xp = x_prompt
    xs = x_sample.reshape(1, bs, d)
    outs = [[] for _ in range(8)]
    for l in range(depth):
        mod = _ada_mod(c_all, ada_w[l], ada_b[l])
        shift, scale, gate = jnp.split(mod, 3, axis=-1)
        w_in_b = w_in[l].astype(BF16)
        w_out_b = w_out[l].astype(BF16)
        p = dict(mu=mu_shift[l].reshape(1, -1), w0=w0[l].reshape(1, -1), wlb=w_lora_b[l].astype(BF16),
                 a0=a0[l].reshape(1, -1), alb=a_lora_b[l].astype(BF16), kk=k_k[l].reshape(1, -1),
                 ka=k_a[l].reshape(1, -1), rk=r_k[l].reshape(1, -1), lw=ln_x_w[l].reshape(1, -1),
                 lb=ln_x_b[l].reshape(1, -1))
        last = l == depth - 1

        pm = lambda a: a[:bp].reshape(bp, 1, d)
        q, k, v, g_att, u, g_rwkv = _inproj(xp, pm(shift), pm(scale), norm_w[l], w_in_b, att_w, shift_w)
        att = _att_prompt(q, k, v, bias_tiles)
        y_rw, s_p = _rwkv_prompt(u, jnp.zeros((bp, 1, shift_w), F32),
                                 jnp.zeros((bp, nh_rw, HEAD_DIM, HEAD_DIM), F32), p)
        xp = _outproj(xp, att, g_att, y_rw, g_rwkv, pm(gate), w_out_b, final_norm_w, last)
        outs[0].append(_pairs_to_heads(k[:, :, tp - keep:, :]))
        outs[1].append(_pairs_to_heads(v[:, :, tp - keep:, :]))
        outs[4].append(s_p)
        outs[6].append(u[:, -1])

        sm = lambda a: a[bp:bp + bs].reshape(1, bs, d)
        q, k, v, g_att, u, g_rwkv = _inproj(xs, sm(shift), sm(scale), norm_w[l], w_in_b, att_w, shift_w)
        nat = lambda a: a[0].transpose(1, 0, 2).reshape(bs, nh_att, HEAD_DIM)
        k_new, v_new = nat(k), nat(v)
        att_s = _att_sample(nat(q), k_new, v_new, cache_win_k[l], cache_win_v[l], rel_bias)
        att_s = att_s.reshape(bs, att_w // LANES, LANES).transpose(1, 0, 2)[None]
        y_rw, s_s = _rwkv_sample(u[0], state_shift[l], state_wkv[l], p)
        xs = _outproj(xs, att_s, g_att, y_rw[None], g_rwkv, sm(gate), w_out_b, final_norm_w, last)
        outs[2].append(k_new.reshape(bs, 1, nh_att, HEAD_DIM))
        outs[3].append(v_new.reshape(bs, 1, nh_att, HEAD_DIM))
        outs[5].append(s_s)
        outs[7].append(u[0])

    stack = lambda i: jnp.stack(outs[i])
    return (xp, xs.reshape(bs, 1, d), stack(0), stack(1), stack(2), stack(3), stack(4), stack(5), stack(6),
            stack(7))
```

```python
import functools
import math

import numpy as np
import jax
import jax.numpy as jnp
from jax import lax
from jax.experimental import pallas as pl
from jax.experimental.pallas import tpu as pltpu

F32 = jnp.float32
BF16 = jnp.bfloat16
HIGHEST = lax.Precision.HIGHEST

HEAD_DIM = 64
DILATED_BRANCHES = ((128, 1), (512, 4), (2048, 16))
MAX_WINDOW = max(w for w, _ in DILATED_BRANCHES)
KEYS_PER_BRANCH = 128
N_BUCKETS = 32
BUCKET_MAX_DIST = MAX_WINDOW
LORA_W = 64
NORM_EPS = 1e-6
GN_EPS = HEAD_DIM * 1e-5

LANES = 128
VMEM_LIMIT_BYTES = 56 * 1024 * 1024

ROW_TILE = 256
ATT_BLOCK = 128
ATT_SUPER = MAX_WINDOW
RWKV_CHUNK = 64


def _cparams(*sem):
    return pltpu.CompilerParams(dimension_semantics=sem, vmem_limit_bytes=VMEM_LIMIT_BYTES)


def _silu(x):
    return x * jax.nn.sigmoid(x)


def _nt(a, b, precision=None):
    return lax.dot_general(a, b, (((1,), (1,)), ((), ())), precision=precision, preferred_element_type=F32)


def _tn(a, b, precision=None):
    return lax.dot_general(a, b, (((0,), (0,)), ((), ())), precision=precision, preferred_element_type=F32)


def _nn(a, b, precision=None):
    return jnp.dot(a, b, precision=precision, preferred_element_type=F32)


_NN, _NT, _TN = ((1,), (0,)), ((1,), (1,)), ((0,), (0,))


def _split_bf16(x):
    hi = x.astype(BF16)
    return hi, (x - hi.astype(F32)).astype(BF16)


def _dot(a, b, dims, mode):
    dn = (dims, ((), ()))
    if mode == "f32":
        return lax.dot_general(a, b, dn, precision=HIGHEST, preferred_element_type=F32)
    d = lambda x, y: lax.dot_general(x, y, dn, preferred_element_type=F32)
    if mode == "bf16":
        return d(a.astype(BF16), b.astype(BF16))
    assert mode == "bf16x3", mode
    ah, al = _split_bf16(a)
    bh, bl = _split_bf16(b)
    return d(ah, bh) + (d(ah, bl) + d(al, bh))


def _ada_kernel(c_ref, w_ref, b_ref, o_ref):
    s = _silu(c_ref[...])
    o_ref[...] = _nn(s, w_ref[...], HIGHEST) + b_ref[...]


def _ada_mod(c_all, ada_w, ada_b):
    n, d = c_all.shape
    e = ada_w.shape[1]
    tn = 512
    return pl.pallas_call(
        _ada_kernel,
        grid=(e // tn,),
        in_specs=[pl.BlockSpec((n, d), lambda j: (0, 0)),
                  pl.BlockSpec((d, tn), lambda j: (0, j)),
                  pl.BlockSpec((1, tn), lambda j: (0, j))],
        out_specs=pl.BlockSpec((n, tn), lambda j: (0, j)),
        out_shape=jax.ShapeDtypeStruct((n, e), F32),
        compiler_params=_cparams("arbitrary"),
        name="ada_mod",
    )(c_all, ada_w, ada_b.reshape(1, e))


def _inproj_kernel(x_ref, shift_ref, scale_ref, nw_ref, w_ref,
                   q_ref, k_ref, v_ref, g_ref, u_ref, gr_ref, *, att_w, shift_w):
    x = x_ref[...]
    xn = x * lax.rsqrt(jnp.mean(x * x, axis=-1, keepdims=True) + NORM_EPS) * nw_ref[...]
    h = xn * (1.0 + scale_ref[...]) + shift_ref[...]
    z = _nn(h.astype(BF16), w_ref[...])
    npair = att_w // LANES
    for p in range(npair):
        q_ref[p] = z[:, p * LANES:(p + 1) * LANES] * (HEAD_DIM ** -0.5)
        k_ref[p] = z[:, att_w + p * LANES:att_w + (p + 1) * LANES]
        v_ref[p] = z[:, 2 * att_w + p * LANES:2 * att_w + (p + 1) * LANES]
        g_ref[p] = z[:, 3 * att_w + p * LANES:3 * att_w + (p + 1) * LANES]
    u_ref[...] = z[:, 4 * att_w:4 * att_w + shift_w]
    gr_ref[...] = z[:, 4 * att_w + shift_w:]


def _inproj(x, shift, scale, norm_w, w_in_bf16, att_w, shift_w):
    b, t, d = x.shape
    in_w = w_in_bf16.shape[1]
    rw = in_w - 4 * att_w - shift_w
    npair = att_w // LANES
    tm = min(ROW_TILE, t)
    per_row = shift.shape[1] != 1
    mod_spec = (pl.BlockSpec((None, tm, d), lambda i, j: (i, j, 0)) if per_row
                else pl.BlockSpec((None, 1, d), lambda i, j: (i, 0, 0)))
    pair_spec = pl.BlockSpec((None, npair, tm, LANES), lambda i, j: (i, 0, j, 0))
    pair_shape = jax.ShapeDtypeStruct((b, npair, t, LANES), F32)
    return pl.pallas_call(
        functools.partial(_inproj_kernel, att_w=att_w, shift_w=shift_w),
        grid=(b, t // tm),
        in_specs=[pl.BlockSpec((None, tm, d), lambda i, j: (i, j, 0)), mod_spec, mod_spec,
                  pl.BlockSpec((1, d), lambda i, j: (0, 0)),
                  pl.BlockSpec((d, in_w), lambda i, j: (0, 0))],
        out_specs=[pair_spec, pair_spec, pair_spec, pair_spec,
                   pl.BlockSpec((None, tm, shift_w), lambda i, j: (i, j, 0)),
                   pl.BlockSpec((None, tm, rw), lambda i, j: (i, j, 0))],
        out_shape=[pair_shape, pair_shape, pair_shape, pair_shape,
                   jax.ShapeDtypeStruct((b, t, shift_w), F32),
                   jax.ShapeDtypeStruct((b, t, rw), F32)],
        compiler_params=_cparams("arbitrary", "arbitrary"),
        name="in_proj",
    )(x, shift, scale, norm_w.reshape(1, d), w_in_bf16)


def _t5_bucket_np(dist):
    max_exact = N_BUCKETS // 2
    nf = np.maximum(dist, max_exact).astype(np.float32)
    large = max_exact + (np.log(nf / np.float32(max_exact)) / np.float32(math.log(BUCKET_MAX_DIST / max_exact))
                         * np.float32(N_BUCKETS - max_exact)).astype(np.int32)
    large = np.minimum(large, N_BUCKETS - 1)
    return np.where(dist < max_exact, dist, large)


def _branch_bias(rel_bias):
    out = []
    for window, dil in DILATED_BRANCHES:
        dist = dil * np.arange(window // dil + 1, dtype=np.int32)
        out.append(rel_bias[_t5_bucket_np(dist)].T.astype(F32))
    return jnp.stack(out)


def _prompt_bias_tiles(rel_bias):
    bias = _branch_bias(rel_bias)
    i = np.arange(ATT_BLOCK)[:, None]
    c = np.arange(2 * ATT_BLOCK)[None, :]
    j = ATT_BLOCK + i - c
    valid = (j >= 0) & (j <= KEYS_PER_BRANCH)
    tile = jnp.where(valid[None, None], bias[:, :, np.clip(j, 0, KEYS_PER_BRANCH)], -jnp.inf)
    first = jnp.where((c >= ATT_BLOCK)[None, None], tile, -jnp.inf)
    return jnp.stack([tile, first], axis=1)


def _att_prompt_kernel(q_ref, k_ref, v_ref, bias_ref, o_ref, m_ref, l_ref, acc_ref, *, super_rows):
    sb = pl.program_id(2)
    m_ref[...] = jnp.full(m_ref.shape, -jnp.inf, F32)
    l_ref[...] = jnp.zeros(l_ref.shape, F32)
    acc_ref[...] = jnp.zeros(acc_ref.shape, F32)
    lane = lax.broadcasted_iota(jnp.int32, (1, LANES), 1)
    head0 = lane < HEAD_DIM

    for bi, (_, dil) in enumerate(DILATED_BRANCHES):
        nblk = super_rows // (ATT_BLOCK * dil)

        def tile(idx, carry, bi=bi, dil=dil, nblk=nblk):
            res = idx // nblk
            blk = idx - res * nblk
            loc = res + dil * ATT_BLOCK * blk
            glob = sb * super_rows + loc
            prev = glob - dil * ATT_BLOCK
            first = prev < 0
            pstart = jnp.where(first, glob, prev)
            rows = lambda s: (pl.ds(s, ATT_BLOCK, stride=dil) if dil > 1 else pl.ds(s, ATT_BLOCK))
            qt = q_ref[rows(loc), :]
            kcat = jnp.concatenate([k_ref[rows(pstart), :], k_ref[rows(glob), :]], axis=0).astype(BF16)
            vcat = jnp.concatenate([v_ref[rows(pstart), :], v_ref[rows(glob), :]], axis=0).astype(BF16)
            q2 = jnp.concatenate([jnp.where(head0, qt, 0.0), jnp.where(head0, 0.0, qt)], axis=0).astype(BF16)
            s = _nt(q2, kcat)
            s = s + bias_ref[bi, first.astype(jnp.int32)].reshape(2 * ATT_BLOCK, 2 * ATT_BLOCK)
            mt = jnp.max(s, axis=-1, keepdims=True)
            p = jnp.exp(s - mt)
            lt = jnp.sum(p, axis=-1, keepdims=True)
            pv = _nn(p.astype(BF16), vcat)
            o_t = jnp.where(head0, pv[:ATT_BLOCK], pv[ATT_BLOCK:])
            m_t = jnp.where(head0, mt[:ATT_BLOCK], mt[ATT_BLOCK:])
            l_t = jnp.where(head0, lt[:ATT_BLOCK], lt[ATT_BLOCK:])
            m_old = m_ref[rows(loc), :]
            m_new = jnp.maximum(m_old, m_t)
            a_old = jnp.exp(m_old - m_new)
            a_t = jnp.exp(m_t - m_new)
            m_ref[rows(loc), :] = m_new
            l_ref[rows(loc), :] = l_ref[rows(loc), :] * a_old + l_t * a_t
            acc_ref[rows(loc), :] = acc_ref[rows(loc), :] * a_old + o_t * a_t
            return carry

        lax.fori_loop(0, dil * nblk, tile, 0)

    o_ref[...] = acc_ref[...] / l_ref[...]


def _att_prompt(q, k, v, bias_tiles):
    b, npair, t, _ = q.shape
    sr = ATT_SUPER
    nb = len(DILATED_BRANCHES)
    return pl.pallas_call(
        functools.partial(_att_prompt_kernel, super_rows=sr),
        grid=(b, npair, t // sr),
        in_specs=[pl.BlockSpec((None, None, sr, LANES), lambda i, p, s: (i, p, s, 0)),
                  pl.BlockSpec((None, None, t, LANES), lambda i, p, s: (i, p, 0, 0)),
                  pl.BlockSpec((None, None, t, LANES), lambda i, p, s: (i, p, 0, 0)),
                  pl.BlockSpec((nb, 2, 2, ATT_BLOCK, 2 * ATT_BLOCK), lambda i, p, s: (0, 0, p, 0, 0))],
        out_specs=pl.BlockSpec((None, None, sr, LANES), lambda i, p, s: (i, p, s, 0)),
        out_shape=jax.ShapeDtypeStruct((b, npair, t, LANES), F32),
        scratch_shapes=[pltpu.VMEM((sr, LANES), F32)] * 3,
        compiler_params=_cparams("arbitrary", "arbitrary", "arbitrary"),
        name="att_prompt",
    )(q, k, v, bias_tiles)


def _att_sample_kernel(q_ref, kn_ref, vn_ref, kt_ref, vt_ref, bias_ref, b0_ref, o_ref):
    nh, hd, l = kt_ref.shape
    nb = bias_ref.shape[0]
    q = q_ref[...]
    qb = q.astype(BF16)
    head = lax.broadcasted_iota(jnp.int32, (nh, 1), 0)
    logit = jnp.zeros((nh, l), F32)
    for h in range(nh):
        s = _nn(qb, kt_ref[h].astype(BF16))
        logit = jnp.where(head == h, s, logit)
    s0 = jnp.sum(q * kn_ref[...], axis=-1, keepdims=True)
    ls = [logit + bias_ref[n] for n in range(nb)]
    s_self = [s0 + b0_ref[n] for n in range(nb)]
    m = functools.reduce(jnp.maximum, [jnp.max(x, axis=-1, keepdims=True) for x in ls] + s_self)
    p = functools.reduce(jnp.add, [jnp.exp(x - m) for x in ls])
    p0 = functools.reduce(jnp.add, [jnp.exp(x - m) for x in s_self])
    den = jnp.sum(p, axis=-1, keepdims=True) + p0
    pb = p.astype(BF16)
    out = p0 * vn_ref[...]
    for h in range(nh):
        out = out + jnp.where(head == h, _nt(pb, vt_ref[h].astype(BF16)), 0.0)
    o_ref[...] = out / den


def _att_sample(q, k_new, v_new, cache_kt, cache_vt, rel_bias):
    b, nh, hd = q.shape
    l = cache_kt.shape[-1]
    bias = _branch_bias(rel_bias)
    back = l - np.arange(l)
    tabs = []
    for n, (window, dil) in enumerate(DILATED_BRANCHES):
        used = (back % dil == 0) & (back <= window)
        tabs.append(jnp.where(used[None, :], bias[n][:, np.where(used, back // dil, 0)], -jnp.inf))
    bias_pos = jnp.stack(tabs)
    bias0 = bias[:, :, 0][..., None]
    row = pl.BlockSpec((None, nh, hd), lambda i: (i, 0, 0))
    cache = pl.BlockSpec((None, nh, hd, l), lambda i: (i, 0, 0, 0))
    full = lambda a: pl.BlockSpec(a.shape, lambda i: (0,) * a.ndim)
    return pl.pallas_call(
        _att_sample_kernel,
        grid=(b,),
        in_specs=[row, row, row, cache, cache, full(bias_pos), full(bias0)],
        out_specs=row,
        out_shape=jax.ShapeDtypeStruct((b, nh, hd), F32),
        compiler_params=_cparams("arbitrary"),
        name="att_sample",
    )(q, k_new, v_new, cache_kt, cache_vt, bias_pos, bias0)


def _rwkv_features(u, u_prev, mu, w0, wlb, a0, alb, kk_scale, ka, rw):
    um = u + (u_prev - u) * mu
    r = um[:, :rw]
    k = um[:, rw:2 * rw]
    v = um[:, 2 * rw:3 * rw]
    xw = um[:, 3 * rw:3 * rw + LORA_W]
    xa = um[:, 3 * rw + LORA_W:3 * rw + 2 * LORA_W]
    wl = w0 + _nn(jnp.tanh(xw).astype(BF16), wlb)
    w_log = -jax.nn.softplus(-wl) - 0.5
    logw = -jnp.exp(w_log)
    a = jax.nn.sigmoid(a0 + _nn(xa.astype(BF16), alb))
    kk_raw = k * kk_scale
    k2 = k * (1.0 + (a - 1.0) * ka)
    return r, k2, v, logw, kk_raw, a


def _head_normalize(kk_raw_h):
    n = jnp.sqrt(jnp.sum(kk_raw_h * kk_raw_h, axis=-1, keepdims=True))
    return kk_raw_h / jnp.maximum(n, 1e-12)


def _group_norm_bonus(y, r, k2, v, rk, lw, lb):
    mean = jnp.mean(y, axis=-1, keepdims=True)
    var = jnp.mean(jnp.square(y - mean), axis=-1, keepdims=True)
    yn = (y - mean) * lax.rsqrt(var + GN_EPS) * lw + lb
    return yn + jnp.sum(r * k2 * rk, axis=-1, keepdims=True) * v


def _rwkv_prompt_kernel(u_ref, prev0_ref, s0_ref, mu_ref, w0_ref, wlb_ref, a0_ref, alb_ref, kk_ref, ka_ref,
                        rk_ref, lw_ref, lb_ref, y_ref, sout_ref, s_scr, carry_scr, *, rw, prec):
    c = pl.program_id(1)
    nc = pl.num_programs(1)
    ch = u_ref.shape[0]
    nh = rw // HEAD_DIM
    hd = HEAD_DIM

    @pl.when(c == 0)
    def _():
        s_scr[...] = s0_ref[...]
        carry_scr[...] = prev0_ref[...]

    u = u_ref[...]
    rowi = lax.broadcasted_iota(jnp.int32, (ch, 1), 0)
    u_prev = jnp.where(rowi == 0, carry_scr[...], pltpu.roll(u, 1, 0))
    carry_scr[...] = u[ch - 1:ch, :]
    r, k2, v, logw, kk_raw, a = _rwkv_features(u, u_prev, mu_ref[...], w0_ref[...], wlb_ref[...], a0_ref[...],
                                               alb_ref[...], kk_ref[...], ka_ref[...], rw)

    ti = lax.broadcasted_iota(jnp.int32, (ch, ch), 0)
    si = lax.broadcasted_iota(jnp.int32, (ch, ch), 1)
    tri_incl = (ti >= si).astype(F32)
    cum = _nn(tri_incl, logw, HIGHEST)
    t2 = lax.broadcasted_iota(jnp.int32, (2 * ch, 2 * ch), 0)
    s2 = lax.broadcasted_iota(jnp.int32, (2 * ch, 2 * ch), 1)
    tt = jnp.where(t2 >= ch, t2 - ch, t2)
    ss = jnp.where(s2 >= ch, s2 - ch, s2)
    score_mask = (tt - ss) >= jnp.where(t2 >= ch, 0, 1)
    eye = (ti == si).astype(F32)
    zeros_cv = jnp.zeros((ch, hd), F32)

    heads = range(nh)
    sls = [slice(h * hd, (h + 1) * hd) for h in heads]
    at, rt, bt, kt, bk, pc, vh = [], [], [], [], [], [], []
    for h in heads:
        sl = sls[h]
        cum_h = cum[:, sl]
        ctot = cum_h[ch - 1:ch, :]
        kk_h = _head_normalize(kk_raw[:, sl])
        b_h = kk_h * a[:, sl]
        e_neg = jnp.exp(-cum_h)
        e_end = jnp.exp(ctot - cum_h)
        at.append(-kk_h * jnp.exp(cum_h - logw[:, sl]))
        rt.append(r[:, sl] * jnp.exp(cum_h))
        bt.append(b_h * e_neg)
        kt.append(k2[:, sl] * e_neg)
        bk.append(jnp.concatenate([b_h * e_end, k2[:, sl] * e_end], axis=0))
        pc.append(jnp.exp(ctot))
        vh.append(v[:, sl])

    sc = [jnp.where(score_mask,
                    _dot(jnp.concatenate([at[h], rt[h]], axis=0), jnp.concatenate([bt[h], kt[h]], axis=0),
                         _NT, prec["score"]), 0.0) for h in heads]
    top = [s[:ch] for s in sc]
    bot = [s[ch:] for s in sc]
    aak_v = [_dot(top[h], jnp.concatenate([zeros_cv, vh[h]], axis=0), _NN, prec["aakv"]) for h in heads]
    ap = [t[:, :ch] for t in top]
    tm = [eye + x for x in ap]
    npow = 1
    while npow * 2 < ch:
        ap = [_dot(x, x, _NN, prec["inv"]) for x in ap]
        tm = [tm[h] + _dot(ap[h], tm[h], _NN, prec["inv"]) for h in heads]
        npow *= 2
    w_m = [_dot(tm[h], at[h], _NN, prec["wu"]) for h in heads]
    u0 = [_dot(tm[h], aak_v[h], _NN, prec["wu"]) for h in heads]
    wz = [jnp.concatenate([w_m[h], zeros_cv], axis=0) for h in heads]
    uv = [jnp.concatenate([u0[h], vh[h]], axis=0) for h in heads]
    s_old = [s_scr[h] for h in heads]
    qhat = [rt[h] + _dot(bot[h], wz[h], _NN, prec["y"]) for h in heads]
    y_c = [_dot(bot[h], uv[h], _NN, prec["y"]) for h in heads]
    phi_t = [_dot(wz[h], bk[h], _TN, prec["state"]) for h in heads]
    g0_t = [_dot(uv[h], bk[h], _TN, prec["state"]) for h in heads]
    y_s = [_dot(qhat[h], s_old[h], _NT, prec["y"]) for h in heads]
    s_phi = [_dot(s_old[h], phi_t[h], _NN, prec["state"]) for h in heads]
    for h in heads:
        sl = sls[h]
        s_scr[h] = s_old[h] * pc[h] + s_phi[h] + g0_t[h]
        y_ref[:, sl] = _group_norm_bonus(y_s[h] + y_c[h], r[:, sl], k2[:, sl], vh[h], rk_ref[:, sl],
                                         lw_ref[:, sl], lb_ref[:, sl])

    @pl.when(c == nc - 1)
    def _():
        sout_ref[...] = s_scr[...]


RWKV_PRECISION = dict(score="bf16", aakv="bf16", inv="bf16", wu="bf16", y="bf16", state="bf16")


def _rwkv_prompt(u, prev0, s0, p, prec=None):
    prec = RWKV_PRECISION if prec is None else prec
    b, t, sw = u.shape
    rw = p["w0"].shape[-1]
    nh = rw // HEAD_DIM
    ch = RWKV_CHUNK
    vec = lambda a: pl.BlockSpec(a.shape, lambda i, c: (0,) * a.ndim)
    params = [p["mu"], p["w0"], p["wlb"], p["a0"], p["alb"], p["kk"], p["ka"], p["rk"], p["lw"], p["lb"]]
    state_spec = pl.BlockSpec((None, nh, HEAD_DIM, HEAD_DIM), lambda i, c: (i, 0, 0, 0))
    return pl.pallas_call(
        functools.partial(_rwkv_prompt_kernel, rw=rw, prec=prec),
        grid=(b, t // ch),
        in_specs=[pl.BlockSpec((None, ch, sw), lambda i, c: (i, c, 0)),
                  pl.BlockSpec((None, 1, sw), lambda i, c: (i, 0, 0)),
                  state_spec] + [vec(a) for a in params],
        out_specs=[pl.BlockSpec((None, ch, rw), lambda i, c: (i, c, 0)), state_spec],
        out_shape=[jax.ShapeDtypeStruct((b, t, rw), F32),
                   jax.ShapeDtypeStruct((b, nh, HEAD_DIM, HEAD_DIM), F32)],
        scratch_shapes=[pltpu.VMEM((nh, HEAD_DIM, HEAD_DIM), F32), pltpu.VMEM((1, sw), F32)],
        compiler_params=_cparams("arbitrary", "arbitrary"),
        name="rwkv_prompt",
    )(u, prev0, s0, *params)


def _rwkv_feat_kernel(u_ref, prev_ref, mu_ref, w0_ref, wlb_ref, a0_ref, alb_ref, kk_ref, ka_ref,
                      r_ref, k_ref, v_ref, w_ref, kkr_ref, a_ref, *, rw):
    r, k2, v, logw, kk_raw, a = _rwkv_features(u_ref[...], prev_ref[...], mu_ref[...], w0_ref[...], wlb_ref[...],
                                               a0_ref[...], alb_ref[...], kk_ref[...], ka_ref[...], rw)
    r_ref[...] = r.T
    k_ref[...] = k2.T
    v_ref[...] = v.T
    w_ref[...] = jnp.exp(logw).T
    kkr_ref[...] = kk_raw.T
    a_ref[...] = a.T


def _rwkv_step_kernel(s_ref, r_ref, k_ref, v_ref, w_ref, kkr_ref, a_ref, rk_ref, lw_ref, lb_ref,
                      y_ref, sout_ref, y_scr):
    hd = s_ref.shape[0]
    kk_raw = kkr_ref[...]
    kk = kk_raw / jnp.maximum(jnp.sqrt(jnp.sum(kk_raw * kk_raw, axis=0, keepdims=True)), 1e-12)
    beta = kk * a_ref[...]
    w, k2, r, v = w_ref[...], k_ref[...], r_ref[...], v_ref[...]

    def value_row(i, carry):
        s = s_ref[i]
        sa = jnp.sum(s * kk, axis=0, keepdims=True)
        s_new = s * w - sa * beta + v_ref[pl.ds(i, 1), :] * k2
        sout_ref[i] = s_new
        y_scr[pl.ds(i, 1), :] = jnp.sum(s_new * r, axis=0, keepdims=True)
        return carry

    lax.fori_loop(0, hd, value_row, 0)
    y = y_scr[...]
    mean = jnp.mean(y, axis=0, keepdims=True)
    var = jnp.mean(jnp.square(y - mean), axis=0, keepdims=True)
    yn = (y - mean) * lax.rsqrt(var + GN_EPS) * lw_ref[...] + lb_ref[...]
    y_ref[...] = yn + jnp.sum(r * k2 * rk_ref[...], axis=0, keepdims=True) * v


def _rwkv_sample(u, prev, s0_t, p):
    b, sw = u.shape
    rw = p["w0"].shape[-1]
    nh = rw // HEAD_DIM
    hd = HEAD_DIM
    full = lambda a: pl.BlockSpec(a.shape, lambda i: (0,) * a.ndim)
    fparams = [p["mu"], p["w0"], p["wlb"], p["a0"], p["alb"], p["kk"], p["ka"]]
    feats = pl.pallas_call(
        functools.partial(_rwkv_feat_kernel, rw=rw),
        grid=(1,),
        in_specs=[full(u), full(prev)] + [full(a) for a in fparams],
        out_specs=[pl.BlockSpec((rw, b), lambda i: (0, 0))] * 6,
        out_shape=[jax.ShapeDtypeStruct((rw, b), F32)] * 6,
        compiler_params=_cparams("arbitrary"),
        name="rwkv_sample_features",
    )(u, prev, *fparams)
    chan = pl.BlockSpec((hd, b), lambda h: (h, 0))
    col = pl.BlockSpec((hd, 1), lambda h: (h, 0))
    state_spec = pl.BlockSpec((None, hd, hd, b), lambda h: (h, 0, 0, 0))
    cols = [p[n].reshape(rw, 1) for n in ("rk", "lw", "lb")]
    y_t, s_new = pl.pallas_call(
        _rwkv_step_kernel,
        grid=(nh,),
        in_specs=[state_spec] + [chan] * 6 + [col] * 3,
        out_specs=[chan, state_spec],
        out_shape=[jax.ShapeDtypeStruct((rw, b), F32), jax.ShapeDtypeStruct((nh, hd, hd, b), F32)],
        scratch_shapes=[pltpu.VMEM((hd, b), F32)],
        compiler_params=_cparams("arbitrary"),
        name="rwkv_sample_step",
    )(s0_t, *feats, *cols)
    return y_t.T, s_new


def _outproj_kernel(x_ref, att_ref, g_ref, rw_ref, gr_ref, gate_ref, w_ref, fnw_ref, o_ref, *, final_norm):
    npair = att_ref.shape[0]
    acc = _nn((rw_ref[...] * _silu(gr_ref[...])).astype(BF16), w_ref[npair * LANES:, :])
    for p in range(npair):
        m = att_ref[p] * _silu(g_ref[p])
        acc = acc + _nn(m.astype(BF16), w_ref[p * LANES:(p + 1) * LANES, :])
    xo = x_ref[...] + gate_ref[...] * acc
    if final_norm:
        xo = xo * lax.rsqrt(jnp.mean(xo * xo, axis=-1, keepdims=True) + NORM_EPS) * fnw_ref[...]
    o_ref[...] = xo


def _outproj(x, att, g_att, rwk, g_rwkv, gate, w_out_bf16, final_norm_w, final_norm):
    b, t, d = x.shape
    npair = att.shape[1]
    rw = rwk.shape[-1]
    tm = min(ROW_TILE, t)
    per_row = gate.shape[1] != 1
    gate_spec = (pl.BlockSpec((None, tm, d), lambda i, j: (i, j, 0)) if per_row
                 else pl.BlockSpec((None, 1, d), lambda i, j: (i, 0, 0)))
    pair_spec = pl.BlockSpec((None, npair, tm, LANES), lambda i, j: (i, 0, j, 0))
    return pl.pallas_call(
        functools.partial(_outproj_kernel, final_norm=final_norm),
        grid=(b, t // tm),
        in_specs=[pl.BlockSpec((None, tm, d), lambda i, j: (i, j, 0)), pair_spec, pair_spec,
                  pl.BlockSpec((None, tm, rw), lambda i, j: (i, j, 0)),
                  pl.BlockSpec((None, tm, rw), lambda i, j: (i, j, 0)),
                  gate_spec,
                  pl.BlockSpec(w_out_bf16.shape, lambda i, j: (0, 0)),
                  pl.BlockSpec((1, d), lambda i, j: (0, 0))],
        out_specs=pl.BlockSpec((None, tm, d), lambda i, j: (i, j, 0)),
        out_shape=jax.ShapeDtypeStruct((b, t, d), F32),
        compiler_params=_cparams("arbitrary", "arbitrary"),
        name="out_proj",
    )(x, att, g_att, rwk, g_rwkv, gate, w_out_bf16, final_norm_w.reshape(1, d))


def _pairs_to_heads(a):
    b, npair, t, _ = a.shape
    return a.transpose(0, 2, 1, 3).reshape(b, t, npair * (LANES // HEAD_DIM), HEAD_DIM)


def kernel(x_prompt, x_sample, cache_win_k, cache_win_v, state_wkv, state_shift, c_prompt, c_sample, rel_bias, norm_w, ada_w, ada_b, w_in, mu_shift, w0, w_lora_b, a0, a_lora_b, k_k, k_a, r_k, ln_x_w, ln_x_b, w_out, final_norm_w):
    depth = norm_w.shape[0]
    bp, tp, d = x_prompt.shape
    bs, ts, _ = x_sample.shape
    assert ts == 1, "the sample group decodes one token per sequence"
    rw = w0.shape[-1]
    att_w = w_out.shape[1] - rw
    shift_w = mu_shift.shape[-1]
    nh_att = att_w // HEAD_DIM
    nh_rw = rw // HEAD_DIM
    assert tp % ATT_SUPER == 0 and tp % RWKV_CHUNK == 0 and bs % 8 == 0
    assert cache_win_k.shape[2] == MAX_WINDOW
    keep = min(MAX_WINDOW, tp)

    bias_tiles = _prompt_bias_tiles(rel_bias)
    c_all = jnp.concatenate([c_prompt, c_sample], axis=0)
    npad = -c_all.shape[0] % 8
    c_all = jnp.pad(c_all, ((0, npad), (0, 0)))

    xp = x_prompt
    xs = x_sample.reshape(1, bs, d)
    outs = [[] for _ in range(8)]
    for l in range(depth):
        mod = _ada_mod(c_all, ada_w[l], ada_b[l])
        shift, scale, gate = jnp.split(mod, 3, axis=-1)
        w_in_b = w_in[l].astype(BF16)
        w_out_b = w_out[l].astype(BF16)
        p = dict(mu=mu_shift[l].reshape(1, -1), w0=w0[l].reshape(1, -1), wlb=w_lora_b[l].astype(BF16),
                 a0=a0[l].reshape(1, -1), alb=a_lora_b[l].astype(BF16), kk=k_k[l].reshape(1, -1),
                 ka=k_a[l].reshape(1, -1), rk=r_k[l].reshape(1, -1), lw=ln_x_w[l].reshape(1, -1),
                 lb=ln_x_b[l].reshape(1, -1))
        last = l == depth - 1

        pm = lambda a: a[:bp].reshape(bp, 1, d)
        q, k, v, g_att, u, g_rwkv = _inproj(xp, pm(shift), pm(scale), norm_w[l], w_in_b, att_w, shift_w)
        att = _att_prompt(q, k, v, bias_tiles)
        y_rw, s_p = _rwkv_prompt(u, jnp.zeros((bp, 1, shift_w), F32),
                                 jnp.zeros((bp, nh_rw, HEAD_DIM, HEAD_DIM), F32), p)
        xp = _outproj(xp, att, g_att, y_rw, g_rwkv, pm(gate), w_out_b, final_norm_w, last)
        outs[0].append(_pairs_to_heads(k[:, :, tp - keep:, :]))
        outs[1].append(_pairs_to_heads(v[:, :, tp - keep:, :]))
        outs[4].append(s_p)
        outs[6].append(u[:, -1])

        sm = lambda a: a[bp:bp + bs].reshape(1, bs, d)
        q, k, v, g_att, u, g_rwkv = _inproj(xs, sm(shift), sm(scale), norm_w[l], w_in_b, att_w, shift_w)
        nat = lambda a: a[0].transpose(1, 0, 2).reshape(bs, nh_att, HEAD_DIM)
        k_new, v_new = nat(k), nat(v)
        att_s = _att_sample(nat(q), k_new, v_new, cache_win_k[l].transpose(0, 2, 3, 1),
                            cache_win_v[l].transpose(0, 2, 3, 1), rel_bias)
        att_s = att_s.reshape(bs, att_w // LANES, LANES).transpose(1, 0, 2)[None]
        y_rw, s_s = _rwkv_sample(u[0], state_shift[l], state_wkv[l].transpose(1, 2, 3, 0), p)
        xs = _outproj(xs, att_s, g_att, y_rw[None], g_rwkv, sm(gate), w_out_b, final_norm_w, last)
        outs[2].append(k_new.reshape(bs, 1, nh_att, HEAD_DIM))
        outs[3].append(v_new.reshape(bs, 1, nh_att, HEAD_DIM))
        outs[5].append(s_s.transpose(3, 0, 1, 2))
        outs[7].append(u[0])

    stack = lambda i: jnp.stack(outs[i])
    return (xp, xs.reshape(bs, 1, d), stack(0), stack(1), stack(2), stack(3), stack(4), stack(5), stack(6),
            stack(7))
```

```python
import functools
import math

import numpy as np
import jax
import jax.numpy as jnp
from jax import lax
from jax.experimental import pallas as pl
from jax.experimental.pallas import tpu as pltpu

F32 = jnp.float32
BF16 = jnp.bfloat16
HIGHEST = lax.Precision.HIGHEST

HEAD_DIM = 64
DILATED_BRANCHES = ((128, 1), (512, 4), (2048, 16))
MAX_WINDOW = max(w for w, _ in DILATED_BRANCHES)
KEYS_PER_BRANCH = 128
N_BUCKETS = 32
BUCKET_MAX_DIST = MAX_WINDOW
LORA_W = 64
NORM_EPS = 1e-6
GN_EPS = HEAD_DIM * 1e-5

LANES = 128
VMEM_LIMIT_BYTES = 56 * 1024 * 1024

ROW_TILE = 256
ATT_BLOCK = 128
ATT_SUPER = MAX_WINDOW
ATT_TILES_PER_ITER = 4
RWKV_CHUNK = 64
RWKV_CHUNKS_PER_STEP = 2


def _cparams(*sem):
    return pltpu.CompilerParams(dimension_semantics=sem, vmem_limit_bytes=VMEM_LIMIT_BYTES)


def _silu(x):
    return x * jax.nn.sigmoid(x)


def _nt(a, b, precision=None):
    return lax.dot_general(a, b, (((1,), (1,)), ((), ())), precision=precision, preferred_element_type=F32)


def _tn(a, b, precision=None):
    return lax.dot_general(a, b, (((0,), (0,)), ((), ())), precision=precision, preferred_element_type=F32)


def _nn(a, b, precision=None):
    return jnp.dot(a, b, precision=precision, preferred_element_type=F32)


_NN, _NT, _TN = ((1,), (0,)), ((1,), (1,)), ((0,), (0,))


def _split_bf16(x):
    hi = x.astype(BF16)
    return hi, (x - hi.astype(F32)).astype(BF16)


def _dot(a, b, dims, mode):
    dn = (dims, ((), ()))
    if mode == "f32":
        return lax.dot_general(a, b, dn, precision=HIGHEST, preferred_element_type=F32)
    d = lambda x, y: lax.dot_general(x, y, dn, preferred_element_type=F32)
    if mode == "bf16":
        return d(a.astype(BF16), b.astype(BF16))
    assert mode == "bf16x3", mode
    ah, al = _split_bf16(a)
    bh, bl = _split_bf16(b)
    return d(ah, bh) + (d(ah, bl) + d(al, bh))


def _ada_kernel(c_ref, w_ref, b_ref, o_ref):
    s = _silu(c_ref[...])
    o_ref[...] = _nn(s, w_ref[...], HIGHEST) + b_ref[...]


def _ada_mod(c_all, ada_w, ada_b):
    n, d = c_all.shape
    e = ada_w.shape[1]
    tn = 512
    return pl.pallas_call(
        _ada_kernel,
        grid=(e // tn,),
        in_specs=[pl.BlockSpec((n, d), lambda j: (0, 0)),
                  pl.BlockSpec((d, tn), lambda j: (0, j)),
                  pl.BlockSpec((1, tn), lambda j: (0, j))],
        out_specs=pl.BlockSpec((n, tn), lambda j: (0, j)),
        out_shape=jax.ShapeDtypeStruct((n, e), F32),
        compiler_params=_cparams("arbitrary"),
        name="ada_mod",
    )(c_all, ada_w, ada_b.reshape(1, e))


def _inproj_kernel(x_ref, shift_ref, scale_ref, nw_ref, w_ref,
                   q_ref, k_ref, v_ref, g_ref, u_ref, gr_ref, *, att_w, shift_w):
    x = x_ref[...]
    xn = x * lax.rsqrt(jnp.mean(x * x, axis=-1, keepdims=True) + NORM_EPS) * nw_ref[...]
    h = xn * (1.0 + scale_ref[...]) + shift_ref[...]
    z = _nn(h.astype(BF16), w_ref[...])
    npair = att_w // LANES
    for p in range(npair):
        q_ref[p] = z[:, p * LANES:(p + 1) * LANES] * (HEAD_DIM ** -0.5)
        k_ref[p] = z[:, att_w + p * LANES:att_w + (p + 1) * LANES]
        v_ref[p] = z[:, 2 * att_w + p * LANES:2 * att_w + (p + 1) * LANES]
        g_ref[p] = z[:, 3 * att_w + p * LANES:3 * att_w + (p + 1) * LANES]
    u_ref[...] = z[:, 4 * att_w:4 * att_w + shift_w]
    gr_ref[...] = z[:, 4 * att_w + shift_w:]


def _inproj(x, shift, scale, norm_w, w_in_bf16, att_w, shift_w):
    b, t, d = x.shape
    in_w = w_in_bf16.shape[1]
    rw = in_w - 4 * att_w - shift_w
    npair = att_w // LANES
    tm = min(ROW_TILE, t)
    per_row = shift.shape[1] != 1
    mod_spec = (pl.BlockSpec((None, tm, d), lambda i, j: (i, j, 0)) if per_row
                else pl.BlockSpec((None, 1, d), lambda i, j: (i, 0, 0)))
    pair_spec = pl.BlockSpec((None, npair, tm, LANES), lambda i, j: (i, 0, j, 0))
    pair_shape = jax.ShapeDtypeStruct((b, npair, t, LANES), F32)
    return pl.pallas_call(
        functools.partial(_inproj_kernel, att_w=att_w, shift_w=shift_w),
        grid=(b, t // tm),
        in_specs=[pl.BlockSpec((None, tm, d), lambda i, j: (i, j, 0)), mod_spec, mod_spec,
                  pl.BlockSpec((1, d), lambda i, j: (0, 0)),
                  pl.BlockSpec((d, in_w), lambda i, j: (0, 0))],
        out_specs=[pair_spec, pair_spec, pair_spec, pair_spec,
                   pl.BlockSpec((None, tm, shift_w), lambda i, j: (i, j, 0)),
                   pl.BlockSpec((None, tm, rw), lambda i, j: (i, j, 0))],
        out_shape=[pair_shape, pair_shape, pair_shape, pair_shape,
                   jax.ShapeDtypeStruct((b, t, shift_w), F32),
                   jax.ShapeDtypeStruct((b, t, rw), F32)],
        compiler_params=_cparams("arbitrary", "arbitrary"),
        name="in_proj",
    )(x, shift, scale, norm_w.reshape(1, d), w_in_bf16)


def _t5_bucket_np(dist):
    max_exact = N_BUCKETS // 2
    nf = np.maximum(dist, max_exact).astype(np.float32)
    large = max_exact + (np.log(nf / np.float32(max_exact)) / np.float32(math.log(BUCKET_MAX_DIST / max_exact))
                         * np.float32(N_BUCKETS - max_exact)).astype(np.int32)
    large = np.minimum(large, N_BUCKETS - 1)
    return np.where(dist < max_exact, dist, large)


def _branch_bias(rel_bias):
    out = []
    for window, dil in DILATED_BRANCHES:
        dist = dil * np.arange(window // dil + 1, dtype=np.int32)
        out.append(rel_bias[_t5_bucket_np(dist)].T.astype(F32))
    return jnp.stack(out)


def _prompt_bias_tiles(rel_bias):
    bias = _branch_bias(rel_bias)
    i = np.arange(ATT_BLOCK)[:, None]
    c = np.arange(2 * ATT_BLOCK)[None, :]
    j = ATT_BLOCK + i - c
    valid = (j >= 0) & (j <= KEYS_PER_BRANCH)
    tile = jnp.where(valid[None, None], bias[:, :, np.clip(j, 0, KEYS_PER_BRANCH)], -jnp.inf)
    first = jnp.where((c >= ATT_BLOCK)[None, None], tile, -jnp.inf)
    return jnp.stack([tile, first], axis=1)


def _att_prompt_kernel(q_ref, k_ref, v_ref, bias_ref, o_ref, m_ref, l_ref, acc_ref, *, super_rows):
    sb = pl.program_id(2)
    lane = lax.broadcasted_iota(jnp.int32, (1, LANES), 1)
    head0 = lane < HEAD_DIM
    group = ATT_TILES_PER_ITER

    for bi, (_, dil) in enumerate(DILATED_BRANCHES):
        nblk = super_rows // (ATT_BLOCK * dil)
        rows = lambda s, dil=dil: (pl.ds(s, ATT_BLOCK, stride=dil) if dil > 1 else pl.ds(s, ATT_BLOCK))

        def tiles(it, carry, bi=bi, dil=dil, nblk=nblk, rows=rows):
            locs, firsts, q2, kcat, vcat = [], [], [], [], []
            for j in range(group):
                idx = it * group + j
                res = idx // nblk
                loc = res + dil * ATT_BLOCK * (idx - res * nblk)
                glob = sb * super_rows + loc
                prev = glob - dil * ATT_BLOCK
                first = prev < 0
                pstart = jnp.where(first, glob, prev)
                qt = q_ref[rows(loc), :]
                locs.append(loc)
                firsts.append(first.astype(jnp.int32))
                q2.append(jnp.concatenate([jnp.where(head0, qt, 0.0), jnp.where(head0, 0.0, qt)],
                                          axis=0).astype(BF16))
                kcat.append(jnp.concatenate([k_ref[rows(pstart), :], k_ref[rows(glob), :]], axis=0).astype(BF16))
                vcat.append(jnp.concatenate([v_ref[rows(pstart), :], v_ref[rows(glob), :]], axis=0).astype(BF16))
            s = [_nt(q2[j], kcat[j]) + bias_ref[bi, firsts[j]].reshape(2 * ATT_BLOCK, 2 * ATT_BLOCK)
                 for j in range(group)]
            mt = [jnp.max(x, axis=-1, keepdims=True) for x in s]
            p = [jnp.exp(s[j] - mt[j]) for j in range(group)]
            lt = [jnp.sum(x, axis=-1, keepdims=True) for x in p]
            pv = [_nn(p[j].astype(BF16), vcat[j]) for j in range(group)]
            for j in range(group):
                r = rows(locs[j])
                m_ref[bi, r, :] = jnp.where(head0, mt[j][:ATT_BLOCK], mt[j][ATT_BLOCK:])
                l_ref[bi, r, :] = jnp.where(head0, lt[j][:ATT_BLOCK], lt[j][ATT_BLOCK:])
                acc_ref[bi, r, :] = jnp.where(head0, pv[j][:ATT_BLOCK], pv[j][ATT_BLOCK:])
            return carry

        lax.fori_loop(0, dil * nblk // group, tiles, 0)

    nb = len(DILATED_BRANCHES)
    chunk = 2 * ATT_BLOCK

    def merge(c, carry):
        r = pl.ds(pl.multiple_of(c * chunk, chunk), chunk)
        ms = [m_ref[n, r, :] for n in range(nb)]
        m = functools.reduce(jnp.maximum, ms)
        ws = [jnp.exp(x - m) for x in ms]
        den = functools.reduce(jnp.add, [l_ref[n, r, :] * ws[n] for n in range(nb)])
        num = functools.reduce(jnp.add, [acc_ref[n, r, :] * ws[n] for n in range(nb)])
        o_ref[r, :] = num / den
        return carry

    lax.fori_loop(0, super_rows // chunk, merge, 0)


def _att_prompt(q, k, v, bias_tiles):
    b, npair, t, _ = q.shape
    sr = ATT_SUPER
    nb = len(DILATED_BRANCHES)
    return pl.pallas_call(
        functools.partial(_att_prompt_kernel, super_rows=sr),
        grid=(b, npair, t // sr),
        in_specs=[pl.BlockSpec((None, None, sr, LANES), lambda i, p, s: (i, p, s, 0)),
                  pl.BlockSpec((None, None, t, LANES), lambda i, p, s: (i, p, 0, 0)),
                  pl.BlockSpec((None, None, t, LANES), lambda i, p, s: (i, p, 0, 0)),
                  pl.BlockSpec((nb, 2, 2, ATT_BLOCK, 2 * ATT_BLOCK), lambda i, p, s: (0, 0, p, 0, 0))],
        out_specs=pl.BlockSpec((None, None, sr, LANES), lambda i, p, s: (i, p, s, 0)),
        out_shape=jax.ShapeDtypeStruct((b, npair, t, LANES), F32),
        scratch_shapes=[pltpu.VMEM((nb, sr, LANES), F32)] * 3,
        compiler_params=_cparams("arbitrary", "arbitrary", "arbitrary"),
        name="att_prompt",
    )(q, k, v, bias_tiles)


def _att_sample_kernel(q_ref, kn_ref, vn_ref, kt_ref, vt_ref, bias_ref, b0_ref, o_ref):
    nh, hd, l = kt_ref.shape
    nb = bias_ref.shape[0]
    q = q_ref[...]
    qb = q.astype(BF16)
    head = lax.broadcasted_iota(jnp.int32, (nh, 1), 0)
    logit = jnp.zeros((nh, l), F32)
    for h in range(nh):
        s = _nn(qb, kt_ref[h].astype(BF16))
        logit = jnp.where(head == h, s, logit)
    s0 = jnp.sum(q * kn_ref[...], axis=-1, keepdims=True)
    ls = [logit + bias_ref[n] for n in range(nb)]
    s_self = [s0 + b0_ref[n] for n in range(nb)]
    m = functools.reduce(jnp.maximum, [jnp.max(x, axis=-1, keepdims=True) for x in ls] + s_self)
    p = functools.reduce(jnp.add, [jnp.exp(x - m) for x in ls])
    p0 = functools.reduce(jnp.add, [jnp.exp(x - m) for x in s_self])
    den = jnp.sum(p, axis=-1, keepdims=True) + p0
    pb = p.astype(BF16)
    out = p0 * vn_ref[...]
    for h in range(nh):
        out = out + jnp.where(head == h, _nt(pb, vt_ref[h].astype(BF16)), 0.0)
    o_ref[...] = out / den


def _att_sample(q, k_new, v_new, cache_kt, cache_vt, rel_bias):
    b, nh, hd = q.shape
    l = cache_kt.shape[-1]
    bias = _branch_bias(rel_bias)
    back = l - np.arange(l)
    tabs = []
    for n, (window, dil) in enumerate(DILATED_BRANCHES):
        used = (back % dil == 0) & (back <= window)
        tabs.append(jnp.where(used[None, :], bias[n][:, np.where(used, back // dil, 0)], -jnp.inf))
    bias_pos = jnp.stack(tabs)
    bias0 = bias[:, :, 0][..., None]
    row = pl.BlockSpec((None, nh, hd), lambda i: (i, 0, 0))
    cache = pl.BlockSpec((None, nh, hd, l), lambda i: (i, 0, 0, 0))
    full = lambda a: pl.BlockSpec(a.shape, lambda i: (0,) * a.ndim)
    return pl.pallas_call(
        _att_sample_kernel,
        grid=(b,),
        in_specs=[row, row, row, cache, cache, full(bias_pos), full(bias0)],
        out_specs=row,
        out_shape=jax.ShapeDtypeStruct((b, nh, hd), F32),
        compiler_params=_cparams("arbitrary"),
        name="att_sample",
    )(q, k_new, v_new, cache_kt, cache_vt, bias_pos, bias0)


def _rwkv_features(u, u_prev, mu, w0, wlb, a0, alb, kk_scale, ka, rw):
    um = u + (u_prev - u) * mu
    r = um[:, :rw]
    k = um[:, rw:2 * rw]
    v = um[:, 2 * rw:3 * rw]
    xw = um[:, 3 * rw:3 * rw + LORA_W]
    xa = um[:, 3 * rw + LORA_W:3 * rw + 2 * LORA_W]
    wl = w0 + _nn(jnp.tanh(xw).astype(BF16), wlb)
    w_log = -jax.nn.softplus(-wl) - 0.5
    logw = -jnp.exp(w_log)
    a = jax.nn.sigmoid(a0 + _nn(xa.astype(BF16), alb))
    kk_raw = k * kk_scale
    k2 = k * (1.0 + (a - 1.0) * ka)
    return r, k2, v, logw, kk_raw, a


def _head_normalize(kk_raw_h):
    n2 = jnp.sum(kk_raw_h * kk_raw_h, axis=-1, keepdims=True)
    return kk_raw_h * lax.rsqrt(jnp.maximum(n2, 1e-24))


def _group_norm_bonus(y, r, k2, v, rk, lw, lb):
    mean = jnp.mean(y, axis=-1, keepdims=True)
    var = jnp.mean(jnp.square(y - mean), axis=-1, keepdims=True)
    yn = (y - mean) * lax.rsqrt(var + GN_EPS) * lw + lb
    return yn + jnp.sum(r * k2 * rk, axis=-1, keepdims=True) * v


def _rwkv_prompt_kernel(u_ref, prev0_ref, s0_ref, mu_ref, w0_ref, wlb_ref, a0_ref, alb_ref, kk_ref, ka_ref,
                        rk_ref, lw_ref, lb_ref, y_ref, sout_ref, s_scr, carry_scr, *, rw, prec):
    c = pl.program_id(1)
    nc = pl.num_programs(1)
    rows = u_ref.shape[0]
    ch = RWKV_CHUNK
    ng = rows // ch
    nh = rw // HEAD_DIM
    hd = HEAD_DIM

    @pl.when(c == 0)
    def _():
        s_scr[...] = s0_ref[...]
        carry_scr[...] = prev0_ref[...]

    u = u_ref[...]
    rowi = lax.broadcasted_iota(jnp.int32, (rows, 1), 0)
    u_prev = jnp.where(rowi == 0, carry_scr[...], pltpu.roll(u, 1, 0))
    carry_scr[...] = u[rows - 1:rows, :]
    r, k2, v, logw, kk_raw, a = _rwkv_features(u, u_prev, mu_ref[...], w0_ref[...], wlb_ref[...], a0_ref[...],
                                               alb_ref[...], kk_ref[...], ka_ref[...], rw)

    ti = lax.broadcasted_iota(jnp.int32, (ch, ch), 0)
    si = lax.broadcasted_iota(jnp.int32, (ch, ch), 1)
    tri_incl = (ti >= si).astype(F32)
    t2 = lax.broadcasted_iota(jnp.int32, (2 * ch, 2 * ch), 0)
    s2 = lax.broadcasted_iota(jnp.int32, (2 * ch, 2 * ch), 1)
    tt = jnp.where(t2 >= ch, t2 - ch, t2)
    ss = jnp.where(s2 >= ch, s2 - ch, s2)
    score_mask = (tt - ss) >= jnp.where(t2 >= ch, 0, 1)
    eye = (ti == si).astype(F32)
    zeros_cv = jnp.zeros((ch, hd), F32)

    nu = ng * nh
    at, rt, bt, kt, bk, pc, vh, rh, k2h = [], [], [], [], [], [], [], [], []
    for g in range(ng):
        rs = slice(g * ch, (g + 1) * ch)
        logw_g = logw[rs]
        cum = _nn(tri_incl, logw_g, HIGHEST)
        for h in range(nh):
            sl = slice(h * hd, (h + 1) * hd)
            cum_h = cum[:, sl]
            ctot = cum_h[ch - 1:ch, :]
            kk_h = _head_normalize(kk_raw[rs, sl])
            b_h = kk_h * a[rs, sl]
            e_neg = jnp.exp(-cum_h)
            e_end = jnp.exp(ctot - cum_h)
            at.append(-kk_h * jnp.exp(cum_h - logw_g[:, sl]))
            rt.append(r[rs, sl] * jnp.exp(cum_h))
            bt.append(b_h * e_neg)
            kt.append(k2[rs, sl] * e_neg)
            bk.append(jnp.concatenate([b_h * e_end, k2[rs, sl] * e_end], axis=0))
            pc.append(jnp.exp(ctot))
            vh.append(v[rs, sl])
            rh.append(r[rs, sl])
            k2h.append(k2[rs, sl])

    heads = range(nu)
    sc = [jnp.where(score_mask,
                    _dot(jnp.concatenate([at[h], rt[h]], axis=0), jnp.concatenate([bt[h], kt[h]], axis=0),
                         _NT, prec["score"]), 0.0) for h in heads]
    top = [s[:ch] for s in sc]
    bot = [s[ch:] for s in sc]
    aak_v = [_dot(top[h], jnp.concatenate([zeros_cv, vh[h]], axis=0), _NN, prec["aakv"]) for h in heads]
    ap = [t[:, :ch] for t in top]
    tm = [eye + x for x in ap]
    npow = 1
    while npow * 2 < ch:
        ap = [_dot(x, x, _NN, prec["inv"]) for x in ap]
        tm = [tm[h] + _dot(ap[h], tm[h], _NN, prec["inv"]) for h in heads]
        npow *= 2
    w_m = [_dot(tm[h], at[h], _NN, prec["wu"]) for h in heads]
    u0 = [_dot(tm[h], aak_v[h], _NN, prec["wu"]) for h in heads]
    wz = [jnp.concatenate([w_m[h], zeros_cv], axis=0) for h in heads]
    uv = [jnp.concatenate([u0[h], vh[h]], axis=0) for h in heads]
    qhat = [rt[h] + _dot(bot[h], wz[h], _NN, prec["y"]) for h in heads]
    y_c = [_dot(bot[h], uv[h], _NN, prec["y"]) for h in heads]
    phi_t = [_dot(wz[h], bk[h], _TN, prec["state"]) for h in heads]
    g0_t = [_dot(uv[h], bk[h], _TN, prec["state"]) for h in heads]

    s_cur = [s_scr[h] for h in range(nh)]
    for g in range(ng):
        rs = slice(g * ch, (g + 1) * ch)
        y_s = [_dot(qhat[g * nh + h], s_cur[h], _NT, prec["y"]) for h in range(nh)]
        s_phi = [_dot(s_cur[h], phi_t[g * nh + h], _NN, prec["state"]) for h in range(nh)]
        for h in range(nh):
            i = g * nh + h
            sl = slice(h * hd, (h + 1) * hd)
            s_cur[h] = s_cur[h] * pc[i] + s_phi[h] + g0_t[i]
            y_ref[rs, sl] = _group_norm_bonus(y_s[h] + y_c[i], rh[i], k2h[i], vh[i], rk_ref[:, sl],
                                              lw_ref[:, sl], lb_ref[:, sl])
    for h in range(nh):
        s_scr[h] = s_cur[h]

    @pl.when(c == nc - 1)
    def _():
        sout_ref[...] = s_scr[...]


RWKV_PRECISION = dict(score="bf16", aakv="bf16", inv="bf16", wu="bf16", y="bf16", state="bf16")


def _rwkv_prompt(u, prev0, s0, p, prec=None):
    prec = RWKV_PRECISION if prec is None else prec
    b, t, sw = u.shape
    rw = p["w0"].shape[-1]
    nh = rw // HEAD_DIM
    ch = RWKV_CHUNK * RWKV_CHUNKS_PER_STEP
    vec = lambda a: pl.BlockSpec(a.shape, lambda i, c: (0,) * a.ndim)
    params = [p["mu"], p["w0"], p["wlb"], p["a0"], p["alb"], p["kk"], p["ka"], p["rk"], p["lw"], p["lb"]]
    state_spec = pl.BlockSpec((None, nh, HEAD_DIM, HEAD_DIM), lambda i, c: (i, 0, 0, 0))
    return pl.pallas_call(
        functools.partial(_rwkv_prompt_kernel, rw=rw, prec=prec),
        grid=(b, t // ch),
        in_specs=[pl.BlockSpec((None, ch, sw), lambda i, c: (i, c, 0)),
                  pl.BlockSpec((None, 1, sw), lambda i, c: (i, 0, 0)),
                  state_spec] + [vec(a) for a in params],
        out_specs=[pl.BlockSpec((None, ch, rw), lambda i, c: (i, c, 0)), state_spec],
        out_shape=[jax.ShapeDtypeStruct((b, t, rw), F32),
                   jax.ShapeDtypeStruct((b, nh, HEAD_DIM, HEAD_DIM), F32)],
        scratch_shapes=[pltpu.VMEM((nh, HEAD_DIM, HEAD_DIM), F32), pltpu.VMEM((1, sw), F32)],
        compiler_params=_cparams("arbitrary", "arbitrary"),
        name="rwkv_prompt",
    )(u, prev0, s0, *params)


def _rwkv_feat_kernel(u_ref, prev_ref, mu_ref, w0_ref, wlb_ref, a0_ref, alb_ref, kk_ref, ka_ref,
                      r_ref, k_ref, v_ref, w_ref, kkr_ref, a_ref, *, rw):
    r, k2, v, logw, kk_raw, a = _rwkv_features(u_ref[...], prev_ref[...], mu_ref[...], w0_ref[...], wlb_ref[...],
                                               a0_ref[...], alb_ref[...], kk_ref[...], ka_ref[...], rw)
    r_ref[...] = r.T
    k_ref[...] = k2.T
    v_ref[...] = v.T
    w_ref[...] = jnp.exp(logw).T
    kkr_ref[...] = kk_raw.T
    a_ref[...] = a.T


def _rwkv_step_kernel(s_ref, r_ref, k_ref, v_ref, w_ref, kkr_ref, a_ref, rk_ref, lw_ref, lb_ref,
                      y_ref, sout_ref, y_scr):
    hd = s_ref.shape[0]
    kk_raw = kkr_ref[...]
    kk = kk_raw / jnp.maximum(jnp.sqrt(jnp.sum(kk_raw * kk_raw, axis=0, keepdims=True)), 1e-12)
    beta = kk * a_ref[...]
    w, k2, r, v = w_ref[...], k_ref[...], r_ref[...], v_ref[...]

    def value_row(i, carry):
        s = s_ref[i]
        sa = jnp.sum(s * kk, axis=0, keepdims=True)
        s_new = s * w - sa * beta + v_ref[pl.ds(i, 1), :] * k2
        sout_ref[i] = s_new
        y_scr[pl.ds(i, 1), :] = jnp.sum(s_new * r, axis=0, keepdims=True)
        return carry

    lax.fori_loop(0, hd, value_row, 0)
    y = y_scr[...]
    mean = jnp.mean(y, axis=0, keepdims=True)
    var = jnp.mean(jnp.square(y - mean), axis=0, keepdims=True)
    yn = (y - mean) * lax.rsqrt(var + GN_EPS) * lw_ref[...] + lb_ref[...]
    y_ref[...] = yn + jnp.sum(r * k2 * rk_ref[...], axis=0, keepdims=True) * v


def _rwkv_sample(u, prev, s0_t, p):
    b, sw = u.shape
    rw = p["w0"].shape[-1]
    nh = rw // HEAD_DIM
    hd = HEAD_DIM
    full = lambda a: pl.BlockSpec(a.shape, lambda i: (0,) * a.ndim)
    fparams = [p["mu"], p["w0"], p["wlb"], p["a0"], p["alb"], p["kk"], p["ka"]]
    feats = pl.pallas_call(
        functools.partial(_rwkv_feat_kernel, rw=rw),
        grid=(1,),
        in_specs=[full(u), full(prev)] + [full(a) for a in fparams],
        out_specs=[pl.BlockSpec((rw, b), lambda i: (0, 0))] * 6,
        out_shape=[jax.ShapeDtypeStruct((rw, b), F32)] * 6,
        compiler_params=_cparams("arbitrary"),
        name="rwkv_sample_features",
    )(u, prev, *fparams)
    chan = pl.BlockSpec((hd, b), lambda h: (h, 0))
    col = pl.BlockSpec((hd, 1), lambda h: (h, 0))
    state_spec = pl.BlockSpec((None, hd, hd, b), lambda h: (h, 0, 0, 0))
    cols = [p[n].reshape(rw, 1) for n in ("rk", "lw", "lb")]
    y_t, s_new = pl.pallas_call(
        _rwkv_step_kernel,
        grid=(nh,),
        in_specs=[state_spec] + [chan] * 6 + [col] * 3,
        out_specs=[chan, state_spec],
        out_shape=[jax.ShapeDtypeStruct((rw, b), F32), jax.ShapeDtypeStruct((nh, hd, hd, b), F32)],
        scratch_shapes=[pltpu.VMEM((hd, b), F32)],
        compiler_params=_cparams("arbitrary"),
        name="rwkv_sample_step",
    )(s0_t, *feats, *cols)
    return y_t.T, s_new


def _outproj_kernel(x_ref, att_ref, g_ref, rw_ref, gr_ref, gate_ref, w_ref, fnw_ref, o_ref, *, final_norm):
    npair = att_ref.shape[0]
    acc = _nn((rw_ref[...] * _silu(gr_ref[...])).astype(BF16), w_ref[npair * LANES:, :])
    for p in range(npair):
        m = att_ref[p] * _silu(g_ref[p])
        acc = acc + _nn(m.astype(BF16), w_ref[p * LANES:(p + 1) * LANES, :])
    xo = x_ref[...] + gate_ref[...] * acc
    if final_norm:
        xo = xo * lax.rsqrt(jnp.mean(xo * xo, axis=-1, keepdims=True) + NORM_EPS) * fnw_ref[...]
    o_ref[...] = xo


def _outproj(x, att, g_att, rwk, g_rwkv, gate, w_out_bf16, final_norm_w, final_norm):
    b, t, d = x.shape
    npair = att.shape[1]
    rw = rwk.shape[-1]
    tm = min(ROW_TILE, t)
    per_row = gate.shape[1] != 1
    gate_spec = (pl.BlockSpec((None, tm, d), lambda i, j: (i, j, 0)) if per_row
                 else pl.BlockSpec((None, 1, d), lambda i, j: (i, 0, 0)))
    pair_spec = pl.BlockSpec((None, npair, tm, LANES), lambda i, j: (i, 0, j, 0))
    return pl.pallas_call(
        functools.partial(_outproj_kernel, final_norm=final_norm),
        grid=(b, t // tm),
        in_specs=[pl.BlockSpec((None, tm, d), lambda i, j: (i, j, 0)), pair_spec, pair_spec,
                  pl.BlockSpec((None, tm, rw), lambda i, j: (i, j, 0)),
                  pl.BlockSpec((None, tm, rw), lambda i, j: (i, j, 0)),
                  gate_spec,
                  pl.BlockSpec(w_out_bf16.shape, lambda i, j: (0, 0)),
                  pl.BlockSpec((1, d), lambda i, j: (0, 0))],
        out_specs=pl.BlockSpec((None, tm, d), lambda i, j: (i, j, 0)),
        out_shape=jax.ShapeDtypeStruct((b, t, d), F32),
        compiler_params=_cparams("arbitrary", "arbitrary"),
        name="out_proj",
    )(x, att, g_att, rwk, g_rwkv, gate, w_out_bf16, final_norm_w.reshape(1, d))


def _pairs_to_heads(a):
    b, npair, t, _ = a.shape
    return a.transpose(0, 2, 1, 3).reshape(b, t, npair * (LANES // HEAD_DIM), HEAD_DIM)


def kernel(x_prompt, x_sample, cache_win_k, cache_win_v, state_wkv, state_shift, c_prompt, c_sample, rel_bias, norm_w, ada_w, ada_b, w_in, mu_shift, w0, w_lora_b, a0, a_lora_b, k_k, k_a, r_k, ln_x_w, ln_x_b, w_out, final_norm_w):
    depth = norm_w.shape[0]
    bp, tp, d = x_prompt.shape
    bs, ts, _ = x_sample.shape
    assert ts == 1, "the sample group decodes one token per sequence"
    rw = w0.shape[-1]
    att_w = w_out.shape[1] - rw
    shift_w = mu_shift.shape[-1]
    nh_att = att_w // HEAD_DIM
    nh_rw = rw // HEAD_DIM
    assert tp % ATT_SUPER == 0 and tp % (RWKV_CHUNK * RWKV_CHUNKS_PER_STEP) == 0 and bs % 8 == 0
    assert cache_win_k.shape[2] == MAX_WINDOW
    keep = min(MAX_WINDOW, tp)

    bias_tiles = _prompt_bias_tiles(rel_bias)
    c_all = jnp.concatenate([c_prompt, c_sample], axis=0)
    npad = -c_all.shape[0] % 8
    c_all = jnp.pad(c_all, ((0, npad), (0, 0)))

    xp = x_prompt
    xs = x_sample.reshape(1, bs, d)
    outs = [[] for _ in range(8)]
    for l in range(depth):
        mod = _ada_mod(c_all, ada_w[l], ada_b[l])
        shift, scale, gate = jnp.split(mod, 3, axis=-1)
        w_in_b = w_in[l].astype(BF16)
        w_out_b = w_out[l].astype(BF16)
        p = dict(mu=mu_shift[l].reshape(1, -1), w0=w0[l].reshape(1, -1), wlb=w_lora_b[l].astype(BF16),
                 a0=a0[l].reshape(1, -1), alb=a_lora_b[l].astype(BF16), kk=k_k[l].reshape(1, -1),
                 ka=k_a[l].reshape(1, -1), rk=r_k[l].reshape(1, -1), lw=ln_x_w[l].reshape(1, -1),
                 lb=ln_x_b[l].reshape(1, -1))
        last = l == depth - 1

        pm = lambda a: a[:bp].reshape(bp, 1, d)
        q, k, v, g_att, u, g_rwkv = _inproj(xp, pm(shift), pm(scale), norm_w[l], w_in_b, att_w, shift_w)
        att = _att_prompt(q, k, v, bias_tiles)
        y_rw, s_p = _rwkv_prompt(u, jnp.zeros((bp, 1, shift_w), F32),
                                 jnp.zeros((bp, nh_rw, HEAD_DIM, HEAD_DIM), F32), p)
        xp = _outproj(xp, att, g_att, y_rw, g_rwkv, pm(gate), w_out_b, final_norm_w, last)
        outs[0].append(_pairs_to_heads(k[:, :, tp - keep:, :]))
        outs[1].append(_pairs_to_heads(v[:, :, tp - keep:, :]))
        outs[4].append(s_p)
        outs[6].append(u[:, -1])

        sm = lambda a: a[bp:bp + bs].reshape(1, bs, d)
        q, k, v, g_att, u, g_rwkv = _inproj(xs, sm(shift), sm(scale), norm_w[l], w_in_b, att_w, shift_w)
        nat = lambda a: a[0].transpose(1, 0, 2).reshape(bs, nh_att, HEAD_DIM)
        k_new, v_new = nat(k), nat(v)
        att_s = _att_sample(nat(q), k_new, v_new, cache_win_k[l].transpose(0, 2, 3, 1),
                            cache_win_v[l].transpose(0, 2, 3, 1), rel_bias)
        att_s = att_s.reshape(bs, att_w // LANES, LANES).transpose(1, 0, 2)[None]
        y_rw, s_s = _rwkv_sample(u[0], state_shift[l], state_wkv[l].transpose(1, 2, 3, 0), p)
        xs = _outproj(xs, att_s, g_att, y_rw[None], g_rwkv, sm(gate), w_out_b, final_norm_w, last)
        outs[2].append(k_new.reshape(bs, 1, nh_att, HEAD_DIM))
        outs[3].append(v_new.reshape(bs, 1, nh_att, HEAD_DIM))
        outs[5].append(s_s.transpose(3, 0, 1, 2))
        outs[7].append(u[0])

    stack = lambda i: jnp.stack(outs[i])
    return (xp, xs.reshape(bs, 1, d), stack(0), stack(1), stack(2), stack(3), stack(4), stack(5), stack(6),
            stack(7))
```

```python
import functools
import math

import numpy as np
import jax
import jax.numpy as jnp
from jax import lax
from jax.experimental import pallas as pl
from jax.experimental.pallas import tpu as pltpu

F32 = jnp.float32
BF16 = jnp.bfloat16
HIGHEST = lax.Precision.HIGHEST

HEAD_DIM = 64
DILATED_BRANCHES = ((128, 1), (512, 4), (2048, 16))
MAX_WINDOW = max(w for w, _ in DILATED_BRANCHES)
KEYS_PER_BRANCH = 128
N_BUCKETS = 32
BUCKET_MAX_DIST = MAX_WINDOW
LORA_W = 64
NORM_EPS = 1e-6
GN_EPS = HEAD_DIM * 1e-5

LANES = 128
VMEM_LIMIT_BYTES = 56 * 1024 * 1024

ROW_TILE = 256
ATT_BLOCK = 128
ATT_SUPER = MAX_WINDOW
ATT_TILES_PER_ITER = 4
RWKV_CHUNK = 64
RWKV_CHUNKS_PER_STEP = 4


def _cparams(*sem):
    return pltpu.CompilerParams(dimension_semantics=sem, vmem_limit_bytes=VMEM_LIMIT_BYTES)


def _silu(x):
    return x * jax.nn.sigmoid(x)


def _nt(a, b):
    return lax.dot_general(a, b, (((1,), (1,)), ((), ())), preferred_element_type=F32)


def _tn(a, b):
    return lax.dot_general(a, b, (((0,), (0,)), ((), ())), preferred_element_type=F32)


def _nn(a, b, precision=None):
    return jnp.dot(a, b, precision=precision, preferred_element_type=F32)


def _ada_kernel(c_ref, w_ref, b_ref, o_ref):
    s = _silu(c_ref[...])
    o_ref[...] = _nn(s, w_ref[...], HIGHEST) + b_ref[...]


def _ada_mod(c_all, ada_w, ada_b):
    n, d = c_all.shape
    e = ada_w.shape[1]
    tn = 512
    return pl.pallas_call(
        _ada_kernel,
        grid=(e // tn,),
        in_specs=[pl.BlockSpec((n, d), lambda j: (0, 0)),
                  pl.BlockSpec((d, tn), lambda j: (0, j)),
                  pl.BlockSpec((1, tn), lambda j: (0, j))],
        out_specs=pl.BlockSpec((n, tn), lambda j: (0, j)),
        out_shape=jax.ShapeDtypeStruct((n, e), F32),
        compiler_params=_cparams("arbitrary"),
        name="ada_mod",
    )(c_all, ada_w, ada_b.reshape(1, e))


def _inproj_kernel(x_ref, shift_ref, scale_ref, nw_ref, w_ref,
                   q_ref, k_ref, v_ref, g_ref, u_ref, gr_ref, *, att_w, shift_w):
    x = x_ref[...]
    xn = x * lax.rsqrt(jnp.mean(x * x, axis=-1, keepdims=True) + NORM_EPS) * nw_ref[...]
    h = xn * (1.0 + scale_ref[...]) + shift_ref[...]
    z = _nn(h.astype(BF16), w_ref[...])
    npair = att_w // LANES
    for p in range(npair):
        q_ref[p] = z[:, p * LANES:(p + 1) * LANES] * (HEAD_DIM ** -0.5)
        k_ref[p] = z[:, att_w + p * LANES:att_w + (p + 1) * LANES]
        v_ref[p] = z[:, 2 * att_w + p * LANES:2 * att_w + (p + 1) * LANES]
        g_ref[p] = z[:, 3 * att_w + p * LANES:3 * att_w + (p + 1) * LANES]
    u_ref[...] = z[:, 4 * att_w:4 * att_w + shift_w]
    gr_ref[...] = z[:, 4 * att_w + shift_w:]


def _inproj(x, shift, scale, norm_w, w_in_bf16, att_w, shift_w):
    b, t, d = x.shape
    in_w = w_in_bf16.shape[1]
    rw = in_w - 4 * att_w - shift_w
    npair = att_w // LANES
    tm = min(ROW_TILE, t)
    per_row = shift.shape[1] != 1
    mod_spec = (pl.BlockSpec((None, tm, d), lambda i, j: (i, j, 0)) if per_row
                else pl.BlockSpec((None, 1, d), lambda i, j: (i, 0, 0)))
    pair_spec = pl.BlockSpec((None, npair, tm, LANES), lambda i, j: (i, 0, j, 0))
    pair_shape = jax.ShapeDtypeStruct((b, npair, t, LANES), F32)
    return pl.pallas_call(
        functools.partial(_inproj_kernel, att_w=att_w, shift_w=shift_w),
        grid=(b, t // tm),
        in_specs=[pl.BlockSpec((None, tm, d), lambda i, j: (i, j, 0)), mod_spec, mod_spec,
                  pl.BlockSpec((1, d), lambda i, j: (0, 0)),
                  pl.BlockSpec((d, in_w), lambda i, j: (0, 0))],
        out_specs=[pair_spec, pair_spec, pair_spec, pair_spec,
                   pl.BlockSpec((None, tm, shift_w), lambda i, j: (i, j, 0)),
                   pl.BlockSpec((None, tm, rw), lambda i, j: (i, j, 0))],
        out_shape=[pair_shape, pair_shape, pair_shape, pair_shape,
                   jax.ShapeDtypeStruct((b, t, shift_w), F32),
                   jax.ShapeDtypeStruct((b, t, rw), F32)],
        compiler_params=_cparams("arbitrary", "arbitrary"),
        name="in_proj",
    )(x, shift, scale, norm_w.reshape(1, d), w_in_bf16)


def _t5_bucket_np(dist):
    max_exact = N_BUCKETS // 2
    nf = np.maximum(dist, max_exact).astype(np.float32)
    large = max_exact + (np.log(nf / np.float32(max_exact)) / np.float32(math.log(BUCKET_MAX_DIST / max_exact))
                         * np.float32(N_BUCKETS - max_exact)).astype(np.int32)
    large = np.minimum(large, N_BUCKETS - 1)
    return np.where(dist < max_exact, dist, large)


def _branch_bias(rel_bias):
    out = []
    for window, dil in DILATED_BRANCHES:
        dist = dil * np.arange(window // dil + 1, dtype=np.int32)
        out.append(rel_bias[_t5_bucket_np(dist)].T.astype(F32))
    return jnp.stack(out)


def _prompt_bias_tiles(rel_bias):
    bias = _branch_bias(rel_bias)
    nb, nh, _ = bias.shape
    blk, width = ATT_BLOCK, 2 * ATT_BLOCK
    pad = jnp.full((nb, nh, blk - 1), -jnp.inf, F32)
    strip = jnp.flip(jnp.concatenate([pad, bias, pad], axis=-1), axis=-1)
    length = 3 * blk - 1
    rows = jnp.broadcast_to(jnp.pad(strip, ((0, 0), (0, 0), (0, 1)))[:, :, None, :], (nb, nh, blk, length + 1))
    skew = rows.reshape(nb, nh, blk * (length + 1))[:, :, :blk * length].reshape(nb, nh, blk, length)
    tile = skew[:, :, :, blk - 1:blk - 1 + width]
    first = jnp.where((np.arange(width) >= blk)[None, None, None, :], tile, -jnp.inf)
    return jnp.stack([tile, first], axis=1)


def _att_prompt_kernel(q_ref, k_ref, v_ref, bias_ref, o_ref, m_ref, l_ref, acc_ref, *, super_rows):
    sb = pl.program_id(2)
    lane = lax.broadcasted_iota(jnp.int32, (1, LANES), 1)
    head0 = lane < HEAD_DIM
    group = ATT_TILES_PER_ITER

    for bi, (_, dil) in enumerate(DILATED_BRANCHES):
        nblk = super_rows // (ATT_BLOCK * dil)
        rows = lambda s, dil=dil: (pl.ds(s, ATT_BLOCK, stride=dil) if dil > 1 else pl.ds(s, ATT_BLOCK))

        def tiles(it, carry, bi=bi, dil=dil, nblk=nblk, rows=rows):
            locs, firsts, q2, kcat, vcat = [], [], [], [], []
            for j in range(group):
                idx = it * group + j
                res = idx // nblk
                loc = res + dil * ATT_BLOCK * (idx - res * nblk)
                glob = sb * super_rows + loc
                prev = glob - dil * ATT_BLOCK
                first = prev < 0
                pstart = jnp.where(first, glob, prev)
                qt = q_ref[rows(loc), :]
                locs.append(loc)
                firsts.append(first.astype(jnp.int32))
                q2.append(jnp.concatenate([jnp.where(head0, qt, 0.0), jnp.where(head0, 0.0, qt)],
                                          axis=0).astype(BF16))
                kcat.append(jnp.concatenate([k_ref[rows(pstart), :], k_ref[rows(glob), :]], axis=0).astype(BF16))
                vcat.append(jnp.concatenate([v_ref[rows(pstart), :], v_ref[rows(glob), :]], axis=0).astype(BF16))
            s = [_nt(q2[j], kcat[j]) + bias_ref[bi, firsts[j]].reshape(2 * ATT_BLOCK, 2 * ATT_BLOCK)
                 for j in range(group)]
            mt = [jnp.max(x, axis=-1, keepdims=True) for x in s]
            p = [jnp.exp(s[j] - mt[j]) for j in range(group)]
            lt = [jnp.sum(x, axis=-1, keepdims=True) for x in p]
            pv = [_nn(p[j].astype(BF16), vcat[j]) for j in range(group)]
            for j in range(group):
                r = rows(locs[j])
                m_ref[bi, r, :] = jnp.where(head0, mt[j][:ATT_BLOCK], mt[j][ATT_BLOCK:])
                l_ref[bi, r, :] = jnp.where(head0, lt[j][:ATT_BLOCK], lt[j][ATT_BLOCK:])
                acc_ref[bi, r, :] = jnp.where(head0, pv[j][:ATT_BLOCK], pv[j][ATT_BLOCK:])
            return carry

        lax.fori_loop(0, dil * nblk // group, tiles, 0)

    nb = len(DILATED_BRANCHES)
    chunk = 2 * ATT_BLOCK

    def merge(c, carry):
        r = pl.ds(pl.multiple_of(c * chunk, chunk), chunk)
        ms = [m_ref[n, r, :] for n in range(nb)]
        m = functools.reduce(jnp.maximum, ms)
        ws = [jnp.exp(x - m) for x in ms]
        den = functools.reduce(jnp.add, [l_ref[n, r, :] * ws[n] for n in range(nb)])
        num = functools.reduce(jnp.add, [acc_ref[n, r, :] * ws[n] for n in range(nb)])
        o_ref[r, :] = num / den
        return carry

    lax.fori_loop(0, super_rows // chunk, merge, 0)


def _att_prompt(q, k, v, bias_tiles):
    b, npair, t, _ = q.shape
    sr = ATT_SUPER
    nb = len(DILATED_BRANCHES)
    return pl.pallas_call(
        functools.partial(_att_prompt_kernel, super_rows=sr),
        grid=(b, npair, t // sr),
        in_specs=[pl.BlockSpec((None, None, sr, LANES), lambda i, p, s: (i, p, s, 0)),
                  pl.BlockSpec((None, None, t, LANES), lambda i, p, s: (i, p, 0, 0)),
                  pl.BlockSpec((None, None, t, LANES), lambda i, p, s: (i, p, 0, 0)),
                  pl.BlockSpec((nb, 2, 2, ATT_BLOCK, 2 * ATT_BLOCK), lambda i, p, s: (0, 0, p, 0, 0))],
        out_specs=pl.BlockSpec((None, None, sr, LANES), lambda i, p, s: (i, p, s, 0)),
        out_shape=jax.ShapeDtypeStruct((b, npair, t, LANES), F32),
        scratch_shapes=[pltpu.VMEM((nb, sr, LANES), F32)] * 3,
        compiler_params=_cparams("arbitrary", "arbitrary", "arbitrary"),
        name="att_prompt",
    )(q, k, v, bias_tiles)


def _att_sample_kernel(q_ref, kn_ref, vn_ref, kt_ref, vt_ref, bias_ref, b0_ref, o_ref):
    nh, hd, l = kt_ref.shape
    nb = bias_ref.shape[0]
    q = q_ref[...]
    qb = q.astype(BF16)
    head = lax.broadcasted_iota(jnp.int32, (nh, 1), 0)
    logit = jnp.zeros((nh, l), F32)
    for h in range(nh):
        s = _nn(qb, kt_ref[h].astype(BF16))
        logit = jnp.where(head == h, s, logit)
    s0 = jnp.sum(q * kn_ref[...], axis=-1, keepdims=True)
    ls = [logit + bias_ref[n] for n in range(nb)]
    s_self = [s0 + b0_ref[n] for n in range(nb)]
    m = functools.reduce(jnp.maximum, [jnp.max(x, axis=-1, keepdims=True) for x in ls] + s_self)
    p = functools.reduce(jnp.add, [jnp.exp(x - m) for x in ls])
    p0 = functools.reduce(jnp.add, [jnp.exp(x - m) for x in s_self])
    den = jnp.sum(p, axis=-1, keepdims=True) + p0
    pb = p.astype(BF16)
    out = p0 * vn_ref[...]
    for h in range(nh):
        out = out + jnp.where(head == h, _nt(pb, vt_ref[h].astype(BF16)), 0.0)
    o_ref[...] = out / den


def _att_sample(q, k_new, v_new, cache_kt, cache_vt, rel_bias):
    b, nh, hd = q.shape
    l = cache_kt.shape[-1]
    bias = _branch_bias(rel_bias)
    back = l - np.arange(l)
    tabs = []
    for n, (window, dil) in enumerate(DILATED_BRANCHES):
        used = (back % dil == 0) & (back <= window)
        tabs.append(jnp.where(used[None, :], bias[n][:, np.where(used, back // dil, 0)], -jnp.inf))
    bias_pos = jnp.stack(tabs)
    bias0 = bias[:, :, 0][..., None]
    row = pl.BlockSpec((None, nh, hd), lambda i: (i, 0, 0))
    cache = pl.BlockSpec((None, nh, hd, l), lambda i: (i, 0, 0, 0))
    full = lambda a: pl.BlockSpec(a.shape, lambda i: (0,) * a.ndim)
    return pl.pallas_call(
        _att_sample_kernel,
        grid=(b,),
        in_specs=[row, row, row, cache, cache, full(bias_pos), full(bias0)],
        out_specs=row,
        out_shape=jax.ShapeDtypeStruct((b, nh, hd), F32),
        compiler_params=_cparams("arbitrary"),
        name="att_sample",
    )(q, k_new, v_new, cache_kt, cache_vt, bias_pos, bias0)


def _rwkv_features(u, u_prev, mu, w0, wlb, a0, alb, kk_scale, ka, rw):
    um = u + (u_prev - u) * mu
    r = um[:, :rw]
    k = um[:, rw:2 * rw]
    v = um[:, 2 * rw:3 * rw]
    xw = um[:, 3 * rw:3 * rw + LORA_W]
    xa = um[:, 3 * rw + LORA_W:3 * rw + 2 * LORA_W]
    wl = w0 + _nn(jnp.tanh(xw).astype(BF16), wlb)
    w_log = -jax.nn.softplus(-wl) - 0.5
    logw = -jnp.exp(w_log)
    a = jax.nn.sigmoid(a0 + _nn(xa.astype(BF16), alb))
    kk_raw = k * kk_scale
    k2 = k * (1.0 + (a - 1.0) * ka)
    return r, k2, v, logw, kk_raw, a


def _rwkv_prompt_kernel(u_ref, prev0_ref, s0_ref, mu_ref, w0_ref, wlb_ref, a0_ref, alb_ref, kk_ref, ka_ref,
                        rk_ref, lw_ref, lb_ref, y_ref, sout_ref, s_scr, carry_scr, *, rw):
    c = pl.program_id(1)
    nc = pl.num_programs(1)
    rows = u_ref.shape[0]
    ch = RWKV_CHUNK
    ng = rows // ch
    nh = rw // HEAD_DIM
    hd = HEAD_DIM
    ntile = rw // LANES

    @pl.when(c == 0)
    def _():
        s_scr[...] = jnp.zeros(s_scr.shape, F32)
        for h in range(nh):
            s_scr[h, :, (h % 2) * hd:(h % 2 + 1) * hd] = s0_ref[h]
        carry_scr[...] = prev0_ref[...]

    u = u_ref[...]
    shifted = pltpu.roll(u, 1, 0)
    row8 = lax.broadcasted_iota(jnp.int32, (8, 1), 0)
    u_prev = jnp.concatenate([jnp.where(row8 == 0, carry_scr[...], shifted[:8]), shifted[8:]], axis=0)
    carry_scr[...] = u[rows - 1:rows, :]
    r, k2, v, logw, kk_raw, a = _rwkv_features(u, u_prev, mu_ref[...], w0_ref[...], wlb_ref[...], a0_ref[...],
                                               alb_ref[...], kk_ref[...], ka_ref[...], rw)
    lane = lax.broadcasted_iota(jnp.int32, (1, LANES), 1)
    left = lane < hd

    def head_sum(x):
        tiles = []
        for q in range(ntile):
            t = x[:, q * LANES:(q + 1) * LANES]
            s_left = jnp.sum(jnp.where(left, t, 0.0), axis=-1, keepdims=True)
            s_right = jnp.sum(jnp.where(left, 0.0, t), axis=-1, keepdims=True)
            tiles.append(jnp.where(left, s_left, s_right))
        return jnp.concatenate(tiles, axis=1)

    kk = kk_raw * lax.rsqrt(jnp.maximum(head_sum(kk_raw * kk_raw), 1e-24))
    beta = kk * a
    v_x = pltpu.roll(v, hd, 1)

    ti = lax.broadcasted_iota(jnp.int32, (ch, ch), 0)
    si = lax.broadcasted_iota(jnp.int32, (ch, ch), 1)
    tri_b = (ti >= si).astype(BF16)
    row_c = lax.broadcasted_iota(jnp.int32, (ch, 1), 0)
    t2 = lax.broadcasted_iota(jnp.int32, (2 * ch, 2 * ch), 0)
    s2 = lax.broadcasted_iota(jnp.int32, (2 * ch, 2 * ch), 1)
    tt = jnp.where(t2 >= ch, t2 - ch, t2)
    ss = jnp.where(s2 >= ch, s2 - ch, s2)
    score_mask = (tt - ss) >= jnp.where(t2 >= ch, 0, 1)
    tcol = lax.broadcasted_iota(jnp.int32, (ch, LANES), 1)
    trow = lax.broadcasted_iota(jnp.int32, (ch, LANES), 0)
    eye_right = (tcol == trow + ch).astype(F32)
    zeros_b = jnp.zeros((ch, LANES), BF16)
    own_of = [left if h % 2 == 0 else jnp.logical_not(left) for h in range(nh)]
    tile_of = [slice((h // 2) * LANES, (h // 2 + 1) * LANES) for h in range(nh)]
    xtile_of = [slice((((h + 1) // 2) % ntile) * LANES, (((h + 1) // 2) % ntile + 1) * LANES) for h in range(nh)]

    nu = ng * nh
    lhs, rhs, bk, vx, at_b, rt_own, pc = [], [], [], [], [], [], []
    for g in range(ng):
        rs = slice(g * ch, (g + 1) * ch)
        l1 = logw[rs].astype(BF16)
        rem = logw[rs] - l1.astype(F32)
        l2 = rem.astype(BF16)
        l3 = (rem - l2.astype(F32)).astype(BF16)
        cum3 = _nn(tri_b, jnp.concatenate([l1, l2, l3], axis=1))
        cum = cum3[:, :rw] + (cum3[:, rw:2 * rw] + cum3[:, 2 * rw:])
        ctot = cum[ch - 1:ch, :]
        pc_g = jnp.exp(ctot)
        e_cur = jnp.exp(cum)
        e_neg = jnp.exp(-cum)
        e_prev = jnp.where(row_c == 0, 1.0, pltpu.roll(e_cur, 1, 0))
        e_end = pc_g * e_neg
        rt_f = r[rs] * e_cur
        at_g = (-kk[rs] * e_prev).astype(BF16)
        rt_g = rt_f.astype(BF16)
        bt_g = (beta[rs] * e_neg).astype(BF16)
        kt_g = (k2[rs] * e_neg).astype(BF16)
        bh_g = (beta[rs] * e_end).astype(BF16)
        kh_g = (k2[rs] * e_end).astype(BF16)
        vx_g = v_x[rs].astype(BF16)
        for h in range(nh):
            own, tl = own_of[h], tile_of[h]
            zb = jnp.zeros((), BF16)
            lhs.append(jnp.concatenate([jnp.where(own, at_g[:, tl], zb), jnp.where(own, rt_g[:, tl], zb)], axis=0))
            rhs.append(jnp.concatenate([bt_g[:, tl], kt_g[:, tl]], axis=0))
            bk.append(jnp.concatenate([jnp.where(own, bh_g[:, tl], zb), jnp.where(own, kh_g[:, tl], zb)], axis=0))
            vx.append(jnp.where(own, zb, vx_g[:, xtile_of[h]]))
            at_b.append(at_g[:, tl])
            rt_own.append(jnp.where(own, rt_f[:, tl], 0.0))
            pc.append(pc_g[:, tl])

    units = range(nu)
    own_u = [own_of[i % nh] for i in units]
    sc = [jnp.where(score_mask, _nt(lhs[i], rhs[i]), 0.0) for i in units]
    top_b = [sc[i][:ch].astype(BF16) for i in units]
    bot_b = [sc[i][ch:].astype(BF16) for i in units]
    aakv = [_nn(top_b[i], jnp.concatenate([zeros_b, vx[i]], axis=0)) for i in units]
    x = [jnp.where(left, sc[i][:ch], eye_right) for i in units]
    npow = 1
    while npow < ch:
        xb = [x[i].astype(BF16) for i in units]
        x = [_nn(xb[i], jnp.concatenate([xb[i], zeros_b], axis=0)) + jnp.where(left, 0.0, x[i]) for i in units]
        npow *= 2
    z_b = [jnp.where(own_u[i], at_b[i], aakv[i].astype(BF16)) for i in units]
    wu = [_nn(x[i].astype(BF16), jnp.concatenate([zeros_b, z_b[i]], axis=0)) for i in units]
    rhs2 = [jnp.concatenate([wu[i].astype(BF16), vx[i]], axis=0) for i in units]
    qy = [_nn(bot_b[i], rhs2[i]) for i in units]
    pg = [_tn(rhs2[i], bk[i]) for i in units]

    bonus = head_sum(r * k2 * rk_ref[...]) * v
    s_cur = [s_scr[h] for h in range(nh)]
    y_sw = []
    for g in range(ng):
        y_u = []
        for h in range(nh):
            i = g * nh + h
            own = own_of[h]
            s_b = s_cur[h].astype(BF16)
            s_pad = jnp.concatenate([zeros_b, s_b] if h % 2 == 0 else [s_b, zeros_b], axis=0)
            qhat_b = jnp.where(own, rt_own[i] + qy[i], 0.0).astype(BF16)
            y_u.append(jnp.where(own, 0.0, qy[i] + _nt(qhat_b, s_pad)))
            g0 = pg[i][ch:] if h % 2 == 0 else pg[i][:ch]
            s_cur[h] = s_cur[h] * pc[i] + _nn(s_b, pg[i].astype(BF16)) + g0
        y_sw.append(jnp.concatenate([y_u[(2 * q - 1) % nh] + y_u[2 * q] for q in range(ntile)], axis=1))
    for h in range(nh):
        s_scr[h] = s_cur[h]
    y = pltpu.roll(jnp.concatenate(y_sw, axis=0), rw - hd, 1)
    mean = head_sum(y) * (1.0 / hd)
    dev = y - mean
    var = head_sum(dev * dev) * (1.0 / hd)
    y_ref[...] = dev * lax.rsqrt(var + GN_EPS) * lw_ref[...] + lb_ref[...] + bonus

    @pl.when(c == nc - 1)
    def _():
        for h in range(nh):
            sout_ref[h] = s_scr[h, :, (h % 2) * hd:(h % 2 + 1) * hd]


def _rwkv_prompt(u, prev0, s0, p):
    b, t, sw = u.shape
    rw = p["w0"].shape[-1]
    nh = rw // HEAD_DIM
    hd = HEAD_DIM
    rows = RWKV_CHUNK * RWKV_CHUNKS_PER_STEP
    vec = lambda a: pl.BlockSpec(a.shape, lambda i, c: (0,) * a.ndim)
    params = [p["mu"], p["w0"], p["wlb"], p["a0"], p["alb"], p["kk"], p["ka"], p["rk"], p["lw"], p["lb"]]
    state_spec = pl.BlockSpec((None, nh, hd, hd), lambda i, c: (i, 0, 0, 0))
    return pl.pallas_call(
        functools.partial(_rwkv_prompt_kernel, rw=rw),
        grid=(b, t // rows),
        in_specs=[pl.BlockSpec((None, rows, sw), lambda i, c: (i, c, 0)),
                  pl.BlockSpec((None, 1, sw), lambda i, c: (i, 0, 0)),
                  state_spec] + [vec(a) for a in params],
        out_specs=[pl.BlockSpec((None, rows, rw), lambda i, c: (i, c, 0)), state_spec],
        out_shape=[jax.ShapeDtypeStruct((b, t, rw), F32), jax.ShapeDtypeStruct((b, nh, hd, hd), F32)],
        scratch_shapes=[pltpu.VMEM((nh, hd, LANES), F32), pltpu.VMEM((1, sw), F32)],
        compiler_params=_cparams("arbitrary", "arbitrary"),
        name="rwkv_prompt",
    )(u, prev0, s0, *params)


def _rwkv_feat_kernel(u_ref, prev_ref, mu_ref, w0_ref, wlb_ref, a0_ref, alb_ref, kk_ref, ka_ref,
                      r_ref, k_ref, v_ref, w_ref, kkr_ref, a_ref, *, rw):
    r, k2, v, logw, kk_raw, a = _rwkv_features(u_ref[...], prev_ref[...], mu_ref[...], w0_ref[...], wlb_ref[...],
                                               a0_ref[...], alb_ref[...], kk_ref[...], ka_ref[...], rw)
    r_ref[...] = r.T
    k_ref[...] = k2.T
    v_ref[...] = v.T
    w_ref[...] = jnp.exp(logw).T
    kkr_ref[...] = kk_raw.T
    a_ref[...] = a.T


def _rwkv_step_kernel(s_ref, r_ref, k_ref, v_ref, w_ref, kkr_ref, a_ref, rk_ref, lw_ref, lb_ref,
                      y_ref, sout_ref, y_scr):
    hd = s_ref.shape[0]
    kk_raw = kkr_ref[...]
    kk = kk_raw / jnp.maximum(jnp.sqrt(jnp.sum(kk_raw * kk_raw, axis=0, keepdims=True)), 1e-12)
    beta = kk * a_ref[...]
    w, k2, r, v = w_ref[...], k_ref[...], r_ref[...], v_ref[...]

    def value_row(i, carry):
        s = s_ref[i]
        sa = jnp.sum(s * kk, axis=0, keepdims=True)
        s_new = s * w - sa * beta + v_ref[pl.ds(i, 1), :] * k2
        sout_ref[i] = s_new
        y_scr[pl.ds(i, 1), :] = jnp.sum(s_new * r, axis=0, keepdims=True)
        return carry

    lax.fori_loop(0, hd, value_row, 0)
    y = y_scr[...]
    mean = jnp.mean(y, axis=0, keepdims=True)
    var = jnp.mean(jnp.square(y - mean), axis=0, keepdims=True)
    yn = (y - mean) * lax.rsqrt(var + GN_EPS) * lw_ref[...] + lb_ref[...]
    y_ref[...] = yn + jnp.sum(r * k2 * rk_ref[...], axis=0, keepdims=True) * v


def _rwkv_sample(u, prev, s0_t, p):
    b, sw = u.shape
    rw = p["w0"].shape[-1]
    nh = rw // HEAD_DIM
    hd = HEAD_DIM
    full = lambda a: pl.BlockSpec(a.shape, lambda i: (0,) * a.ndim)
    fparams = [p["mu"], p["w0"], p["wlb"], p["a0"], p["alb"], p["kk"], p["ka"]]
    feats = pl.pallas_call(
        functools.partial(_rwkv_feat_kernel, rw=rw),
        grid=(1,),
        in_specs=[full(u), full(prev)] + [full(a) for a in fparams],
        out_specs=[pl.BlockSpec((rw, b), lambda i: (0, 0))] * 6,
        out_shape=[jax.ShapeDtypeStruct((rw, b), F32)] * 6,
        compiler_params=_cparams("arbitrary"),
        name="rwkv_sample_features",
    )(u, prev, *fparams)
    chan = pl.BlockSpec((hd, b), lambda h: (h, 0))
    col = pl.BlockSpec((hd, 1), lambda h: (h, 0))
    state_spec = pl.BlockSpec((None, hd, hd, b), lambda h: (h, 0, 0, 0))
    cols = [p[n].reshape(rw, 1) for n in ("rk", "lw", "lb")]
    y_t, s_new = pl.pallas_call(
        _rwkv_step_kernel,
        grid=(nh,),
        in_specs=[state_spec] + [chan] * 6 + [col] * 3,
        out_specs=[chan, state_spec],
        out_shape=[jax.ShapeDtypeStruct((rw, b), F32), jax.ShapeDtypeStruct((nh, hd, hd, b), F32)],
        scratch_shapes=[pltpu.VMEM((hd, b), F32)],
        compiler_params=_cparams("arbitrary"),
        name="rwkv_sample_step",
    )(s0_t, *feats, *cols)
    return y_t.T, s_new


def _outproj_kernel(x_ref, att_ref, g_ref, rw_ref, gr_ref, gate_ref, w_ref, fnw_ref, o_ref, *, final_norm):
    npair = att_ref.shape[0]
    acc = _nn((rw_ref[...] * _silu(gr_ref[...])).astype(BF16), w_ref[npair * LANES:, :])
    for p in range(npair):
        m = att_ref[p] * _silu(g_ref[p])
        acc = acc + _nn(m.astype(BF16), w_ref[p * LANES:(p + 1) * LANES, :])
    xo = x_ref[...] + gate_ref[...] * acc
    if final_norm:
        xo = xo * lax.rsqrt(jnp.mean(xo * xo, axis=-1, keepdims=True) + NORM_EPS) * fnw_ref[...]
    o_ref[...] = xo


def _outproj(x, att, g_att, rwk, g_rwkv, gate, w_out_bf16, final_norm_w, final_norm):
    b, t, d = x.shape
    npair = att.shape[1]
    rw = rwk.shape[-1]
    tm = min(ROW_TILE, t)
    per_row = gate.shape[1] != 1
    gate_spec = (pl.BlockSpec((None, tm, d), lambda i, j: (i, j, 0)) if per_row
                 else pl.BlockSpec((None, 1, d), lambda i, j: (i, 0, 0)))
    pair_spec = pl.BlockSpec((None, npair, tm, LANES), lambda i, j: (i, 0, j, 0))
    return pl.pallas_call(
        functools.partial(_outproj_kernel, final_norm=final_norm),
        grid=(b, t // tm),
        in_specs=[pl.BlockSpec((None, tm, d), lambda i, j: (i, j, 0)), pair_spec, pair_spec,
                  pl.BlockSpec((None, tm, rw), lambda i, j: (i, j, 0)),
                  pl.BlockSpec((None, tm, rw), lambda i, j: (i, j, 0)),
                  gate_spec,
                  pl.BlockSpec(w_out_bf16.shape, lambda i, j: (0, 0)),
                  pl.BlockSpec((1, d), lambda i, j: (0, 0))],
        out_specs=pl.BlockSpec((None, tm, d), lambda i, j: (i, j, 0)),
        out_shape=jax.ShapeDtypeStruct((b, t, d), F32),
        compiler_params=_cparams("arbitrary", "arbitrary"),
        name="out_proj",
    )(x, att, g_att, rwk, g_rwkv, gate, w_out_bf16, final_norm_w.reshape(1, d))


def _pairs_to_heads(a):
    b, npair, t, _ = a.shape
    return a.transpose(0, 2, 1, 3).reshape(b, t, npair * (LANES // HEAD_DIM), HEAD_DIM)


def kernel(x_prompt, x_sample, cache_win_k, cache_win_v, state_wkv, state_shift, c_prompt, c_sample, rel_bias, norm_w, ada_w, ada_b, w_in, mu_shift, w0, w_lora_b, a0, a_lora_b, k_k, k_a, r_k, ln_x_w, ln_x_b, w_out, final_norm_w):
    depth = norm_w.shape[0]
    bp, tp, d = x_prompt.shape
    bs, ts, _ = x_sample.shape
    assert ts == 1, "the sample group decodes one token per sequence"
    rw = w0.shape[-1]
    att_w = w_out.shape[1] - rw
    shift_w = mu_shift.shape[-1]
    nh_att = att_w // HEAD_DIM
    nh_rw = rw // HEAD_DIM
    assert tp % ATT_SUPER == 0 and tp % (RWKV_CHUNK * RWKV_CHUNKS_PER_STEP) == 0 and bs % 8 == 0
    assert cache_win_k.shape[2] == MAX_WINDOW
    keep = min(MAX_WINDOW, tp)

    bias_tiles = _prompt_bias_tiles(rel_bias)
    c_all = jnp.concatenate([c_prompt, c_sample], axis=0)
    npad = -c_all.shape[0] % 8
    c_all = jnp.pad(c_all, ((0, npad), (0, 0)))

    xp = x_prompt
    xs = x_sample.reshape(1, bs, d)
    outs = [[] for _ in range(8)]
    for l in range(depth):
        mod = _ada_mod(c_all, ada_w[l], ada_b[l])
        shift, scale, gate = jnp.split(mod, 3, axis=-1)
        w_in_b = w_in[l].astype(BF16)
        w_out_b = w_out[l].astype(BF16)
        p = dict(mu=mu_shift[l].reshape(1, -1), w0=w0[l].reshape(1, -1), wlb=w_lora_b[l].astype(BF16),
                 a0=a0[l].reshape(1, -1), alb=a_lora_b[l].astype(BF16), kk=k_k[l].reshape(1, -1),
                 ka=k_a[l].reshape(1, -1), rk=r_k[l].reshape(1, -1), lw=ln_x_w[l].reshape(1, -1),
                 lb=ln_x_b[l].reshape(1, -1))
        last = l == depth - 1

        pm = lambda a: a[:bp].reshape(bp, 1, d)
        q, k, v, g_att, u, g_rwkv = _inproj(xp, pm(shift), pm(scale), norm_w[l], w_in_b, att_w, shift_w)
        att = _att_prompt(q, k, v, bias_tiles)
        y_rw, s_p = _rwkv_prompt(u, jnp.zeros((bp, 1, shift_w), F32),
                                 jnp.zeros((bp, nh_rw, HEAD_DIM, HEAD_DIM), F32), p)
        xp = _outproj(xp, att, g_att, y_rw, g_rwkv, pm(gate), w_out_b, final_norm_w, last)
        outs[0].append(_pairs_to_heads(k[:, :, tp - keep:, :]))
        outs[1].append(_pairs_to_heads(v[:, :, tp - keep:, :]))
        outs[4].append(s_p)
        outs[6].append(u[:, -1])

        sm = lambda a: a[bp:bp + bs].reshape(1, bs, d)
        q, k, v, g_att, u, g_rwkv = _inproj(xs, sm(shift), sm(scale), norm_w[l], w_in_b, att_w, shift_w)
        nat = lambda a: a[0].transpose(1, 0, 2).reshape(bs, nh_att, HEAD_DIM)
        k_new, v_new = nat(k), nat(v)
        att_s = _att_sample(nat(q), k_new, v_new, cache_win_k[l].transpose(0, 2, 3, 1),
                            cache_win_v[l].transpose(0, 2, 3, 1), rel_bias)
        att_s = att_s.reshape(bs, att_w // LANES, LANES).transpose(1, 0, 2)[None]
        y_rw, s_s = _rwkv_sample(u[0], state_shift[l], state_wkv[l].transpose(1, 2, 3, 0), p)
        xs = _outproj(xs, att_s, g_att, y_rw[None], g_rwkv, sm(gate), w_out_b, final_norm_w, last)
        outs[2].append(k_new.reshape(bs, 1, nh_att, HEAD_DIM))
        outs[3].append(v_new.reshape(bs, 1, nh_att, HEAD_DIM))
        outs[5].append(s_s.transpose(3, 0, 1, 2))
        outs[7].append(u[0])

    stack = lambda i: jnp.stack(outs[i])
    return (xp, xs.reshape(bs, 1, d), stack(0), stack(1), stack(2), stack(3), stack(4), stack(5), stack(6),
            stack(7))
```

```python
import functools
import math

import numpy as np
import jax
import jax.numpy as jnp
from jax import lax
from jax.experimental import pallas as pl
from jax.experimental.pallas import tpu as pltpu

F32 = jnp.float32
BF16 = jnp.bfloat16
HIGHEST = lax.Precision.HIGHEST

HEAD_DIM = 64
DILATED_BRANCHES = ((128, 1), (512, 4), (2048, 16))
MAX_WINDOW = max(w for w, _ in DILATED_BRANCHES)
KEYS_PER_BRANCH = 128
N_BUCKETS = 32
BUCKET_MAX_DIST = MAX_WINDOW
LORA_W = 64
NORM_EPS = 1e-6
GN_EPS = HEAD_DIM * 1e-5

LANES = 128
VMEM_LIMIT_BYTES = 56 * 1024 * 1024

ROW_TILE = 256
ATT_BLOCK = 128
ATT_SUPER = MAX_WINDOW
ATT_TILES_PER_ITER = 4
RWKV_CHUNK = 64
RWKV_CHUNKS_PER_STEP = 4


def _cparams(*sem):
    return pltpu.CompilerParams(dimension_semantics=sem, vmem_limit_bytes=VMEM_LIMIT_BYTES)


def _silu(x):
    return x * jax.nn.sigmoid(x)


def _nt(a, b):
    return lax.dot_general(a, b, (((1,), (1,)), ((), ())), preferred_element_type=F32)


def _tn(a, b):
    return lax.dot_general(a, b, (((0,), (0,)), ((), ())), preferred_element_type=F32)


def _nn(a, b, precision=None):
    return jnp.dot(a, b, precision=precision, preferred_element_type=F32)


def _ada_kernel(c_ref, w_ref, b_ref, o_ref):
    s = _silu(c_ref[...])
    o_ref[...] = _nn(s, w_ref[...], HIGHEST) + b_ref[...]


def _ada_mod(c_all, ada_w, ada_b):
    n, d = c_all.shape
    e = ada_w.shape[1]
    tn = 512
    return pl.pallas_call(
        _ada_kernel,
        grid=(e // tn,),
        in_specs=[pl.BlockSpec((n, d), lambda j: (0, 0)),
                  pl.BlockSpec((d, tn), lambda j: (0, j)),
                  pl.BlockSpec((1, tn), lambda j: (0, j))],
        out_specs=pl.BlockSpec((n, tn), lambda j: (0, j)),
        out_shape=jax.ShapeDtypeStruct((n, e), F32),
        compiler_params=_cparams("arbitrary"),
        name="ada_mod",
    )(c_all, ada_w, ada_b.reshape(1, e))


def _inproj_kernel(x_ref, shift_ref, scale_ref, nw_ref, w_ref,
                   q_ref, k_ref, v_ref, g_ref, u_ref, gr_ref, *, att_w, shift_w):
    x = x_ref[...]
    xn = x * lax.rsqrt(jnp.mean(x * x, axis=-1, keepdims=True) + NORM_EPS) * nw_ref[...]
    h = xn * (1.0 + scale_ref[...]) + shift_ref[...]
    z = _nn(h.astype(BF16), w_ref[...])
    npair = att_w // LANES
    for p in range(npair):
        q_ref[p] = z[:, p * LANES:(p + 1) * LANES] * (HEAD_DIM ** -0.5)
        k_ref[p] = z[:, att_w + p * LANES:att_w + (p + 1) * LANES]
        v_ref[p] = z[:, 2 * att_w + p * LANES:2 * att_w + (p + 1) * LANES]
        g_ref[p] = z[:, 3 * att_w + p * LANES:3 * att_w + (p + 1) * LANES]
    u_ref[...] = z[:, 4 * att_w:4 * att_w + shift_w]
    gr_ref[...] = z[:, 4 * att_w + shift_w:]


def _inproj(x, shift, scale, norm_w, w_in_bf16, att_w, shift_w):
    b, t, d = x.shape
    in_w = w_in_bf16.shape[1]
    rw = in_w - 4 * att_w - shift_w
    npair = att_w // LANES
    tm = min(ROW_TILE, t)
    per_row = shift.shape[1] != 1
    mod_spec = (pl.BlockSpec((None, tm, d), lambda i, j: (i, j, 0)) if per_row
                else pl.BlockSpec((None, 1, d), lambda i, j: (i, 0, 0)))
    pair_spec = pl.BlockSpec((None, npair, tm, LANES), lambda i, j: (i, 0, j, 0))
    pair_shape = jax.ShapeDtypeStruct((b, npair, t, LANES), F32)
    return pl.pallas_call(
        functools.partial(_inproj_kernel, att_w=att_w, shift_w=shift_w),
        grid=(b, t // tm),
        in_specs=[pl.BlockSpec((None, tm, d), lambda i, j: (i, j, 0)), mod_spec, mod_spec,
                  pl.BlockSpec((1, d), lambda i, j: (0, 0)),
                  pl.BlockSpec((d, in_w), lambda i, j: (0, 0))],
        out_specs=[pair_spec, pair_spec, pair_spec, pair_spec,
                   pl.BlockSpec((None, tm, shift_w), lambda i, j: (i, j, 0)),
                   pl.BlockSpec((None, tm, rw), lambda i, j: (i, j, 0))],
        out_shape=[pair_shape, pair_shape, pair_shape, pair_shape,
                   jax.ShapeDtypeStruct((b, t, shift_w), F32),
                   jax.ShapeDtypeStruct((b, t, rw), F32)],
        compiler_params=_cparams("arbitrary", "arbitrary"),
        name="in_proj",
    )(x, shift, scale, norm_w.reshape(1, d), w_in_bf16)


def _t5_bucket_np(dist):
    max_exact = N_BUCKETS // 2
    nf = np.maximum(dist, max_exact).astype(np.float32)
    large = max_exact + (np.log(nf / np.float32(max_exact)) / np.float32(math.log(BUCKET_MAX_DIST / max_exact))
                         * np.float32(N_BUCKETS - max_exact)).astype(np.int32)
    large = np.minimum(large, N_BUCKETS - 1)
    return np.where(dist < max_exact, dist, large)


def _branch_bias(rel_bias):
    out = []
    for window, dil in DILATED_BRANCHES:
        dist = dil * np.arange(window // dil + 1, dtype=np.int32)
        out.append(rel_bias[_t5_bucket_np(dist)].T.astype(F32))
    return jnp.stack(out)


def _prompt_bias_tiles(rel_bias):
    bias = _branch_bias(rel_bias)
    nb, nh, _ = bias.shape
    blk, width = ATT_BLOCK, 2 * ATT_BLOCK
    pad = jnp.full((nb, nh, blk - 1), -jnp.inf, F32)
    strip = jnp.flip(jnp.concatenate([pad, bias, pad], axis=-1), axis=-1)
    length = 3 * blk - 1
    rows = jnp.broadcast_to(jnp.pad(strip, ((0, 0), (0, 0), (0, 1)))[:, :, None, :], (nb, nh, blk, length + 1))
    skew = rows.reshape(nb, nh, blk * (length + 1))[:, :, :blk * length].reshape(nb, nh, blk, length)
    tile = skew[:, :, :, blk - 1:blk - 1 + width]
    first = jnp.where((np.arange(width) >= blk)[None, None, None, :], tile, -jnp.inf)
    return jnp.stack([tile, first], axis=1)


def _att_prompt_kernel(q_ref, k_ref, v_ref, bias_ref, o_ref, m_ref, l_ref, acc_ref, *, super_rows):
    sb = pl.program_id(2)
    lane = lax.broadcasted_iota(jnp.int32, (1, LANES), 1)
    head0 = lane < HEAD_DIM
    group = ATT_TILES_PER_ITER
    ones_b = jnp.ones((2 * ATT_BLOCK, LANES), BF16)

    for bi, (_, dil) in enumerate(DILATED_BRANCHES):
        nblk = super_rows // (ATT_BLOCK * dil)
        rows = lambda s, dil=dil: (pl.ds(s, ATT_BLOCK, stride=dil) if dil > 1 else pl.ds(s, ATT_BLOCK))

        def tiles(it, carry, bi=bi, dil=dil, nblk=nblk, rows=rows):
            locs, firsts, q2, kcat, vcat = [], [], [], [], []
            for j in range(group):
                idx = it * group + j
                res = idx // nblk
                loc = res + dil * ATT_BLOCK * (idx - res * nblk)
                glob = sb * super_rows + loc
                prev = glob - dil * ATT_BLOCK
                first = prev < 0
                pstart = jnp.where(first, glob, prev)
                qt = q_ref[rows(loc), :]
                locs.append(loc)
                firsts.append(first.astype(jnp.int32))
                q2.append(jnp.concatenate([jnp.where(head0, qt, 0.0), jnp.where(head0, 0.0, qt)],
                                          axis=0).astype(BF16))
                kcat.append(jnp.concatenate([k_ref[rows(pstart), :], k_ref[rows(glob), :]], axis=0).astype(BF16))
                vcat.append(jnp.concatenate([v_ref[rows(pstart), :], v_ref[rows(glob), :]], axis=0).astype(BF16))
            s = [_nt(q2[j], kcat[j]) + bias_ref[bi, firsts[j]].reshape(2 * ATT_BLOCK, 2 * ATT_BLOCK)
                 for j in range(group)]
            mt = [jnp.max(x, axis=-1, keepdims=True) for x in s]
            p = [jnp.exp(s[j] - mt[j]).astype(BF16) for j in range(group)]
            pv = [_nn(p[j], jnp.concatenate([vcat[j], ones_b], axis=1)) for j in range(group)]
            for j in range(group):
                r = rows(locs[j])
                m_ref[bi, r, :] = jnp.where(head0, mt[j][:ATT_BLOCK], mt[j][ATT_BLOCK:])
                l_ref[bi, r, :] = jnp.where(head0, pv[j][:ATT_BLOCK, LANES:], pv[j][ATT_BLOCK:, LANES:])
                acc_ref[bi, r, :] = jnp.where(head0, pv[j][:ATT_BLOCK, :LANES], pv[j][ATT_BLOCK:, :LANES])
            return carry

        lax.fori_loop(0, dil * nblk // group, tiles, 0)

    nb = len(DILATED_BRANCHES)
    chunk = 2 * ATT_BLOCK

    def merge(c, carry):
        r = pl.ds(pl.multiple_of(c * chunk, chunk), chunk)
        ms = [m_ref[n, r, :] for n in range(nb)]
        m = functools.reduce(jnp.maximum, ms)
        ws = [jnp.exp(x - m) for x in ms]
        den = functools.reduce(jnp.add, [l_ref[n, r, :] * ws[n] for n in range(nb)])
        num = functools.reduce(jnp.add, [acc_ref[n, r, :] * ws[n] for n in range(nb)])
        o_ref[r, :] = num / den
        return carry

    lax.fori_loop(0, super_rows // chunk, merge, 0)


def _att_prompt(q, k, v, bias_tiles):
    b, npair, t, _ = q.shape
    sr = ATT_SUPER
    nb = len(DILATED_BRANCHES)
    return pl.pallas_call(
        functools.partial(_att_prompt_kernel, super_rows=sr),
        grid=(b, npair, t // sr),
        in_specs=[pl.BlockSpec((None, None, sr, LANES), lambda i, p, s: (i, p, s, 0)),
                  pl.BlockSpec((None, None, t, LANES), lambda i, p, s: (i, p, 0, 0)),
                  pl.BlockSpec((None, None, t, LANES), lambda i, p, s: (i, p, 0, 0)),
                  pl.BlockSpec((nb, 2, 2, ATT_BLOCK, 2 * ATT_BLOCK), lambda i, p, s: (0, 0, p, 0, 0))],
        out_specs=pl.BlockSpec((None, None, sr, LANES), lambda i, p, s: (i, p, s, 0)),
        out_shape=jax.ShapeDtypeStruct((b, npair, t, LANES), F32),
        scratch_shapes=[pltpu.VMEM((nb, sr, LANES), F32)] * 3,
        compiler_params=_cparams("arbitrary", "arbitrary", "arbitrary"),
        name="att_prompt",
    )(q, k, v, bias_tiles)


def _att_sample_kernel(q_ref, kn_ref, vn_ref, kt_ref, vt_ref, bias_ref, b0_ref, o_ref):
    nh, hd, l = kt_ref.shape
    nb = bias_ref.shape[0]
    q = q_ref[...]
    qb = q.astype(BF16)
    head = lax.broadcasted_iota(jnp.int32, (nh, 1), 0)
    logit = jnp.zeros((nh, l), F32)
    for h in range(nh):
        s = _nn(qb, kt_ref[h].astype(BF16))
        logit = jnp.where(head == h, s, logit)
    s0 = jnp.sum(q * kn_ref[...], axis=-1, keepdims=True)
    ls = [logit + bias_ref[n] for n in range(nb)]
    s_self = [s0 + b0_ref[n] for n in range(nb)]
    m = functools.reduce(jnp.maximum, [jnp.max(x, axis=-1, keepdims=True) for x in ls] + s_self)
    p = functools.reduce(jnp.add, [jnp.exp(x - m) for x in ls])
    p0 = functools.reduce(jnp.add, [jnp.exp(x - m) for x in s_self])
    den = jnp.sum(p, axis=-1, keepdims=True) + p0
    pb = p.astype(BF16)
    out = p0 * vn_ref[...]
    for h in range(nh):
        out = out + jnp.where(head == h, _nt(pb, vt_ref[h].astype(BF16)), 0.0)
    o_ref[...] = out / den


def _att_sample(q, k_new, v_new, cache_kt, cache_vt, rel_bias):
    b, nh, hd = q.shape
    l = cache_kt.shape[-1]
    bias = _branch_bias(rel_bias)
    back = l - np.arange(l)
    tabs = []
    for n, (window, dil) in enumerate(DILATED_BRANCHES):
        used = (back % dil == 0) & (back <= window)
        tabs.append(jnp.where(used[None, :], bias[n][:, np.where(used, back // dil, 0)], -jnp.inf))
    bias_pos = jnp.stack(tabs)
    bias0 = bias[:, :, 0][..., None]
    row = pl.BlockSpec((None, nh, hd), lambda i: (i, 0, 0))
    cache = pl.BlockSpec((None, nh, hd, l), lambda i: (i, 0, 0, 0))
    full = lambda a: pl.BlockSpec(a.shape, lambda i: (0,) * a.ndim)
    return pl.pallas_call(
        _att_sample_kernel,
        grid=(b,),
        in_specs=[row, row, row, cache, cache, full(bias_pos), full(bias0)],
        out_specs=row,
        out_shape=jax.ShapeDtypeStruct((b, nh, hd), F32),
        compiler_params=_cparams("arbitrary"),
        name="att_sample",
    )(q, k_new, v_new, cache_kt, cache_vt, bias_pos, bias0)


def _rwkv_features(u, u_prev, mu, w0, wlb, a0, alb, kk_scale, ka, rw):
    um = u + (u_prev - u) * mu
    r = um[:, :rw]
    k = um[:, rw:2 * rw]
    v = um[:, 2 * rw:3 * rw]
    xw = um[:, 3 * rw:3 * rw + LORA_W]
    xa = um[:, 3 * rw + LORA_W:3 * rw + 2 * LORA_W]
    wl = w0 + _nn(jnp.tanh(xw).astype(BF16), wlb)
    w_log = -jax.nn.softplus(-wl) - 0.5
    logw = -jnp.exp(w_log)
    a = jax.nn.sigmoid(a0 + _nn(xa.astype(BF16), alb))
    kk_raw = k * kk_scale
    k2 = k * (1.0 + (a - 1.0) * ka)
    return r, k2, v, logw, kk_raw, a


def _rwkv_prompt_kernel(u_ref, prev0_ref, s0_ref, mu_ref, w0_ref, wlb_ref, a0_ref, alb_ref, kk_ref, ka_ref,
                        rk_ref, lw_ref, lb_ref, y_ref, sout_ref, s_scr, carry_scr, *, rw):
    c = pl.program_id(1)
    nc = pl.num_programs(1)
    rows = u_ref.shape[0]
    ch = RWKV_CHUNK
    ng = rows // ch
    nh = rw // HEAD_DIM
    hd = HEAD_DIM
    ntile = rw // LANES

    @pl.when(c == 0)
    def _():
        s_scr[...] = jnp.zeros(s_scr.shape, F32)
        for h in range(nh):
            s_scr[h, :, (h % 2) * hd:(h % 2 + 1) * hd] = s0_ref[h]
        carry_scr[...] = prev0_ref[...]

    u = u_ref[...]
    shifted = pltpu.roll(u, 1, 0)
    row8 = lax.broadcasted_iota(jnp.int32, (8, 1), 0)
    u_prev = jnp.concatenate([jnp.where(row8 == 0, carry_scr[...], shifted[:8]), shifted[8:]], axis=0)
    carry_scr[...] = u[rows - 1:rows, :]
    r, k2, v, logw, kk_raw, a = _rwkv_features(u, u_prev, mu_ref[...], w0_ref[...], wlb_ref[...], a0_ref[...],
                                               alb_ref[...], kk_ref[...], ka_ref[...], rw)
    lane = lax.broadcasted_iota(jnp.int32, (1, LANES), 1)
    left = lane < hd

    def head_sum(x):
        tiles = []
        for q in range(ntile):
            t = x[:, q * LANES:(q + 1) * LANES]
            s_left = jnp.sum(jnp.where(left, t, 0.0), axis=-1, keepdims=True)
            s_right = jnp.sum(jnp.where(left, 0.0, t), axis=-1, keepdims=True)
            tiles.append(jnp.where(left, s_left, s_right))
        return jnp.concatenate(tiles, axis=1)

    kk = kk_raw * lax.rsqrt(jnp.maximum(head_sum(kk_raw * kk_raw), 1e-24))
    beta = kk * a
    v_x = pltpu.roll(v, hd, 1)

    ti = lax.broadcasted_iota(jnp.int32, (ch, ch), 0)
    si = lax.broadcasted_iota(jnp.int32, (ch, ch), 1)
    tri_b = (ti >= si).astype(BF16)
    row_c = lax.broadcasted_iota(jnp.int32, (ch, 1), 0)
    t2 = lax.broadcasted_iota(jnp.int32, (2 * ch, 2 * ch), 0)
    s2 = lax.broadcasted_iota(jnp.int32, (2 * ch, 2 * ch), 1)
    tt = jnp.where(t2 >= ch, t2 - ch, t2)
    ss = jnp.where(s2 >= ch, s2 - ch, s2)
    score_mask = (tt - ss) >= jnp.where(t2 >= ch, 0, 1)
    tcol = lax.broadcasted_iota(jnp.int32, (ch, LANES), 1)
    trow = lax.broadcasted_iota(jnp.int32, (ch, LANES), 0)
    eye_right = (tcol == trow + ch).astype(F32)
    zeros_b = jnp.zeros((ch, LANES), BF16)
    own_of = [left if h % 2 == 0 else jnp.logical_not(left) for h in range(nh)]
    tile_of = [slice((h // 2) * LANES, (h // 2 + 1) * LANES) for h in range(nh)]
    xtile_of = [slice((((h + 1) // 2) % ntile) * LANES, (((h + 1) // 2) % ntile + 1) * LANES) for h in range(nh)]

    nu = ng * nh
    lhs, rhs, bk, vx, at_b, rt_own, pc = [], [], [], [], [], [], []
    for g in range(ng):
        rs = slice(g * ch, (g + 1) * ch)
        l1 = logw[rs].astype(BF16)
        rem = logw[rs] - l1.astype(F32)
        l2 = rem.astype(BF16)
        l3 = (rem - l2.astype(F32)).astype(BF16)
        cum3 = _nn(tri_b, jnp.concatenate([l1, l2, l3], axis=1))
        cum = cum3[:, :rw] + (cum3[:, rw:2 * rw] + cum3[:, 2 * rw:])
        ctot = cum[ch - 1:ch, :]
        pc_g = jnp.exp(ctot)
        e_cur = jnp.exp(cum)
        e_neg = jnp.exp(-cum)
        e_prev = jnp.where(row_c == 0, 1.0, pltpu.roll(e_cur, 1, 0))
        e_end = pc_g * e_neg
        rt_f = r[rs] * e_cur
        at_g = (-kk[rs] * e_prev).astype(BF16)
        rt_g = rt_f.astype(BF16)
        bt_g = (beta[rs] * e_neg).astype(BF16)
        kt_g = (k2[rs] * e_neg).astype(BF16)
        bh_g = (beta[rs] * e_end).astype(BF16)
        kh_g = (k2[rs] * e_end).astype(BF16)
        vx_g = v_x[rs].astype(BF16)
        for h in range(nh):
            own, tl = own_of[h], tile_of[h]
            zb = jnp.zeros((), BF16)
            lhs.append(jnp.concatenate([jnp.where(own, at_g[:, tl], zb), jnp.where(own, rt_g[:, tl], zb)], axis=0))
            rhs.append(jnp.concatenate([bt_g[:, tl], kt_g[:, tl]], axis=0))
            bk.append(jnp.concatenate([jnp.where(own, bh_g[:, tl], zb), jnp.where(own, kh_g[:, tl], zb)], axis=0))
            vx.append(jnp.where(own, zb, vx_g[:, xtile_of[h]]))
            at_b.append(at_g[:, tl])
            rt_own.append(jnp.where(own, rt_f[:, tl], 0.0))
            pc.append(pc_g[:, tl])

    units = range(nu)
    own_u = [own_of[i % nh] for i in units]
    sc = [jnp.where(score_mask, _nt(lhs[i], rhs[i]), 0.0) for i in units]
    top_b = [sc[i][:ch].astype(BF16) for i in units]
    bot_b = [sc[i][ch:].astype(BF16) for i in units]
    aakv = [_nn(top_b[i], jnp.concatenate([zeros_b, vx[i]], axis=0)) for i in units]
    x = [jnp.where(left, sc[i][:ch], eye_right) for i in units]
    npow = 1
    while npow < ch:
        xb = [x[i].astype(BF16) for i in units]
        x = [_nn(xb[i], jnp.concatenate([xb[i], zeros_b], axis=0)) + jnp.where(left, 0.0, x[i]) for i in units]
        npow *= 2
    z_b = [jnp.where(own_u[i], at_b[i], aakv[i].astype(BF16)) for i in units]
    wu = [_nn(x[i].astype(BF16), jnp.concatenate([zeros_b, z_b[i]], axis=0)) for i in units]
    rhs2 = [jnp.concatenate([wu[i].astype(BF16), vx[i]], axis=0) for i in units]
    qy = [_nn(bot_b[i], rhs2[i]) for i in units]
    pg = [_tn(rhs2[i], bk[i]) for i in units]

    bonus = head_sum(r * k2 * rk_ref[...]) * v
    s_cur = [s_scr[h] for h in range(nh)]
    y_sw = []
    for g in range(ng):
        y_u = []
        for h in range(nh):
            i = g * nh + h
            own = own_of[h]
            s_b = s_cur[h].astype(BF16)
            s_pad = jnp.concatenate([zeros_b, s_b] if h % 2 == 0 else [s_b, zeros_b], axis=0)
            qhat_b = jnp.where(own, rt_own[i] + qy[i], 0.0).astype(BF16)
            y_u.append(jnp.where(own, 0.0, qy[i] + _nt(qhat_b, s_pad)))
            g0 = pg[i][ch:] if h % 2 == 0 else pg[i][:ch]
            s_cur[h] = s_cur[h] * pc[i] + _nn(s_b, pg[i].astype(BF16)) + g0
        y_sw.append(jnp.concatenate([y_u[(2 * q - 1) % nh] + y_u[2 * q] for q in range(ntile)], axis=1))
    for h in range(nh):
        s_scr[h] = s_cur[h]
    y = pltpu.roll(jnp.concatenate(y_sw, axis=0), rw - hd, 1)
    mean = head_sum(y) * (1.0 / hd)
    dev = y - mean
    var = head_sum(dev * dev) * (1.0 / hd)
    y_ref[...] = dev * lax.rsqrt(var + GN_EPS) * lw_ref[...] + lb_ref[...] + bonus

    @pl.when(c == nc - 1)
    def _():
        for h in range(nh):
            sout_ref[h] = s_scr[h, :, (h % 2) * hd:(h % 2 + 1) * hd]


def _rwkv_prompt(u, prev0, s0, p):
    b, t, sw = u.shape
    rw = p["w0"].shape[-1]
    nh = rw // HEAD_DIM
    hd = HEAD_DIM
    rows = RWKV_CHUNK * RWKV_CHUNKS_PER_STEP
    vec = lambda a: pl.BlockSpec(a.shape, lambda i, c: (0,) * a.ndim)
    params = [p["mu"], p["w0"], p["wlb"], p["a0"], p["alb"], p["kk"], p["ka"], p["rk"], p["lw"], p["lb"]]
    state_spec = pl.BlockSpec((None, nh, hd, hd), lambda i, c: (i, 0, 0, 0))
    return pl.pallas_call(
        functools.partial(_rwkv_prompt_kernel, rw=rw),
        grid=(b, t // rows),
        in_specs=[pl.BlockSpec((None, rows, sw), lambda i, c: (i, c, 0)),
                  pl.BlockSpec((None, 1, sw), lambda i, c: (i, 0, 0)),
                  state_spec] + [vec(a) for a in params],
        out_specs=[pl.BlockSpec((None, rows, rw), lambda i, c: (i, c, 0)), state_spec],
        out_shape=[jax.ShapeDtypeStruct((b, t, rw), F32), jax.ShapeDtypeStruct((b, nh, hd, hd), F32)],
        scratch_shapes=[pltpu.VMEM((nh, hd, LANES), F32), pltpu.VMEM((1, sw), F32)],
        compiler_params=_cparams("arbitrary", "arbitrary"),
        name="rwkv_prompt",
    )(u, prev0, s0, *params)


def _rwkv_feat_kernel(u_ref, prev_ref, mu_ref, w0_ref, wlb_ref, a0_ref, alb_ref, kk_ref, ka_ref,
                      r_ref, k_ref, v_ref, w_ref, kkr_ref, a_ref, *, rw):
    r, k2, v, logw, kk_raw, a = _rwkv_features(u_ref[...], prev_ref[...], mu_ref[...], w0_ref[...], wlb_ref[...],
                                               a0_ref[...], alb_ref[...], kk_ref[...], ka_ref[...], rw)
    r_ref[...] = r.T
    k_ref[...] = k2.T
    v_ref[...] = v.T
    w_ref[...] = jnp.exp(logw).T
    kkr_ref[...] = kk_raw.T
    a_ref[...] = a.T


def _rwkv_step_kernel(s_ref, r_ref, k_ref, v_ref, w_ref, kkr_ref, a_ref, rk_ref, lw_ref, lb_ref,
                      y_ref, sout_ref, y_scr):
    hd = s_ref.shape[0]
    kk_raw = kkr_ref[...]
    kk = kk_raw / jnp.maximum(jnp.sqrt(jnp.sum(kk_raw * kk_raw, axis=0, keepdims=True)), 1e-12)
    beta = kk * a_ref[...]
    w, k2, r, v = w_ref[...], k_ref[...], r_ref[...], v_ref[...]

    def value_row(i, carry):
        s = s_ref[i]
        sa = jnp.sum(s * kk, axis=0, keepdims=True)
        s_new = s * w - sa * beta + v_ref[pl.ds(i, 1), :] * k2
        sout_ref[i] = s_new
        y_scr[pl.ds(i, 1), :] = jnp.sum(s_new * r, axis=0, keepdims=True)
        return carry

    lax.fori_loop(0, hd, value_row, 0)
    y = y_scr[...]
    mean = jnp.mean(y, axis=0, keepdims=True)
    var = jnp.mean(jnp.square(y - mean), axis=0, keepdims=True)
    yn = (y - mean) * lax.rsqrt(var + GN_EPS) * lw_ref[...] + lb_ref[...]
    y_ref[...] = yn + jnp.sum(r * k2 * rk_ref[...], axis=0, keepdims=True) * v


def _rwkv_sample(u, prev, s0_t, p):
    b, sw = u.shape
    rw = p["w0"].shape[-1]
    nh = rw // HEAD_DIM
    hd = HEAD_DIM
    full = lambda a: pl.BlockSpec(a.shape, lambda i: (0,) * a.ndim)
    fparams = [p["mu"], p["w0"], p["wlb"], p["a0"], p["alb"], p["kk"], p["ka"]]
    feats = pl.pallas_call(
        functools.partial(_rwkv_feat_kernel, rw=rw),
        grid=(1,),
        in_specs=[full(u), full(prev)] + [full(a) for a in fparams],
        out_specs=[pl.BlockSpec((rw, b), lambda i: (0, 0))] * 6,
        out_shape=[jax.ShapeDtypeStruct((rw, b), F32)] * 6,
        compiler_params=_cparams("arbitrary"),
        name="rwkv_sample_features",
    )(u, prev, *fparams)
    chan = pl.BlockSpec((hd, b), lambda h: (h, 0))
    col = pl.BlockSpec((hd, 1), lambda h: (h, 0))
    state_spec = pl.BlockSpec((None, hd, hd, b), lambda h: (h, 0, 0, 0))
    cols = [p[n].reshape(rw, 1) for n in ("rk", "lw", "lb")]
    y_t, s_new = pl.pallas_call(
        _rwkv_step_kernel,
        grid=(nh,),
        in_specs=[state_spec] + [chan] * 6 + [col] * 3,
        out_specs=[chan, state_spec],
        out_shape=[jax.ShapeDtypeStruct((rw, b), F32), jax.ShapeDtypeStruct((nh, hd, hd, b), F32)],
        scratch_shapes=[pltpu.VMEM((hd, b), F32)],
        compiler_params=_cparams("arbitrary"),
        name="rwkv_sample_step",
    )(s0_t, *feats, *cols)
    return y_t.T, s_new


def _outproj_kernel(x_ref, att_ref, g_ref, rw_ref, gr_ref, gate_ref, w_ref, fnw_ref, o_ref, *, final_norm):
    npair = att_ref.shape[0]
    mixed = [(att_ref[p] * _silu(g_ref[p])).astype(BF16) for p in range(npair)]
    mixed.append((rw_ref[...] * _silu(gr_ref[...])).astype(BF16))
    acc = _nn(jnp.concatenate(mixed, axis=1), w_ref[...])
    xo = x_ref[...] + gate_ref[...] * acc
    if final_norm:
        xo = xo * lax.rsqrt(jnp.mean(xo * xo, axis=-1, keepdims=True) + NORM_EPS) * fnw_ref[...]
    o_ref[...] = xo


def _outproj(x, att, g_att, rwk, g_rwkv, gate, w_out_bf16, final_norm_w, final_norm):
    b, t, d = x.shape
    npair = att.shape[1]
    rw = rwk.shape[-1]
    tm = min(ROW_TILE, t)
    per_row = gate.shape[1] != 1
    gate_spec = (pl.BlockSpec((None, tm, d), lambda i, j: (i, j, 0)) if per_row
                 else pl.BlockSpec((None, 1, d), lambda i, j: (i, 0, 0)))
    pair_spec = pl.BlockSpec((None, npair, tm, LANES), lambda i, j: (i, 0, j, 0))
    return pl.pallas_call(
        functools.partial(_outproj_kernel, final_norm=final_norm),
        grid=(b, t // tm),
        in_specs=[pl.BlockSpec((None, tm, d), lambda i, j: (i, j, 0)), pair_spec, pair_spec,
                  pl.BlockSpec((None, tm, rw), lambda i, j: (i, j, 0)),
                  pl.BlockSpec((None, tm, rw), lambda i, j: (i, j, 0)),
                  gate_spec,
                  pl.BlockSpec(w_out_bf16.shape, lambda i, j: (0, 0)),
                  pl.BlockSpec((1, d), lambda i, j: (0, 0))],
        out_specs=pl.BlockSpec((None, tm, d), lambda i, j: (i, j, 0)),
        out_shape=jax.ShapeDtypeStruct((b, t, d), F32),
        compiler_params=_cparams("arbitrary", "arbitrary"),
        name="out_proj",
    )(x, att, g_att, rwk, g_rwkv, gate, w_out_bf16, final_norm_w.reshape(1, d))


def _pairs_to_heads(a):
    b, npair, t, _ = a.shape
    return a.transpose(0, 2, 1, 3).reshape(b, t, npair * (LANES // HEAD_DIM), HEAD_DIM)


def kernel(x_prompt, x_sample, cache_win_k, cache_win_v, state_wkv, state_shift, c_prompt, c_sample, rel_bias, norm_w, ada_w, ada_b, w_in, mu_shift, w0, w_lora_b, a0, a_lora_b, k_k, k_a, r_k, ln_x_w, ln_x_b, w_out, final_norm_w):
    depth = norm_w.shape[0]
    bp, tp, d = x_prompt.shape
    bs, ts, _ = x_sample.shape
    assert ts == 1, "the sample group decodes one token per sequence"
    rw = w0.shape[-1]
    att_w = w_out.shape[1] - rw
    shift_w = mu_shift.shape[-1]
    nh_att = att_w // HEAD_DIM
    nh_rw = rw // HEAD_DIM
    assert tp % ATT_SUPER == 0 and tp % (RWKV_CHUNK * RWKV_CHUNKS_PER_STEP) == 0 and bs % 8 == 0
    assert cache_win_k.shape[2] == MAX_WINDOW
    keep = min(MAX_WINDOW, tp)

    bias_tiles = _prompt_bias_tiles(rel_bias)
    c_all = jnp.concatenate([c_prompt, c_sample], axis=0)
    npad = -c_all.shape[0] % 8
    c_all = jnp.pad(c_all, ((0, npad), (0, 0)))

    xp = x_prompt
    xs = x_sample.reshape(1, bs, d)
    outs = [[] for _ in range(8)]
    for l in range(depth):
        mod = _ada_mod(c_all, ada_w[l], ada_b[l])
        shift, scale, gate = jnp.split(mod, 3, axis=-1)
        w_in_b = w_in[l].astype(BF16)
        w_out_b = w_out[l].astype(BF16)
        p = dict(mu=mu_shift[l].reshape(1, -1), w0=w0[l].reshape(1, -1), wlb=w_lora_b[l].astype(BF16),
                 a0=a0[l].reshape(1, -1), alb=a_lora_b[l].astype(BF16), kk=k_k[l].reshape(1, -1),
                 ka=k_a[l].reshape(1, -1), rk=r_k[l].reshape(1, -1), lw=ln_x_w[l].reshape(1, -1),
                 lb=ln_x_b[l].reshape(1, -1))
        last = l == depth - 1

        pm = lambda a: a[:bp].reshape(bp, 1, d)
        q, k, v, g_att, u, g_rwkv = _inproj(xp, pm(shift), pm(scale), norm_w[l], w_in_b, att_w, shift_w)
        att = _att_prompt(q, k, v, bias_tiles)
        y_rw, s_p = _rwkv_prompt(u, jnp.zeros((bp, 1, shift_w), F32),
                                 jnp.zeros((bp, nh_rw, HEAD_DIM, HEAD_DIM), F32), p)
        xp = _outproj(xp, att, g_att, y_rw, g_rwkv, pm(gate), w_out_b, final_norm_w, last)
        outs[0].append(_pairs_to_heads(k[:, :, tp - keep:, :]))
        outs[1].append(_pairs_to_heads(v[:, :, tp - keep:, :]))
        outs[4].append(s_p)
        outs[6].append(u[:, -1])

        sm = lambda a: a[bp:bp + bs].reshape(1, bs, d)
        q, k, v, g_att, u, g_rwkv = _inproj(xs, sm(shift), sm(scale), norm_w[l], w_in_b, att_w, shift_w)
        nat = lambda a: a[0].transpose(1, 0, 2).reshape(bs, nh_att, HEAD_DIM)
        k_new, v_new = nat(k), nat(v)
        att_s = _att_sample(nat(q), k_new, v_new, cache_win_k[l].transpose(0, 2, 3, 1),
                            cache_win_v[l].transpose(0, 2, 3, 1), rel_bias)
        att_s = att_s.reshape(bs, att_w // LANES, LANES).transpose(1, 0, 2)[None]
        y_rw, s_s = _rwkv_sample(u[0], state_shift[l], state_wkv[l].transpose(1, 2, 3, 0), p)
        xs = _outproj(xs, att_s, g_att, y_rw[None], g_rwkv, sm(gate), w_out_b, final_norm_w, last)
        outs[2].append(k_new.reshape(bs, 1, nh_att, HEAD_DIM))
        outs[3].append(v_new.reshape(bs, 1, nh_att, HEAD_DIM))
        outs[5].append(s_s.transpose(3, 0, 1, 2))
        outs[7].append(u[0])

    stack = lambda i: jnp.stack(outs[i])
    return (xp, xs.reshape(bs, 1, d), stack(0), stack(1), stack(2), stack(3), stack(4), stack(5), stack(6),
            stack(7))
```

```python
import functools
import math

import numpy as np
import jax
import jax.numpy as jnp
from jax import lax
from jax.experimental import pallas as pl
from jax.experimental.pallas import tpu as pltpu

F32 = jnp.float32
BF16 = jnp.bfloat16
HIGHEST = lax.Precision.HIGHEST

HEAD_DIM = 64
DILATED_BRANCHES = ((128, 1), (512, 4), (2048, 16))
MAX_WINDOW = max(w for w, _ in DILATED_BRANCHES)
KEYS_PER_BRANCH = 128
N_BUCKETS = 32
BUCKET_MAX_DIST = MAX_WINDOW
LORA_W = 64
NORM_EPS = 1e-6
GN_EPS = HEAD_DIM * 1e-5

LANES = 128
VMEM_LIMIT_BYTES = 56 * 1024 * 1024

ROW_TILE = 256
ATT_BLOCK = 128
ATT_SUPER = MAX_WINDOW
ATT_TILES_PER_ITER = 4
RWKV_CHUNK = 64
RWKV_CHUNKS_PER_STEP = 4


def _cparams(*sem):
    return pltpu.CompilerParams(dimension_semantics=sem, vmem_limit_bytes=VMEM_LIMIT_BYTES)


def _silu(x):
    return x * jax.nn.sigmoid(x)


def _nt(a, b):
    return lax.dot_general(a, b, (((1,), (1,)), ((), ())), preferred_element_type=F32)


def _tn(a, b):
    return lax.dot_general(a, b, (((0,), (0,)), ((), ())), preferred_element_type=F32)


def _nn(a, b, precision=None):
    return jnp.dot(a, b, precision=precision, preferred_element_type=F32)


def _ada_kernel(c_ref, w_ref, b_ref, o_ref):
    s = _silu(c_ref[...])
    o_ref[...] = _nn(s, w_ref[...], HIGHEST) + b_ref[...]


def _ada_mod(c_all, ada_w, ada_b):
    n, d = c_all.shape
    e = ada_w.shape[1]
    tn = 512
    return pl.pallas_call(
        _ada_kernel,
        grid=(e // tn,),
        in_specs=[pl.BlockSpec((n, d), lambda j: (0, 0)),
                  pl.BlockSpec((d, tn), lambda j: (0, j)),
                  pl.BlockSpec((1, tn), lambda j: (0, j))],
        out_specs=pl.BlockSpec((n, tn), lambda j: (0, j)),
        out_shape=jax.ShapeDtypeStruct((n, e), F32),
        compiler_params=_cparams("arbitrary"),
        name="ada_mod",
    )(c_all, ada_w, ada_b.reshape(1, e))


def _inproj_kernel(x_ref, shift_ref, scale_ref, nw_ref, w_ref,
                   q_ref, k_ref, v_ref, g_ref, u_ref, gr_ref, *, att_w, shift_w):
    x = x_ref[...]
    xn = x * lax.rsqrt(jnp.mean(x * x, axis=-1, keepdims=True) + NORM_EPS) * nw_ref[...]
    h = xn * (1.0 + scale_ref[...]) + shift_ref[...]
    z = _nn(h.astype(BF16), w_ref[...])
    npair = att_w // LANES
    for p in range(npair):
        q_ref[p] = z[:, p * LANES:(p + 1) * LANES] * (HEAD_DIM ** -0.5)
        k_ref[p] = z[:, att_w + p * LANES:att_w + (p + 1) * LANES]
        v_ref[p] = z[:, 2 * att_w + p * LANES:2 * att_w + (p + 1) * LANES]
        g_ref[p] = z[:, 3 * att_w + p * LANES:3 * att_w + (p + 1) * LANES]
    u_ref[...] = z[:, 4 * att_w:4 * att_w + shift_w]
    gr_ref[...] = z[:, 4 * att_w + shift_w:]


def _inproj(x, shift, scale, norm_w, w_in_bf16, att_w, shift_w):
    b, t, d = x.shape
    in_w = w_in_bf16.shape[1]
    rw = in_w - 4 * att_w - shift_w
    npair = att_w // LANES
    tm = min(ROW_TILE, t)
    per_row = shift.shape[1] != 1
    mod_spec = (pl.BlockSpec((None, tm, d), lambda i, j: (i, j, 0)) if per_row
                else pl.BlockSpec((None, 1, d), lambda i, j: (i, 0, 0)))
    pair_spec = pl.BlockSpec((None, npair, tm, LANES), lambda i, j: (i, 0, j, 0))
    pair_shape = jax.ShapeDtypeStruct((b, npair, t, LANES), F32)
    return pl.pallas_call(
        functools.partial(_inproj_kernel, att_w=att_w, shift_w=shift_w),
        grid=(b, t // tm),
        in_specs=[pl.BlockSpec((None, tm, d), lambda i, j: (i, j, 0)), mod_spec, mod_spec,
                  pl.BlockSpec((1, d), lambda i, j: (0, 0)),
                  pl.BlockSpec((d, in_w), lambda i, j: (0, 0))],
        out_specs=[pair_spec, pair_spec, pair_spec, pair_spec,
                   pl.BlockSpec((None, tm, shift_w), lambda i, j: (i, j, 0)),
                   pl.BlockSpec((None, tm, rw), lambda i, j: (i, j, 0))],
        out_shape=[pair_shape, pair_shape, pair_shape, pair_shape,
                   jax.ShapeDtypeStruct((b, t, shift_w), F32),
                   jax.ShapeDtypeStruct((b, t, rw), F32)],
        compiler_params=_cparams("arbitrary", "arbitrary"),
        name="in_proj",
    )(x, shift, scale, norm_w.reshape(1, d), w_in_bf16)


def _t5_bucket_np(dist):
    max_exact = N_BUCKETS // 2
    nf = np.maximum(dist, max_exact).astype(np.float32)
    large = max_exact + (np.log(nf / np.float32(max_exact)) / np.float32(math.log(BUCKET_MAX_DIST / max_exact))
                         * np.float32(N_BUCKETS - max_exact)).astype(np.int32)
    large = np.minimum(large, N_BUCKETS - 1)
    return np.where(dist < max_exact, dist, large)


def _branch_bias(rel_bias):
    out = []
    for window, dil in DILATED_BRANCHES:
        dist = dil * np.arange(window // dil + 1, dtype=np.int32)
        out.append(rel_bias[_t5_bucket_np(dist)].T.astype(F32))
    return jnp.stack(out)


def _prompt_bias_tiles(rel_bias):
    bias = _branch_bias(rel_bias)
    nb, nh, _ = bias.shape
    blk, width = ATT_BLOCK, 2 * ATT_BLOCK
    pad = jnp.full((nb, nh, blk - 1), -jnp.inf, F32)
    strip = jnp.flip(jnp.concatenate([pad, bias, pad], axis=-1), axis=-1)
    length = 3 * blk - 1
    rows = jnp.broadcast_to(jnp.pad(strip, ((0, 0), (0, 0), (0, 1)))[:, :, None, :], (nb, nh, blk, length + 1))
    skew = rows.reshape(nb, nh, blk * (length + 1))[:, :, :blk * length].reshape(nb, nh, blk, length)
    tile = skew[:, :, :, blk - 1:blk - 1 + width]
    first = jnp.where((np.arange(width) >= blk)[None, None, None, :], tile, -jnp.inf)
    return jnp.stack([tile, first], axis=1)


def _att_prompt_kernel(q_ref, k_ref, v_ref, bias_ref, o_ref, m_ref, l_ref, acc_ref, *, super_rows):
    sb = pl.program_id(2)
    lane = lax.broadcasted_iota(jnp.int32, (1, LANES), 1)
    head0 = lane < HEAD_DIM
    group = ATT_TILES_PER_ITER
    ones_b = jnp.ones((2 * ATT_BLOCK, LANES), BF16)

    for bi, (_, dil) in enumerate(DILATED_BRANCHES):
        nblk = super_rows // (ATT_BLOCK * dil)
        rows = lambda s, dil=dil: (pl.ds(s, ATT_BLOCK, stride=dil) if dil > 1 else pl.ds(s, ATT_BLOCK))

        def tiles(it, carry, bi=bi, dil=dil, nblk=nblk, rows=rows):
            locs, firsts, q2, kcat, vcat = [], [], [], [], []
            for j in range(group):
                idx = it * group + j
                res = idx // nblk
                loc = res + dil * ATT_BLOCK * (idx - res * nblk)
                glob = sb * super_rows + loc
                prev = glob - dil * ATT_BLOCK
                first = prev < 0
                pstart = jnp.where(first, glob, prev)
                qt = q_ref[rows(loc), :]
                locs.append(loc)
                firsts.append(first.astype(jnp.int32))
                q2.append(jnp.concatenate([jnp.where(head0, qt, 0.0), jnp.where(head0, 0.0, qt)],
                                          axis=0).astype(BF16))
                kcat.append(jnp.concatenate([k_ref[rows(pstart), :], k_ref[rows(glob), :]], axis=0).astype(BF16))
                vcat.append(jnp.concatenate([v_ref[rows(pstart), :], v_ref[rows(glob), :]], axis=0).astype(BF16))
            s = [_nt(q2[j], kcat[j]) + bias_ref[bi, firsts[j]].reshape(2 * ATT_BLOCK, 2 * ATT_BLOCK)
                 for j in range(group)]
            mt = [jnp.max(x, axis=-1, keepdims=True) for x in s]
            p = [jnp.exp(s[j] - mt[j]).astype(BF16) for j in range(group)]
            pv = [_nn(p[j], jnp.concatenate([vcat[j], ones_b], axis=1)) for j in range(group)]
            for j in range(group):
                r = rows(locs[j])
                m_ref[bi, r, :] = jnp.where(head0, mt[j][:ATT_BLOCK], mt[j][ATT_BLOCK:])
                l_ref[bi, r, :] = jnp.where(head0, pv[j][:ATT_BLOCK, LANES:], pv[j][ATT_BLOCK:, LANES:])
                acc_ref[bi, r, :] = jnp.where(head0, pv[j][:ATT_BLOCK, :LANES], pv[j][ATT_BLOCK:, :LANES])
            return carry

        lax.fori_loop(0, dil * nblk // group, tiles, 0)

    nb = len(DILATED_BRANCHES)
    chunk = 2 * ATT_BLOCK

    def merge(c, carry):
        r = pl.ds(pl.multiple_of(c * chunk, chunk), chunk)
        ms = [m_ref[n, r, :] for n in range(nb)]
        m = functools.reduce(jnp.maximum, ms)
        ws = [jnp.exp(x - m) for x in ms]
        den = functools.reduce(jnp.add, [l_ref[n, r, :] * ws[n] for n in range(nb)])
        num = functools.reduce(jnp.add, [acc_ref[n, r, :] * ws[n] for n in range(nb)])
        o_ref[r, :] = num / den
        return carry

    lax.fori_loop(0, super_rows // chunk, merge, 0)


def _att_prompt(q, k, v, bias_tiles):
    b, npair, t, _ = q.shape
    sr = ATT_SUPER
    nb = len(DILATED_BRANCHES)
    return pl.pallas_call(
        functools.partial(_att_prompt_kernel, super_rows=sr),
        grid=(b, npair, t // sr),
        in_specs=[pl.BlockSpec((None, None, sr, LANES), lambda i, p, s: (i, p, s, 0)),
                  pl.BlockSpec((None, None, t, LANES), lambda i, p, s: (i, p, 0, 0)),
                  pl.BlockSpec((None, None, t, LANES), lambda i, p, s: (i, p, 0, 0)),
                  pl.BlockSpec((nb, 2, 2, ATT_BLOCK, 2 * ATT_BLOCK), lambda i, p, s: (0, 0, p, 0, 0))],
        out_specs=pl.BlockSpec((None, None, sr, LANES), lambda i, p, s: (i, p, s, 0)),
        out_shape=jax.ShapeDtypeStruct((b, npair, t, LANES), F32),
        scratch_shapes=[pltpu.VMEM((nb, sr, LANES), F32)] * 3,
        compiler_params=_cparams("arbitrary", "arbitrary", "arbitrary"),
        name="att_prompt",
    )(q, k, v, bias_tiles)


def _att_sample_kernel(q_ref, kn_ref, vn_ref, kt_ref, vt_ref, bias_ref, b0_ref, o_ref):
    nh, hd, l = kt_ref.shape
    nb = bias_ref.shape[0]
    q = q_ref[...]
    qb = q.astype(BF16)
    head = lax.broadcasted_iota(jnp.int32, (nh, 1), 0)
    logit = jnp.zeros((nh, l), F32)
    for h in range(nh):
        s = _nn(qb, kt_ref[h].astype(BF16))
        logit = jnp.where(head == h, s, logit)
    s0 = jnp.sum(q * kn_ref[...], axis=-1, keepdims=True)
    ls = [logit + bias_ref[n] for n in range(nb)]
    s_self = [s0 + b0_ref[n] for n in range(nb)]
    m = functools.reduce(jnp.maximum, [jnp.max(x, axis=-1, keepdims=True) for x in ls] + s_self)
    p = functools.reduce(jnp.add, [jnp.exp(x - m) for x in ls])
    p0 = functools.reduce(jnp.add, [jnp.exp(x - m) for x in s_self])
    den = jnp.sum(p, axis=-1, keepdims=True) + p0
    pb = p.astype(BF16)
    out = p0 * vn_ref[...]
    for h in range(nh):
        out = out + jnp.where(head == h, _nt(pb, vt_ref[h].astype(BF16)), 0.0)
    o_ref[...] = out / den


def _att_sample_job(q, k_new, v_new, cache_kt, cache_vt, rel_bias, seq_of_step):
    b, nh, hd = q.shape
    l = cache_kt.shape[-1]
    bias = _branch_bias(rel_bias)
    back = l - np.arange(l)
    tabs = []
    for n, (window, dil) in enumerate(DILATED_BRANCHES):
        used = (back % dil == 0) & (back <= window)
        tabs.append(jnp.where(used[None, :], bias[n][:, np.where(used, back // dil, 0)], -jnp.inf))
    bias_pos = jnp.stack(tabs)
    bias0 = bias[:, :, 0][..., None]
    row = pl.BlockSpec((None, nh, hd), lambda *g: (seq_of_step(*g), 0, 0))
    cache = pl.BlockSpec((None, nh, hd, l), lambda *g: (seq_of_step(*g), 0, 0, 0))
    full = lambda a: pl.BlockSpec(a.shape, lambda *g: (0,) * a.ndim)
    return dict(body=_att_sample_kernel, args=[q, k_new, v_new, cache_kt, cache_vt, bias_pos, bias0],
                in_specs=[row, row, row, cache, cache, full(bias_pos), full(bias0)],
                out_specs=[row], out_shape=[jax.ShapeDtypeStruct((b, nh, hd), F32)])


def _att_sample(q, k_new, v_new, cache_kt, cache_vt, rel_bias):
    job = _att_sample_job(q, k_new, v_new, cache_kt, cache_vt, rel_bias, lambda i: i)
    return pl.pallas_call(
        job["body"],
        grid=(q.shape[0],),
        in_specs=job["in_specs"],
        out_specs=job["out_specs"][0],
        out_shape=job["out_shape"][0],
        compiler_params=_cparams("arbitrary"),
        name="att_sample",
    )(*job["args"])


def _rwkv_features(u, u_prev, mu, w0, wlb, a0, alb, kk_scale, ka, rw):
    um = u + (u_prev - u) * mu
    r = um[:, :rw]
    k = um[:, rw:2 * rw]
    v = um[:, 2 * rw:3 * rw]
    xw = um[:, 3 * rw:3 * rw + LORA_W]
    xa = um[:, 3 * rw + LORA_W:3 * rw + 2 * LORA_W]
    wl = w0 + _nn(jnp.tanh(xw).astype(BF16), wlb)
    logw = -math.exp(-0.5) * jax.nn.sigmoid(wl)
    a = jax.nn.sigmoid(a0 + _nn(xa.astype(BF16), alb))
    kk_raw = k * kk_scale
    k2 = k * (1.0 + (a - 1.0) * ka)
    return r, k2, v, logw, kk_raw, a


def _rwkv_prompt_kernel(u_ref, prev0_ref, s0_ref, mu_ref, w0_ref, wlb_ref, a0_ref, alb_ref, kk_ref, ka_ref,
                        rk_ref, lw_ref, lb_ref, y_ref, sout_ref, s_scr, carry_scr, *, rw, rider=None):
    c = pl.program_id(1)
    nc = pl.num_programs(1)
    rows = u_ref.shape[0]
    ch = RWKV_CHUNK
    ng = rows // ch
    nh = rw // HEAD_DIM
    hd = HEAD_DIM
    ntile = rw // LANES

    @pl.when(c == 0)
    def _():
        s_scr[...] = jnp.zeros(s_scr.shape, F32)
        for h in range(nh):
            s_scr[h, :, (h % 2) * hd:(h % 2 + 1) * hd] = s0_ref[h]
        carry_scr[...] = prev0_ref[...]

    if rider is not None:
        rider()

    u = u_ref[...]
    shifted = pltpu.roll(u, 1, 0)
    row8 = lax.broadcasted_iota(jnp.int32, (8, 1), 0)
    u_prev = jnp.concatenate([jnp.where(row8 == 0, carry_scr[...], shifted[:8]), shifted[8:]], axis=0)
    carry_scr[...] = u[rows - 1:rows, :]
    r, k2, v, logw, kk_raw, a = _rwkv_features(u, u_prev, mu_ref[...], w0_ref[...], wlb_ref[...], a0_ref[...],
                                               alb_ref[...], kk_ref[...], ka_ref[...], rw)
    lane = lax.broadcasted_iota(jnp.int32, (1, LANES), 1)
    left = lane < hd

    def head_sum(x):
        tiles = []
        for q in range(ntile):
            t = x[:, q * LANES:(q + 1) * LANES]
            s_left = jnp.sum(jnp.where(left, t, 0.0), axis=-1, keepdims=True)
            s_right = jnp.sum(jnp.where(left, 0.0, t), axis=-1, keepdims=True)
            tiles.append(jnp.where(left, s_left, s_right))
        return jnp.concatenate(tiles, axis=1)

    kk = kk_raw * lax.rsqrt(jnp.maximum(head_sum(kk_raw * kk_raw), 1e-24))
    beta = kk * a
    v_x = pltpu.roll(v, hd, 1)

    ti = lax.broadcasted_iota(jnp.int32, (ch, ch), 0)
    si = lax.broadcasted_iota(jnp.int32, (ch, ch), 1)
    tri_b = (ti >= si).astype(BF16)
    row_c = lax.broadcasted_iota(jnp.int32, (ch, 1), 0)
    t2 = lax.broadcasted_iota(jnp.int32, (2 * ch, 2 * ch), 0)
    s2 = lax.broadcasted_iota(jnp.int32, (2 * ch, 2 * ch), 1)
    tt = jnp.where(t2 >= ch, t2 - ch, t2)
    ss = jnp.where(s2 >= ch, s2 - ch, s2)
    score_mask = (tt - ss) >= jnp.where(t2 >= ch, 0, 1)
    tcol = lax.broadcasted_iota(jnp.int32, (ch, LANES), 1)
    trow = lax.broadcasted_iota(jnp.int32, (ch, LANES), 0)
    eye_right = (tcol == trow + ch).astype(F32)
    zeros_b = jnp.zeros((ch, LANES), BF16)
    own_of = [left if h % 2 == 0 else jnp.logical_not(left) for h in range(nh)]
    tile_of = [slice((h // 2) * LANES, (h // 2 + 1) * LANES) for h in range(nh)]
    xtile_of = [slice((((h + 1) // 2) % ntile) * LANES, (((h + 1) // 2) % ntile + 1) * LANES) for h in range(nh)]

    nu = ng * nh
    lhs, rhs, bk, vx, at_b, rt_own, pc = [], [], [], [], [], [], []
    for g in range(ng):
        rs = slice(g * ch, (g + 1) * ch)
        l1 = logw[rs].astype(BF16)
        rem = logw[rs] - l1.astype(F32)
        l2 = rem.astype(BF16)
        l3 = (rem - l2.astype(F32)).astype(BF16)
        cum3 = _nn(tri_b, jnp.concatenate([l1, l2, l3], axis=1))
        cum = cum3[:, :rw] + (cum3[:, rw:2 * rw] + cum3[:, 2 * rw:])
        ctot = cum[ch - 1:ch, :]
        pc_g = jnp.exp(ctot)
        e_cur = jnp.exp(cum)
        e_neg = jnp.exp(-cum)
        e_prev = jnp.where(row_c == 0, 1.0, pltpu.roll(e_cur, 1, 0))
        e_end = pc_g * e_neg
        rt_f = r[rs] * e_cur
        at_g = (-kk[rs] * e_prev).astype(BF16)
        rt_g = rt_f.astype(BF16)
        bt_g = (beta[rs] * e_neg).astype(BF16)
        kt_g = (k2[rs] * e_neg).astype(BF16)
        bh_g = (beta[rs] * e_end).astype(BF16)
        kh_g = (k2[rs] * e_end).astype(BF16)
        vx_g = v_x[rs].astype(BF16)
        for h in range(nh):
            own, tl = own_of[h], tile_of[h]
            zb = jnp.zeros((), BF16)
            lhs.append(jnp.concatenate([jnp.where(own, at_g[:, tl], zb), jnp.where(own, rt_g[:, tl], zb)], axis=0))
            rhs.append(jnp.concatenate([bt_g[:, tl], kt_g[:, tl]], axis=0))
            bk.append(jnp.concatenate([jnp.where(own, bh_g[:, tl], zb), jnp.where(own, kh_g[:, tl], zb)], axis=0))
            vx.append(jnp.where(own, zb, vx_g[:, xtile_of[h]]))
            at_b.append(at_g[:, tl])
            rt_own.append(jnp.where(own, rt_f[:, tl], 0.0))
            pc.append(pc_g[:, tl])

    units = range(nu)
    own_u = [own_of[i % nh] for i in units]
    sc = [jnp.where(score_mask, _nt(lhs[i], rhs[i]), 0.0) for i in units]
    top_b = [sc[i][:ch].astype(BF16) for i in units]
    bot_b = [sc[i][ch:].astype(BF16) for i in units]
    aakv = [_nn(top_b[i], jnp.concatenate([zeros_b, vx[i]], axis=0)) for i in units]
    x = [jnp.where(left, sc[i][:ch], eye_right) for i in units]
    npow = 1
    while npow < ch:
        xb = [x[i].astype(BF16) for i in units]
        x = [_nn(xb[i], jnp.concatenate([xb[i], zeros_b], axis=0)) + jnp.where(left, 0.0, x[i]) for i in units]
        npow *= 2
    z_b = [jnp.where(own_u[i], at_b[i], aakv[i].astype(BF16)) for i in units]
    wu = [_nn(x[i].astype(BF16), jnp.concatenate([zeros_b, z_b[i]], axis=0)) for i in units]
    rhs2 = [jnp.concatenate([wu[i].astype(BF16), vx[i]], axis=0) for i in units]
    qy = [_nn(bot_b[i], rhs2[i]) for i in units]
    pg = [_tn(rhs2[i], bk[i]) for i in units]

    bonus = head_sum(r * k2 * rk_ref[...]) * v
    s_cur = [s_scr[h] for h in range(nh)]
    y_sw = []
    for g in range(ng):
        y_u = []
        for h in range(nh):
            i = g * nh + h
            own = own_of[h]
            s_b = s_cur[h].astype(BF16)
            s_pad = jnp.concatenate([zeros_b, s_b] if h % 2 == 0 else [s_b, zeros_b], axis=0)
            qhat_b = jnp.where(own, rt_own[i] + qy[i], 0.0).astype(BF16)
            y_u.append(jnp.where(own, 0.0, qy[i] + _nt(qhat_b, s_pad)))
            g0 = pg[i][ch:] if h % 2 == 0 else pg[i][:ch]
            s_cur[h] = s_cur[h] * pc[i] + _nn(s_b, pg[i].astype(BF16)) + g0
        y_sw.append(jnp.concatenate([y_u[(2 * q - 1) % nh] + y_u[2 * q] for q in range(ntile)], axis=1))
    for h in range(nh):
        s_scr[h] = s_cur[h]
    y = pltpu.roll(jnp.concatenate(y_sw, axis=0), rw - hd, 1)
    mean = head_sum(y) * (1.0 / hd)
    dev = y - mean
    var = head_sum(dev * dev) * (1.0 / hd)
    y_ref[...] = dev * lax.rsqrt(var + GN_EPS) * lw_ref[...] + lb_ref[...] + bonus

    @pl.when(c == nc - 1)
    def _():
        for h in range(nh):
            sout_ref[h] = s_scr[h, :, (h % 2) * hd:(h % 2 + 1) * hd]


def _rwkv_rider_kernel(*refs, rw, n_in, rider_body, rider_in, rider_out):
    own_in = refs[:n_in]
    r_in = refs[n_in:n_in + rider_in]
    own_out = refs[n_in + rider_in:n_in + rider_in + 2]
    r_out = refs[n_in + rider_in + 2:n_in + rider_in + 2 + rider_out]
    scratch = refs[n_in + rider_in + 2 + rider_out:]
    _rwkv_prompt_kernel(*own_in, *own_out, *scratch, rw=rw, rider=lambda: rider_body(*r_in, *r_out))


def _rwkv_prompt(u, prev0, s0, p, rider=None):
    b, t, sw = u.shape
    rw = p["w0"].shape[-1]
    nh = rw // HEAD_DIM
    hd = HEAD_DIM
    rows = RWKV_CHUNK * RWKV_CHUNKS_PER_STEP
    vec = lambda a: pl.BlockSpec(a.shape, lambda i, c: (0,) * a.ndim)
    params = [p["mu"], p["w0"], p["wlb"], p["a0"], p["alb"], p["kk"], p["ka"], p["rk"], p["lw"], p["lb"]]
    state_spec = pl.BlockSpec((None, nh, hd, hd), lambda i, c: (i, 0, 0, 0))
    args = [u, prev0, s0] + params
    in_specs = [pl.BlockSpec((None, rows, sw), lambda i, c: (i, c, 0)),
                pl.BlockSpec((None, 1, sw), lambda i, c: (i, 0, 0)),
                state_spec] + [vec(a) for a in params]
    out_specs = [pl.BlockSpec((None, rows, rw), lambda i, c: (i, c, 0)), state_spec]
    out_shape = [jax.ShapeDtypeStruct((b, t, rw), F32), jax.ShapeDtypeStruct((b, nh, hd, hd), F32)]
    body = functools.partial(_rwkv_prompt_kernel, rw=rw)
    if rider is not None:
        body = functools.partial(_rwkv_rider_kernel, rw=rw, n_in=len(args), rider_body=rider["body"],
                                 rider_in=len(rider["args"]), rider_out=len(rider["out_specs"]))
        args = args + rider["args"]
        in_specs = in_specs + rider["in_specs"]
        out_specs = out_specs + rider["out_specs"]
        out_shape = out_shape + rider["out_shape"]
    return pl.pallas_call(
        body,
        grid=(b, t // rows),
        in_specs=in_specs,
        out_specs=out_specs,
        out_shape=out_shape,
        scratch_shapes=[pltpu.VMEM((nh, hd, LANES), F32), pltpu.VMEM((1, sw), F32)],
        compiler_params=_cparams("arbitrary", "arbitrary"),
        name="rwkv_prompt",
    )(*args)


def _rwkv_feat_kernel(u_ref, prev_ref, mu_ref, w0_ref, wlb_ref, a0_ref, alb_ref, kk_ref, ka_ref,
                      r_ref, k_ref, v_ref, w_ref, kkr_ref, a_ref, *, rw):
    r, k2, v, logw, kk_raw, a = _rwkv_features(u_ref[...], prev_ref[...], mu_ref[...], w0_ref[...], wlb_ref[...],
                                               a0_ref[...], alb_ref[...], kk_ref[...], ka_ref[...], rw)
    r_ref[...] = r.T
    k_ref[...] = k2.T
    v_ref[...] = v.T
    w_ref[...] = jnp.exp(logw).T
    kkr_ref[...] = kk_raw.T
    a_ref[...] = a.T


def _rwkv_step_kernel(s_ref, r_ref, k_ref, v_ref, w_ref, kkr_ref, a_ref, rk_ref, lw_ref, lb_ref,
                      y_ref, sout_ref, y_scr):
    hd = s_ref.shape[0]
    kk_raw = kkr_ref[...]
    kk = kk_raw / jnp.maximum(jnp.sqrt(jnp.sum(kk_raw * kk_raw, axis=0, keepdims=True)), 1e-12)
    beta = kk * a_ref[...]
    w, k2, r, v = w_ref[...], k_ref[...], r_ref[...], v_ref[...]

    def value_row(i, carry):
        s = s_ref[i]
        sa = jnp.sum(s * kk, axis=0, keepdims=True)
        s_new = s * w - sa * beta + v_ref[pl.ds(i, 1), :] * k2
        sout_ref[i] = s_new
        y_scr[pl.ds(i, 1), :] = jnp.sum(s_new * r, axis=0, keepdims=True)
        return carry

    lax.fori_loop(0, hd, value_row, 0)
    y = y_scr[...]
    mean = jnp.mean(y, axis=0, keepdims=True)
    var = jnp.mean(jnp.square(y - mean), axis=0, keepdims=True)
    yn = (y - mean) * lax.rsqrt(var + GN_EPS) * lw_ref[...] + lb_ref[...]
    y_ref[...] = yn + jnp.sum(r * k2 * rk_ref[...], axis=0, keepdims=True) * v


def _rwkv_sample(u, prev, s0_t, p):
    b, sw = u.shape
    rw = p["w0"].shape[-1]
    nh = rw // HEAD_DIM
    hd = HEAD_DIM
    full = lambda a: pl.BlockSpec(a.shape, lambda i: (0,) * a.ndim)
    fparams = [p["mu"], p["w0"], p["wlb"], p["a0"], p["alb"], p["kk"], p["ka"]]
    feats = pl.pallas_call(
        functools.partial(_rwkv_feat_kernel, rw=rw),
        grid=(1,),
        in_specs=[full(u), full(prev)] + [full(a) for a in fparams],
        out_specs=[pl.BlockSpec((rw, b), lambda i: (0, 0))] * 6,
        out_shape=[jax.ShapeDtypeStruct((rw, b), F32)] * 6,
        compiler_params=_cparams("arbitrary"),
        name="rwkv_sample_features",
    )(u, prev, *fparams)
    chan = pl.BlockSpec((hd, b), lambda h: (h, 0))
    col = pl.BlockSpec((hd, 1), lambda h: (h, 0))
    state_spec = pl.BlockSpec((None, hd, hd, b), lambda h: (h, 0, 0, 0))
    cols = [p[n].reshape(rw, 1) for n in ("rk", "lw", "lb")]
    y_t, s_new = pl.pallas_call(
        _rwkv_step_kernel,
        grid=(nh,),
        in_specs=[state_spec] + [chan] * 6 + [col] * 3,
        out_specs=[chan, state_spec],
        out_shape=[jax.ShapeDtypeStruct((rw, b), F32), jax.ShapeDtypeStruct((nh, hd, hd, b), F32)],
        scratch_shapes=[pltpu.VMEM((hd, b), F32)],
        compiler_params=_cparams("arbitrary"),
        name="rwkv_sample_step",
    )(s0_t, *feats, *cols)
    return y_t.T, s_new


def _outproj_kernel(x_ref, att_ref, g_ref, rw_ref, gr_ref, gate_ref, w_ref, fnw_ref, o_ref, *, final_norm):
    npair = att_ref.shape[0]
    mixed = [(att_ref[p] * _silu(g_ref[p])).astype(BF16) for p in range(npair)]
    mixed.append((rw_ref[...] * _silu(gr_ref[...])).astype(BF16))
    acc = _nn(jnp.concatenate(mixed, axis=1), w_ref[...])
    xo = x_ref[...] + gate_ref[...] * acc
    if final_norm:
        xo = xo * lax.rsqrt(jnp.mean(xo * xo, axis=-1, keepdims=True) + NORM_EPS) * fnw_ref[...]
    o_ref[...] = xo


def _outproj(x, att, g_att, rwk, g_rwkv, gate, w_out_bf16, final_norm_w, final_norm):
    b, t, d = x.shape
    npair = att.shape[1]
    rw = rwk.shape[-1]
    tm = min(ROW_TILE, t)
    per_row = gate.shape[1] != 1
    gate_spec = (pl.BlockSpec((None, tm, d), lambda i, j: (i, j, 0)) if per_row
                 else pl.BlockSpec((None, 1, d), lambda i, j: (i, 0, 0)))
    pair_spec = pl.BlockSpec((None, npair, tm, LANES), lambda i, j: (i, 0, j, 0))
    return pl.pallas_call(
        functools.partial(_outproj_kernel, final_norm=final_norm),
        grid=(b, t // tm),
        in_specs=[pl.BlockSpec((None, tm, d), lambda i, j: (i, j, 0)), pair_spec, pair_spec,
                  pl.BlockSpec((None, tm, rw), lambda i, j: (i, j, 0)),
                  pl.BlockSpec((None, tm, rw), lambda i, j: (i, j, 0)),
                  gate_spec,
                  pl.BlockSpec(w_out_bf16.shape, lambda i, j: (0, 0)),
                  pl.BlockSpec((1, d), lambda i, j: (0, 0))],
        out_specs=pl.BlockSpec((None, tm, d), lambda i, j: (i, j, 0)),
        out_shape=jax.ShapeDtypeStruct((b, t, d), F32),
        compiler_params=_cparams("arbitrary", "arbitrary"),
        name="out_proj",
    )(x, att, g_att, rwk, g_rwkv, gate, w_out_bf16, final_norm_w.reshape(1, d))


def _pairs_to_heads(a):
    b, npair, t, _ = a.shape
    return a.transpose(0, 2, 1, 3).reshape(b, t, npair * (LANES // HEAD_DIM), HEAD_DIM)


def kernel(x_prompt, x_sample, cache_win_k, cache_win_v, state_wkv, state_shift, c_prompt, c_sample, rel_bias, norm_w, ada_w, ada_b, w_in, mu_shift, w0, w_lora_b, a0, a_lora_b, k_k, k_a, r_k, ln_x_w, ln_x_b, w_out, final_norm_w):
    depth = norm_w.shape[0]
    bp, tp, d = x_prompt.shape
    bs, ts, _ = x_sample.shape
    assert ts == 1, "the sample group decodes one token per sequence"
    rw = w0.shape[-1]
    att_w = w_out.shape[1] - rw
    shift_w = mu_shift.shape[-1]
    nh_att = att_w // HEAD_DIM
    nh_rw = rw // HEAD_DIM
    assert tp % ATT_SUPER == 0 and tp % (RWKV_CHUNK * RWKV_CHUNKS_PER_STEP) == 0 and bs % 8 == 0
    assert cache_win_k.shape[2] == MAX_WINDOW
    keep = min(MAX_WINDOW, tp)

    bias_tiles = _prompt_bias_tiles(rel_bias)
    c_all = jnp.concatenate([c_prompt, c_sample], axis=0)
    npad = -c_all.shape[0] % 8
    c_all = jnp.pad(c_all, ((0, npad), (0, 0)))

    xp = x_prompt
    xs = x_sample.reshape(1, bs, d)
    outs = [[] for _ in range(8)]
    for l in range(depth):
        mod = _ada_mod(c_all, ada_w[l], ada_b[l])
        shift, scale, gate = jnp.split(mod, 3, axis=-1)
        w_in_b = w_in[l].astype(BF16)
        w_out_b = w_out[l].astype(BF16)
        p = dict(mu=mu_shift[l].reshape(1, -1), w0=w0[l].reshape(1, -1), wlb=w_lora_b[l].astype(BF16),
                 a0=a0[l].reshape(1, -1), alb=a_lora_b[l].astype(BF16), kk=k_k[l].reshape(1, -1),
                 ka=k_a[l].reshape(1, -1), rk=r_k[l].reshape(1, -1), lw=ln_x_w[l].reshape(1, -1),
                 lb=ln_x_b[l].reshape(1, -1))
        last = l == depth - 1

        sm = lambda a: a[bp:bp + bs].reshape(1, bs, d)
        q_s, k_s, v_s, g_att_s, u_s, g_rwkv_s = _inproj(xs, sm(shift), sm(scale), norm_w[l], w_in_b, att_w, shift_w)
        nat = lambda a: a[0].transpose(1, 0, 2).reshape(bs, nh_att, HEAD_DIM)
        k_new, v_new = nat(k_s), nat(v_s)
        att_s_args = (nat(q_s), k_new, v_new, cache_win_k[l].transpose(0, 2, 3, 1),
                      cache_win_v[l].transpose(0, 2, 3, 1), rel_bias)

        pm = lambda a: a[:bp].reshape(bp, 1, d)
        q, k, v, g_att, u, g_rwkv = _inproj(xp, pm(shift), pm(scale), norm_w[l], w_in_b, att_w, shift_w)
        att = _att_prompt(q, k, v, bias_tiles)
        prev0 = jnp.zeros((bp, 1, shift_w), F32)
        s0 = jnp.zeros((bp, nh_rw, HEAD_DIM, HEAD_DIM), F32)
        steps = tp // (RWKV_CHUNK * RWKV_CHUNKS_PER_STEP)
        if bs == bp * steps:
            rider = _att_sample_job(*att_s_args, lambda i, c: i * steps + c)
            y_rw, s_p, att_s = _rwkv_prompt(u, prev0, s0, p, rider)
        else:
            y_rw, s_p = _rwkv_prompt(u, prev0, s0, p)
            att_s = _att_sample(*att_s_args)
        xp = _outproj(xp, att, g_att, y_rw, g_rwkv, pm(gate), w_out_b, final_norm_w, last)
        outs[0].append(_pairs_to_heads(k[:, :, tp - keep:, :]))
        outs[1].append(_pairs_to_heads(v[:, :, tp - keep:, :]))
        outs[4].append(s_p)
        outs[6].append(u[:, -1])

        att_s = att_s.reshape(bs, att_w // LANES, LANES).transpose(1, 0, 2)[None]
        y_rw_s, s_s = _rwkv_sample(u_s[0], state_shift[l], state_wkv[l].transpose(1, 2, 3, 0), p)
        xs = _outproj(xs, att_s, g_att_s, y_rw_s[None], g_rwkv_s, sm(gate), w_out_b, final_norm_w, last)
        outs[2].append(k_new.reshape(bs, 1, nh_att, HEAD_DIM))
        outs[3].append(v_new.reshape(bs, 1, nh_att, HEAD_DIM))
        outs[5].append(s_s.transpose(3, 0, 1, 2))
        outs[7].append(u_s[0])

    stack = lambda i: jnp.stack(outs[i])
    return (xp, xs.reshape(bs, 1, d), stack(0), stack(1), stack(2), stack(3), stack(4), stack(5), stack(6),
            stack(7))
```

```python
import functools
import math

import numpy as np
import jax
import jax.numpy as jnp
from jax import lax
from jax.experimental import pallas as pl
from jax.experimental.pallas import tpu as pltpu

F32 = jnp.float32
BF16 = jnp.bfloat16
HIGHEST = lax.Precision.HIGHEST

HEAD_DIM = 64
DILATED_BRANCHES = ((128, 1), (512, 4), (2048, 16))
MAX_WINDOW = max(w for w, _ in DILATED_BRANCHES)
KEYS_PER_BRANCH = 128
N_BUCKETS = 32
BUCKET_MAX_DIST = MAX_WINDOW
LORA_W = 64
NORM_EPS = 1e-6
GN_EPS = HEAD_DIM * 1e-5

LANES = 128
VMEM_LIMIT_BYTES = 56 * 1024 * 1024

ROW_TILE = 256
ATT_BLOCK = 128
ATT_SUPER = MAX_WINDOW
ATT_TILES_PER_ITER = 4
RWKV_CHUNK = 64
RWKV_CHUNKS_PER_STEP = 4


def _cparams(*sem):
    return pltpu.CompilerParams(dimension_semantics=sem, vmem_limit_bytes=VMEM_LIMIT_BYTES)


def _silu(x):
    return x * jax.nn.sigmoid(x)


def _nt(a, b):
    return lax.dot_general(a, b, (((1,), (1,)), ((), ())), preferred_element_type=F32)


def _tn(a, b):
    return lax.dot_general(a, b, (((0,), (0,)), ((), ())), preferred_element_type=F32)


def _nn(a, b, precision=None):
    return jnp.dot(a, b, precision=precision, preferred_element_type=F32)


def _ada_kernel(c_ref, w_ref, b_ref, o_ref):
    s = _silu(c_ref[...])
    o_ref[...] = _nn(s, w_ref[...], HIGHEST) + b_ref[...]


def _ada_mod(c_all, ada_w, ada_b):
    n, d = c_all.shape
    e = ada_w.shape[1]
    tn = 512
    return pl.pallas_call(
        _ada_kernel,
        grid=(e // tn,),
        in_specs=[pl.BlockSpec((n, d), lambda j: (0, 0)),
                  pl.BlockSpec((d, tn), lambda j: (0, j)),
                  pl.BlockSpec((1, tn), lambda j: (0, j))],
        out_specs=pl.BlockSpec((n, tn), lambda j: (0, j)),
        out_shape=jax.ShapeDtypeStruct((n, e), F32),
        compiler_params=_cparams("arbitrary"),
        name="ada_mod",
    )(c_all, ada_w, ada_b.reshape(1, e))


def _inproj_kernel(x_ref, shift_ref, scale_ref, nw_ref, w_ref,
                   q_ref, k_ref, v_ref, g_ref, u_ref, gr_ref, *, att_w, shift_w):
    x = x_ref[...]
    xn = x * lax.rsqrt(jnp.mean(x * x, axis=-1, keepdims=True) + NORM_EPS) * nw_ref[...]
    h = xn * (1.0 + scale_ref[...]) + shift_ref[...]
    z = _nn(h.astype(BF16), w_ref[...])
    npair = att_w // LANES
    for p in range(npair):
        q_ref[p] = z[:, p * LANES:(p + 1) * LANES] * (HEAD_DIM ** -0.5)
        k_ref[p] = z[:, att_w + p * LANES:att_w + (p + 1) * LANES]
        v_ref[p] = z[:, 2 * att_w + p * LANES:2 * att_w + (p + 1) * LANES]
        g_ref[p] = z[:, 3 * att_w + p * LANES:3 * att_w + (p + 1) * LANES]
    u_ref[...] = z[:, 4 * att_w:4 * att_w + shift_w]
    gr_ref[...] = z[:, 4 * att_w + shift_w:]


def _inproj(x, shift, scale, norm_w, w_in_bf16, att_w, shift_w):
    b, t, d = x.shape
    in_w = w_in_bf16.shape[1]
    rw = in_w - 4 * att_w - shift_w
    npair = att_w // LANES
    tm = min(ROW_TILE, t)
    per_row = shift.shape[1] != 1
    mod_spec = (pl.BlockSpec((None, tm, d), lambda i, j: (i, j, 0)) if per_row
                else pl.BlockSpec((None, 1, d), lambda i, j: (i, 0, 0)))
    pair_spec = pl.BlockSpec((None, npair, tm, LANES), lambda i, j: (i, 0, j, 0))
    pair_shape = jax.ShapeDtypeStruct((b, npair, t, LANES), F32)
    return pl.pallas_call(
        functools.partial(_inproj_kernel, att_w=att_w, shift_w=shift_w),
        grid=(b, t // tm),
        in_specs=[pl.BlockSpec((None, tm, d), lambda i, j: (i, j, 0)), mod_spec, mod_spec,
                  pl.BlockSpec((1, d), lambda i, j: (0, 0)),
                  pl.BlockSpec((d, in_w), lambda i, j: (0, 0))],
        out_specs=[pair_spec, pair_spec, pair_spec, pair_spec,
                   pl.BlockSpec((None, tm, shift_w), lambda i, j: (i, j, 0)),
                   pl.BlockSpec((None, tm, rw), lambda i, j: (i, j, 0))],
        out_shape=[pair_shape, pair_shape, pair_shape, pair_shape,
                   jax.ShapeDtypeStruct((b, t, shift_w), F32),
                   jax.ShapeDtypeStruct((b, t, rw), F32)],
        compiler_params=_cparams("arbitrary", "arbitrary"),
        name="in_proj",
    )(x, shift, scale, norm_w.reshape(1, d), w_in_bf16)


def _t5_bucket_np(dist):
    max_exact = N_BUCKETS // 2
    nf = np.maximum(dist, max_exact).astype(np.float32)
    large = max_exact + (np.log(nf / np.float32(max_exact)) / np.float32(math.log(BUCKET_MAX_DIST / max_exact))
                         * np.float32(N_BUCKETS - max_exact)).astype(np.int32)
    large = np.minimum(large, N_BUCKETS - 1)
    return np.where(dist < max_exact, dist, large)


def _branch_bias(rel_bias):
    out = []
    for window, dil in DILATED_BRANCHES:
        dist = dil * np.arange(window // dil + 1, dtype=np.int32)
        out.append(rel_bias[_t5_bucket_np(dist)].T.astype(F32))
    return jnp.stack(out)


def _prompt_bias_tiles(rel_bias):
    bias = _branch_bias(rel_bias)
    nb, nh, _ = bias.shape
    blk, width = ATT_BLOCK, 2 * ATT_BLOCK
    pad = jnp.full((nb, nh, blk - 1), -jnp.inf, F32)
    strip = jnp.flip(jnp.concatenate([pad, bias, pad], axis=-1), axis=-1)
    length = 3 * blk - 1
    rows = jnp.broadcast_to(jnp.pad(strip, ((0, 0), (0, 0), (0, 1)))[:, :, None, :], (nb, nh, blk, length + 1))
    skew = rows.reshape(nb, nh, blk * (length + 1))[:, :, :blk * length].reshape(nb, nh, blk, length)
    tile = skew[:, :, :, blk - 1:blk - 1 + width]
    first = jnp.where((np.arange(width) >= blk)[None, None, None, :], tile, -jnp.inf)
    return jnp.stack([tile, first], axis=1)


def _att_prompt_kernel(q_ref, k_ref, v_ref, bias_ref, o_ref, m_ref, l_ref, acc_ref, *, super_rows):
    sb = pl.program_id(2)
    lane = lax.broadcasted_iota(jnp.int32, (1, LANES), 1)
    head0 = lane < HEAD_DIM
    group = ATT_TILES_PER_ITER
    ones_b = jnp.ones((2 * ATT_BLOCK, LANES), BF16)

    for bi, (_, dil) in enumerate(DILATED_BRANCHES):
        nblk = super_rows // (ATT_BLOCK * dil)
        rows = lambda s, dil=dil: (pl.ds(s, ATT_BLOCK, stride=dil) if dil > 1 else pl.ds(s, ATT_BLOCK))

        def tiles(it, carry, bi=bi, dil=dil, nblk=nblk, rows=rows):
            locs, firsts, q2, kcat, vcat = [], [], [], [], []
            for j in range(group):
                idx = it * group + j
                res = idx // nblk
                loc = res + dil * ATT_BLOCK * (idx - res * nblk)
                glob = sb * super_rows + loc
                prev = glob - dil * ATT_BLOCK
                first = prev < 0
                pstart = jnp.where(first, glob, prev)
                qt = q_ref[rows(loc), :]
                locs.append(loc)
                firsts.append(first.astype(jnp.int32))
                q2.append(jnp.concatenate([jnp.where(head0, qt, 0.0), jnp.where(head0, 0.0, qt)],
                                          axis=0).astype(BF16))
                kcat.append(jnp.concatenate([k_ref[rows(pstart), :], k_ref[rows(glob), :]], axis=0).astype(BF16))
                vcat.append(jnp.concatenate([v_ref[rows(pstart), :], v_ref[rows(glob), :]], axis=0).astype(BF16))
            s = [_nt(q2[j], kcat[j]) + bias_ref[bi, firsts[j]].reshape(2 * ATT_BLOCK, 2 * ATT_BLOCK)
                 for j in range(group)]
            mt = [jnp.max(x, axis=-1, keepdims=True) for x in s]
            p = [jnp.exp(s[j] - mt[j]).astype(BF16) for j in range(group)]
            pv = [_nn(p[j], jnp.concatenate([vcat[j], ones_b], axis=1)) for j in range(group)]
            for j in range(group):
                r = rows(locs[j])
                m_ref[bi, r, :] = jnp.where(head0, mt[j][:ATT_BLOCK], mt[j][ATT_BLOCK:])
                l_ref[bi, r, :] = jnp.where(head0, pv[j][:ATT_BLOCK, LANES:], pv[j][ATT_BLOCK:, LANES:])
                acc_ref[bi, r, :] = jnp.where(head0, pv[j][:ATT_BLOCK, :LANES], pv[j][ATT_BLOCK:, :LANES])
            return carry

        lax.fori_loop(0, dil * nblk // group, tiles, 0)

    nb = len(DILATED_BRANCHES)
    chunk = 2 * ATT_BLOCK

    def merge(c, carry):
        r = pl.ds(pl.multiple_of(c * chunk, chunk), chunk)
        ms = [m_ref[n, r, :] for n in range(nb)]
        m = functools.reduce(jnp.maximum, ms)
        ws = [jnp.exp(x - m) for x in ms]
        den = functools.reduce(jnp.add, [l_ref[n, r, :] * ws[n] for n in range(nb)])
        num = functools.reduce(jnp.add, [acc_ref[n, r, :] * ws[n] for n in range(nb)])
        o_ref[r, :] = num / den
        return carry

    lax.fori_loop(0, super_rows // chunk, merge, 0)


def _att_prompt(q, k, v, bias_tiles):
    b, npair, t, _ = q.shape
    sr = ATT_SUPER
    nb = len(DILATED_BRANCHES)
    return pl.pallas_call(
        functools.partial(_att_prompt_kernel, super_rows=sr),
        grid=(b, npair, t // sr),
        in_specs=[pl.BlockSpec((None, None, sr, LANES), lambda i, p, s: (i, p, s, 0)),
                  pl.BlockSpec((None, None, t, LANES), lambda i, p, s: (i, p, 0, 0)),
                  pl.BlockSpec((None, None, t, LANES), lambda i, p, s: (i, p, 0, 0)),
                  pl.BlockSpec((nb, 2, 2, ATT_BLOCK, 2 * ATT_BLOCK), lambda i, p, s: (0, 0, p, 0, 0))],
        out_specs=pl.BlockSpec((None, None, sr, LANES), lambda i, p, s: (i, p, s, 0)),
        out_shape=jax.ShapeDtypeStruct((b, npair, t, LANES), F32),
        scratch_shapes=[pltpu.VMEM((nb, sr, LANES), F32)] * 3,
        compiler_params=_cparams("arbitrary", "arbitrary", "arbitrary"),
        name="att_prompt",
    )(q, k, v, bias_tiles)


def _att_sample_kernel(q_ref, kn_ref, vn_ref, kt_ref, vt_ref, bias_ref, b0_ref, o_ref):
    nh, hd, l = kt_ref.shape
    nb = bias_ref.shape[0]
    q = q_ref[...]
    qb = q.astype(BF16)
    head = lax.broadcasted_iota(jnp.int32, (nh, 1), 0)
    logit = jnp.zeros((nh, l), F32)
    for h in range(nh):
        s = _nn(qb, kt_ref[h].astype(BF16))
        logit = jnp.where(head == h, s, logit)
    s0 = jnp.sum(q * kn_ref[...], axis=-1, keepdims=True)
    ls = [logit + bias_ref[n] for n in range(nb)]
    s_self = [s0 + b0_ref[n] for n in range(nb)]
    m = functools.reduce(jnp.maximum, [jnp.max(x, axis=-1, keepdims=True) for x in ls] + s_self)
    p = functools.reduce(jnp.add, [jnp.exp(x - m) for x in ls])
    p0 = functools.reduce(jnp.add, [jnp.exp(x - m) for x in s_self])
    den = jnp.sum(p, axis=-1, keepdims=True) + p0
    pb = p.astype(BF16)
    out = p0 * vn_ref[...]
    for h in range(nh):
        out = out + jnp.where(head == h, _nt(pb, vt_ref[h].astype(BF16)), 0.0)
    o_ref[...] = out / den


def _att_sample_job(q, k_new, v_new, cache_kt, cache_vt, rel_bias, seq_of_step):
    b, nh, hd = q.shape
    l = cache_kt.shape[-1]
    bias = _branch_bias(rel_bias)
    back = l - np.arange(l)
    tabs = []
    for n, (window, dil) in enumerate(DILATED_BRANCHES):
        used = (back % dil == 0) & (back <= window)
        tabs.append(jnp.where(used[None, :], bias[n][:, np.where(used, back // dil, 0)], -jnp.inf))
    bias_pos = jnp.stack(tabs)
    bias0 = bias[:, :, 0][..., None]
    row = pl.BlockSpec((None, nh, hd), lambda *g: (seq_of_step(*g), 0, 0))
    cache = pl.BlockSpec((None, nh, hd, l), lambda *g: (seq_of_step(*g), 0, 0, 0))
    full = lambda a: pl.BlockSpec(a.shape, lambda *g: (0,) * a.ndim)
    return dict(body=_att_sample_kernel, args=[q, k_new, v_new, cache_kt, cache_vt, bias_pos, bias0],
                in_specs=[row, row, row, cache, cache, full(bias_pos), full(bias0)],
                out_specs=[row], out_shape=[jax.ShapeDtypeStruct((b, nh, hd), F32)])


def _att_sample(q, k_new, v_new, cache_kt, cache_vt, rel_bias):
    job = _att_sample_job(q, k_new, v_new, cache_kt, cache_vt, rel_bias, lambda i: i)
    return pl.pallas_call(
        job["body"],
        grid=(q.shape[0],),
        in_specs=job["in_specs"],
        out_specs=job["out_specs"][0],
        out_shape=job["out_shape"][0],
        compiler_params=_cparams("arbitrary"),
        name="att_sample",
    )(*job["args"])


def _gated_outproj(x_ref, att_ref, g_ref, rwk, gr_ref, gate_ref, w_ref, fnw_ref, final_norm):
    npair = att_ref.shape[0]
    mixed = [(att_ref[p] * _silu(g_ref[p])).astype(BF16) for p in range(npair)]
    mixed.append((rwk * _silu(gr_ref[...])).astype(BF16))
    acc = _nn(jnp.concatenate(mixed, axis=1), w_ref[...])
    xo = x_ref[...] + gate_ref[...] * acc
    if final_norm:
        xo = xo * lax.rsqrt(jnp.mean(xo * xo, axis=-1, keepdims=True) + NORM_EPS) * fnw_ref[...]
    return xo


def _rwkv_features(u, u_prev, mu, w0, wlb, a0, alb, kk_scale, ka, rw):
    um = u + (u_prev - u) * mu
    r = um[:, :rw]
    k = um[:, rw:2 * rw]
    v = um[:, 2 * rw:3 * rw]
    xw = um[:, 3 * rw:3 * rw + LORA_W]
    xa = um[:, 3 * rw + LORA_W:3 * rw + 2 * LORA_W]
    wl = w0 + _nn(jnp.tanh(xw).astype(BF16), wlb)
    logw = -math.exp(-0.5) * jax.nn.sigmoid(wl)
    a = jax.nn.sigmoid(a0 + _nn(xa.astype(BF16), alb))
    kk_raw = k * kk_scale
    k2 = k * (1.0 + (a - 1.0) * ka)
    return r, k2, v, logw, kk_raw, a


def _rwkv_prompt_kernel(u_ref, prev0_ref, s0_ref, mu_ref, w0_ref, wlb_ref, a0_ref, alb_ref, kk_ref, ka_ref,
                        rk_ref, lw_ref, lb_ref, y_ref, sout_ref, s_scr, carry_scr, *, rw, rider=None, finish=None):
    c = pl.program_id(1)
    nc = pl.num_programs(1)
    rows = u_ref.shape[0]
    ch = RWKV_CHUNK
    ng = rows // ch
    nh = rw // HEAD_DIM
    hd = HEAD_DIM
    ntile = rw // LANES

    @pl.when(c == 0)
    def _():
        s_scr[...] = jnp.zeros(s_scr.shape, F32)
        for h in range(nh):
            s_scr[h, :, (h % 2) * hd:(h % 2 + 1) * hd] = s0_ref[h]
        carry_scr[...] = prev0_ref[...]

    if rider is not None:
        rider()

    u = u_ref[...]
    shifted = pltpu.roll(u, 1, 0)
    row8 = lax.broadcasted_iota(jnp.int32, (8, 1), 0)
    u_prev = jnp.concatenate([jnp.where(row8 == 0, carry_scr[...], shifted[:8]), shifted[8:]], axis=0)
    carry_scr[...] = u[rows - 1:rows, :]
    r, k2, v, logw, kk_raw, a = _rwkv_features(u, u_prev, mu_ref[...], w0_ref[...], wlb_ref[...], a0_ref[...],
                                               alb_ref[...], kk_ref[...], ka_ref[...], rw)
    lane = lax.broadcasted_iota(jnp.int32, (1, LANES), 1)
    left = lane < hd

    def head_sum(x):
        tiles = []
        for q in range(ntile):
            t = x[:, q * LANES:(q + 1) * LANES]
            s_left = jnp.sum(jnp.where(left, t, 0.0), axis=-1, keepdims=True)
            s_right = jnp.sum(jnp.where(left, 0.0, t), axis=-1, keepdims=True)
            tiles.append(jnp.where(left, s_left, s_right))
        return jnp.concatenate(tiles, axis=1)

    kk = kk_raw * lax.rsqrt(jnp.maximum(head_sum(kk_raw * kk_raw), 1e-24))
    beta = kk * a
    v_x = pltpu.roll(v, hd, 1)

    ti = lax.broadcasted_iota(jnp.int32, (ch, ch), 0)
    si = lax.broadcasted_iota(jnp.int32, (ch, ch), 1)
    tri_b = (ti >= si).astype(BF16)
    row_c = lax.broadcasted_iota(jnp.int32, (ch, 1), 0)
    t2 = lax.broadcasted_iota(jnp.int32, (2 * ch, 2 * ch), 0)
    s2 = lax.broadcasted_iota(jnp.int32, (2 * ch, 2 * ch), 1)
    tt = jnp.where(t2 >= ch, t2 - ch, t2)
    ss = jnp.where(s2 >= ch, s2 - ch, s2)
    score_mask = (tt - ss) >= jnp.where(t2 >= ch, 0, 1)
    tcol = lax.broadcasted_iota(jnp.int32, (ch, LANES), 1)
    trow = lax.broadcasted_iota(jnp.int32, (ch, LANES), 0)
    eye_right = (tcol == trow + ch).astype(F32)
    zeros_b = jnp.zeros((ch, LANES), BF16)
    own_of = [left if h % 2 == 0 else jnp.logical_not(left) for h in range(nh)]
    tile_of = [slice((h // 2) * LANES, (h // 2 + 1) * LANES) for h in range(nh)]
    xtile_of = [slice((((h + 1) // 2) % ntile) * LANES, (((h + 1) // 2) % ntile + 1) * LANES) for h in range(nh)]

    nu = ng * nh
    lhs, rhs, bk, vx, at_b, rt_own, pc = [], [], [], [], [], [], []
    for g in range(ng):
        rs = slice(g * ch, (g + 1) * ch)
        l1 = logw[rs].astype(BF16)
        rem = logw[rs] - l1.astype(F32)
        l2 = rem.astype(BF16)
        l3 = (rem - l2.astype(F32)).astype(BF16)
        cum3 = _nn(tri_b, jnp.concatenate([l1, l2, l3], axis=1))
        cum = cum3[:, :rw] + (cum3[:, rw:2 * rw] + cum3[:, 2 * rw:])
        ctot = cum[ch - 1:ch, :]
        pc_g = jnp.exp(ctot)
        e_cur = jnp.exp(cum)
        e_neg = jnp.exp(-cum)
        e_prev = jnp.where(row_c == 0, 1.0, pltpu.roll(e_cur, 1, 0))
        e_end = pc_g * e_neg
        rt_f = r[rs] * e_cur
        at_g = (-kk[rs] * e_prev).astype(BF16)
        rt_g = rt_f.astype(BF16)
        bt_g = (beta[rs] * e_neg).astype(BF16)
        kt_g = (k2[rs] * e_neg).astype(BF16)
        bh_g = (beta[rs] * e_end).astype(BF16)
        kh_g = (k2[rs] * e_end).astype(BF16)
        vx_g = v_x[rs].astype(BF16)
        for h in range(nh):
            own, tl = own_of[h], tile_of[h]
            zb = jnp.zeros((), BF16)
            lhs.append(jnp.concatenate([jnp.where(own, at_g[:, tl], zb), jnp.where(own, rt_g[:, tl], zb)], axis=0))
            rhs.append(jnp.concatenate([bt_g[:, tl], kt_g[:, tl]], axis=0))
            bk.append(jnp.concatenate([jnp.where(own, bh_g[:, tl], zb), jnp.where(own, kh_g[:, tl], zb)], axis=0))
            vx.append(jnp.where(own, zb, vx_g[:, xtile_of[h]]))
            at_b.append(at_g[:, tl])
            rt_own.append(jnp.where(own, rt_f[:, tl], 0.0))
            pc.append(pc_g[:, tl])

    units = range(nu)
    own_u = [own_of[i % nh] for i in units]
    sc = [jnp.where(score_mask, _nt(lhs[i], rhs[i]), 0.0) for i in units]
    top_b = [sc[i][:ch].astype(BF16) for i in units]
    bot_b = [sc[i][ch:].astype(BF16) for i in units]
    aakv = [_nn(top_b[i], jnp.concatenate([zeros_b, vx[i]], axis=0)) for i in units]
    x = [jnp.where(left, sc[i][:ch], eye_right) for i in units]
    npow = 1
    while npow < ch:
        xb = [x[i].astype(BF16) for i in units]
        x = [_nn(xb[i], jnp.concatenate([xb[i], zeros_b], axis=0)) + jnp.where(left, 0.0, x[i]) for i in units]
        npow *= 2
    z_b = [jnp.where(own_u[i], at_b[i], aakv[i].astype(BF16)) for i in units]
    wu = [_nn(x[i].astype(BF16), jnp.concatenate([zeros_b, z_b[i]], axis=0)) for i in units]
    rhs2 = [jnp.concatenate([wu[i].astype(BF16), vx[i]], axis=0) for i in units]
    qy = [_nn(bot_b[i], rhs2[i]) for i in units]
    pg = [_tn(rhs2[i], bk[i]) for i in units]

    bonus = head_sum(r * k2 * rk_ref[...]) * v
    s_cur = [s_scr[h] for h in range(nh)]
    y_sw = []
    for g in range(ng):
        y_u = []
        for h in range(nh):
            i = g * nh + h
            own = own_of[h]
            s_b = s_cur[h].astype(BF16)
            s_pad = jnp.concatenate([zeros_b, s_b] if h % 2 == 0 else [s_b, zeros_b], axis=0)
            qhat_b = jnp.where(own, rt_own[i] + qy[i], 0.0).astype(BF16)
            y_u.append(jnp.where(own, 0.0, qy[i] + _nt(qhat_b, s_pad)))
            g0 = pg[i][ch:] if h % 2 == 0 else pg[i][:ch]
            s_cur[h] = s_cur[h] * pc[i] + _nn(s_b, pg[i].astype(BF16)) + g0
        y_sw.append(jnp.concatenate([y_u[(2 * q - 1) % nh] + y_u[2 * q] for q in range(ntile)], axis=1))
    for h in range(nh):
        s_scr[h] = s_cur[h]
    y = pltpu.roll(jnp.concatenate(y_sw, axis=0), rw - hd, 1)
    mean = head_sum(y) * (1.0 / hd)
    dev = y - mean
    var = head_sum(dev * dev) * (1.0 / hd)
    y = dev * lax.rsqrt(var + GN_EPS) * lw_ref[...] + lb_ref[...] + bonus
    y_ref[...] = y if finish is None else finish(y)

    @pl.when(c == nc - 1)
    def _():
        for h in range(nh):
            sout_ref[h] = s_scr[h, :, (h % 2) * hd:(h % 2 + 1) * hd]


def _rwkv_fused_kernel(*refs, rw, n_in, rider_body, rider_in, rider_out, post_in, final_norm):
    own_in, rest = refs[:n_in], refs[n_in:]
    r_in, rest = rest[:rider_in], rest[rider_in:]
    p_in, rest = rest[:post_in], rest[post_in:]
    own_out, rest = rest[:2], rest[2:]
    r_out, scratch = rest[:rider_out], rest[rider_out:]
    rider = (lambda: rider_body(*r_in, *r_out)) if rider_body is not None else None
    finish = None
    if post_in:
        x_ref, att_ref, g_ref, gr_ref, gate_ref, w_ref, fnw_ref = p_in
        finish = lambda y: _gated_outproj(x_ref, att_ref, g_ref, y, gr_ref, gate_ref, w_ref, fnw_ref, final_norm)
    _rwkv_prompt_kernel(*own_in, *own_out, *scratch, rw=rw, rider=rider, finish=finish)


def _rwkv_prompt(u, prev0, s0, p, rider=None, post=None):
    b, t, sw = u.shape
    rw = p["w0"].shape[-1]
    nh = rw // HEAD_DIM
    hd = HEAD_DIM
    rows = RWKV_CHUNK * RWKV_CHUNKS_PER_STEP
    vec = lambda a: pl.BlockSpec(a.shape, lambda i, c: (0,) * a.ndim)
    params = [p["mu"], p["w0"], p["wlb"], p["a0"], p["alb"], p["kk"], p["ka"], p["rk"], p["lw"], p["lb"]]
    state_spec = pl.BlockSpec((None, nh, hd, hd), lambda i, c: (i, 0, 0, 0))
    args = [u, prev0, s0] + params
    in_specs = [pl.BlockSpec((None, rows, sw), lambda i, c: (i, c, 0)),
                pl.BlockSpec((None, 1, sw), lambda i, c: (i, 0, 0)),
                state_spec] + [vec(a) for a in params]
    n_in = len(args)
    out_w = rw
    rider_args, rider_in_specs, rider_out_specs, rider_out_shape, rider_body = [], [], [], [], None
    if rider is not None:
        rider_args, rider_in_specs = rider["args"], rider["in_specs"]
        rider_out_specs, rider_out_shape, rider_body = rider["out_specs"], rider["out_shape"], rider["body"]
    post_args, post_specs, final_norm = [], [], False
    if post is not None:
        x, att, g_att, g_rwkv, gate, w_out_bf16, final_norm_w, final_norm = post
        d = x.shape[-1]
        npair = att.shape[1]
        out_w = d
        pair_spec = pl.BlockSpec((None, npair, rows, LANES), lambda i, c: (i, 0, c, 0))
        post_args = [x, att, g_att, g_rwkv, gate, w_out_bf16, final_norm_w.reshape(1, d)]
        post_specs = [pl.BlockSpec((None, rows, d), lambda i, c: (i, c, 0)), pair_spec, pair_spec,
                      pl.BlockSpec((None, rows, rw), lambda i, c: (i, c, 0)),
                      pl.BlockSpec((None, 1, d), lambda i, c: (i, 0, 0)),
                      vec(w_out_bf16), pl.BlockSpec((1, d), lambda i, c: (0, 0))]
    body = functools.partial(_rwkv_fused_kernel, rw=rw, n_in=n_in, rider_body=rider_body,
                             rider_in=len(rider_args), rider_out=len(rider_out_specs),
                             post_in=len(post_args), final_norm=final_norm)
    return pl.pallas_call(
        body,
        grid=(b, t // rows),
        in_specs=in_specs + rider_in_specs + post_specs,
        out_specs=[pl.BlockSpec((None, rows, out_w), lambda i, c: (i, c, 0)), state_spec] + rider_out_specs,
        out_shape=[jax.ShapeDtypeStruct((b, t, out_w), F32), jax.ShapeDtypeStruct((b, nh, hd, hd), F32)]
        + rider_out_shape,
        scratch_shapes=[pltpu.VMEM((nh, hd, LANES), F32), pltpu.VMEM((1, sw), F32)],
        compiler_params=_cparams("arbitrary", "arbitrary"),
        name="rwkv_prompt",
    )(*args, *rider_args, *post_args)


def _rwkv_feat_kernel(u_ref, prev_ref, mu_ref, w0_ref, wlb_ref, a0_ref, alb_ref, kk_ref, ka_ref,
                      r_ref, k_ref, v_ref, w_ref, kkr_ref, a_ref, *, rw):
    r, k2, v, logw, kk_raw, a = _rwkv_features(u_ref[...], prev_ref[...], mu_ref[...], w0_ref[...], wlb_ref[...],
                                               a0_ref[...], alb_ref[...], kk_ref[...], ka_ref[...], rw)
    r_ref[...] = r.T
    k_ref[...] = k2.T
    v_ref[...] = v.T
    w_ref[...] = jnp.exp(logw).T
    kkr_ref[...] = kk_raw.T
    a_ref[...] = a.T


def _rwkv_step_kernel(s_ref, r_ref, k_ref, v_ref, w_ref, kkr_ref, a_ref, rk_ref, lw_ref, lb_ref,
                      y_ref, sout_ref, y_scr):
    hd = s_ref.shape[0]
    kk_raw = kkr_ref[...]
    kk = kk_raw / jnp.maximum(jnp.sqrt(jnp.sum(kk_raw * kk_raw, axis=0, keepdims=True)), 1e-12)
    beta = kk * a_ref[...]
    w, k2, r, v = w_ref[...], k_ref[...], r_ref[...], v_ref[...]

    def value_row(i, carry):
        s = s_ref[i]
        sa = jnp.sum(s * kk, axis=0, keepdims=True)
        s_new = s * w - sa * beta + v_ref[pl.ds(i, 1), :] * k2
        sout_ref[i] = s_new
        y_scr[pl.ds(i, 1), :] = jnp.sum(s_new * r, axis=0, keepdims=True)
        return carry

    lax.fori_loop(0, hd, value_row, 0)
    y = y_scr[...]
    mean = jnp.mean(y, axis=0, keepdims=True)
    var = jnp.mean(jnp.square(y - mean), axis=0, keepdims=True)
    yn = (y - mean) * lax.rsqrt(var + GN_EPS) * lw_ref[...] + lb_ref[...]
    y_ref[...] = yn + jnp.sum(r * k2 * rk_ref[...], axis=0, keepdims=True) * v


def _rwkv_sample(u, prev, s0_t, p):
    b, sw = u.shape
    rw = p["w0"].shape[-1]
    nh = rw // HEAD_DIM
    hd = HEAD_DIM
    full = lambda a: pl.BlockSpec(a.shape, lambda i: (0,) * a.ndim)
    fparams = [p["mu"], p["w0"], p["wlb"], p["a0"], p["alb"], p["kk"], p["ka"]]
    feats = pl.pallas_call(
        functools.partial(_rwkv_feat_kernel, rw=rw),
        grid=(1,),
        in_specs=[full(u), full(prev)] + [full(a) for a in fparams],
        out_specs=[pl.BlockSpec((rw, b), lambda i: (0, 0))] * 6,
        out_shape=[jax.ShapeDtypeStruct((rw, b), F32)] * 6,
        compiler_params=_cparams("arbitrary"),
        name="rwkv_sample_features",
    )(u, prev, *fparams)
    chan = pl.BlockSpec((hd, b), lambda h: (h, 0))
    col = pl.BlockSpec((hd, 1), lambda h: (h, 0))
    state_spec = pl.BlockSpec((None, hd, hd, b), lambda h: (h, 0, 0, 0))
    cols = [p[n].reshape(rw, 1) for n in ("rk", "lw", "lb")]
    y_t, s_new = pl.pallas_call(
        _rwkv_step_kernel,
        grid=(nh,),
        in_specs=[state_spec] + [chan] * 6 + [col] * 3,
        out_specs=[chan, state_spec],
        out_shape=[jax.ShapeDtypeStruct((rw, b), F32), jax.ShapeDtypeStruct((nh, hd, hd, b), F32)],
        scratch_shapes=[pltpu.VMEM((hd, b), F32)],
        compiler_params=_cparams("arbitrary"),
        name="rwkv_sample_step",
    )(s0_t, *feats, *cols)
    return y_t.T, s_new


def _outproj_kernel(x_ref, att_ref, g_ref, rw_ref, gr_ref, gate_ref, w_ref, fnw_ref, o_ref, *, final_norm):
    o_ref[...] = _gated_outproj(x_ref, att_ref, g_ref, rw_ref[...], gr_ref, gate_ref, w_ref, fnw_ref, final_norm)


def _outproj(x, att, g_att, rwk, g_rwkv, gate, w_out_bf16, final_norm_w, final_norm):
    b, t, d = x.shape
    npair = att.shape[1]
    rw = rwk.shape[-1]
    tm = min(ROW_TILE, t)
    per_row = gate.shape[1] != 1
    gate_spec = (pl.BlockSpec((None, tm, d), lambda i, j: (i, j, 0)) if per_row
                 else pl.BlockSpec((None, 1, d), lambda i, j: (i, 0, 0)))
    pair_spec = pl.BlockSpec((None, npair, tm, LANES), lambda i, j: (i, 0, j, 0))
    return pl.pallas_call(
        functools.partial(_outproj_kernel, final_norm=final_norm),
        grid=(b, t // tm),
        in_specs=[pl.BlockSpec((None, tm, d), lambda i, j: (i, j, 0)), pair_spec, pair_spec,
                  pl.BlockSpec((None, tm, rw), lambda i, j: (i, j, 0)),
                  pl.BlockSpec((None, tm, rw), lambda i, j: (i, j, 0)),
                  gate_spec,
                  pl.BlockSpec(w_out_bf16.shape, lambda i, j: (0, 0)),
                  pl.BlockSpec((1, d), lambda i, j: (0, 0))],
        out_specs=pl.BlockSpec((None, tm, d), lambda i, j: (i, j, 0)),
        out_shape=jax.ShapeDtypeStruct((b, t, d), F32),
        compiler_params=_cparams("arbitrary", "arbitrary"),
        name="out_proj",
    )(x, att, g_att, rwk, g_rwkv, gate, w_out_bf16, final_norm_w.reshape(1, d))


def _pairs_to_heads(a):
    b, npair, t, _ = a.shape
    return a.transpose(0, 2, 1, 3).reshape(b, t, npair * (LANES // HEAD_DIM), HEAD_DIM)


def kernel(x_prompt, x_sample, cache_win_k, cache_win_v, state_wkv, state_shift, c_prompt, c_sample, rel_bias, norm_w, ada_w, ada_b, w_in, mu_shift, w0, w_lora_b, a0, a_lora_b, k_k, k_a, r_k, ln_x_w, ln_x_b, w_out, final_norm_w):
    depth = norm_w.shape[0]
    bp, tp, d = x_prompt.shape
    bs, ts, _ = x_sample.shape
    assert ts == 1, "the sample group decodes one token per sequence"
    rw = w0.shape[-1]
    att_w = w_out.shape[1] - rw
    shift_w = mu_shift.shape[-1]
    nh_att = att_w // HEAD_DIM
    nh_rw = rw // HEAD_DIM
    assert tp % ATT_SUPER == 0 and tp % (RWKV_CHUNK * RWKV_CHUNKS_PER_STEP) == 0 and bs % 8 == 0
    assert cache_win_k.shape[2] == MAX_WINDOW
    keep = min(MAX_WINDOW, tp)

    bias_tiles = _prompt_bias_tiles(rel_bias)
    c_all = jnp.concatenate([c_prompt, c_sample], axis=0)
    npad = -c_all.shape[0] % 8
    c_all = jnp.pad(c_all, ((0, npad), (0, 0)))

    xp = x_prompt
    xs = x_sample.reshape(1, bs, d)
    outs = [[] for _ in range(8)]
    for l in range(depth):
        mod = _ada_mod(c_all, ada_w[l], ada_b[l])
        shift, scale, gate = jnp.split(mod, 3, axis=-1)
        w_in_b = w_in[l].astype(BF16)
        w_out_b = w_out[l].astype(BF16)
        p = dict(mu=mu_shift[l].reshape(1, -1), w0=w0[l].reshape(1, -1), wlb=w_lora_b[l].astype(BF16),
                 a0=a0[l].reshape(1, -1), alb=a_lora_b[l].astype(BF16), kk=k_k[l].reshape(1, -1),
                 ka=k_a[l].reshape(1, -1), rk=r_k[l].reshape(1, -1), lw=ln_x_w[l].reshape(1, -1),
                 lb=ln_x_b[l].reshape(1, -1))
        last = l == depth - 1

        sm = lambda a: a[bp:bp + bs].reshape(1, bs, d)
        q_s, k_s, v_s, g_att_s, u_s, g_rwkv_s = _inproj(xs, sm(shift), sm(scale), norm_w[l], w_in_b, att_w, shift_w)
        nat = lambda a: a[0].transpose(1, 0, 2).reshape(bs, nh_att, HEAD_DIM)
        k_new, v_new = nat(k_s), nat(v_s)
        att_s_args = (nat(q_s), k_new, v_new, cache_win_k[l].transpose(0, 2, 3, 1),
                      cache_win_v[l].transpose(0, 2, 3, 1), rel_bias)

        pm = lambda a: a[:bp].reshape(bp, 1, d)
        q, k, v, g_att, u, g_rwkv = _inproj(xp, pm(shift), pm(scale), norm_w[l], w_in_b, att_w, shift_w)
        att = _att_prompt(q, k, v, bias_tiles)
        prev0 = jnp.zeros((bp, 1, shift_w), F32)
        s0 = jnp.zeros((bp, nh_rw, HEAD_DIM, HEAD_DIM), F32)
        steps = tp // (RWKV_CHUNK * RWKV_CHUNKS_PER_STEP)
        post = (xp, att, g_att, g_rwkv, pm(gate), w_out_b, final_norm_w, last)
        if bs == bp * steps:
            rider = _att_sample_job(*att_s_args, lambda i, c: i * steps + c)
            xp, s_p, att_s = _rwkv_prompt(u, prev0, s0, p, rider, post)
        else:
            xp, s_p = _rwkv_prompt(u, prev0, s0, p, None, post)
            att_s = _att_sample(*att_s_args)
        outs[0].append(_pairs_to_heads(k[:, :, tp - keep:, :]))
        outs[1].append(_pairs_to_heads(v[:, :, tp - keep:, :]))
        outs[4].append(s_p)
        outs[6].append(u[:, -1])

        att_s = att_s.reshape(bs, att_w // LANES, LANES).transpose(1, 0, 2)[None]
        y_rw_s, s_s = _rwkv_sample(u_s[0], state_shift[l], state_wkv[l].transpose(1, 2, 3, 0), p)
        xs = _outproj(xs, att_s, g_att_s, y_rw_s[None], g_rwkv_s, sm(gate), w_out_b, final_norm_w, last)
        outs[2].append(k_new.reshape(bs, 1, nh_att, HEAD_DIM))
        outs[3].append(v_new.reshape(bs, 1, nh_att, HEAD_DIM))
        outs[5].append(s_s.transpose(3, 0, 1, 2))
        outs[7].append(u_s[0])

    stack = lambda i: jnp.stack(outs[i])
    return (xp, xs.reshape(bs, 1, d), stack(0), stack(1), stack(2), stack(3), stack(4), stack(5), stack(6),
            stack(7))
```

```python
import functools
import math

import numpy as np
import jax
import jax.numpy as jnp
from jax import lax
from jax.experimental import pallas as pl
from jax.experimental.pallas import tpu as pltpu

F32 = jnp.float32
BF16 = jnp.bfloat16
HIGHEST = lax.Precision.HIGHEST

HEAD_DIM = 64
DILATED_BRANCHES = ((128, 1), (512, 4), (2048, 16))
MAX_WINDOW = max(w for w, _ in DILATED_BRANCHES)
KEYS_PER_BRANCH = 128
N_BUCKETS = 32
BUCKET_MAX_DIST = MAX_WINDOW
LORA_W = 64
NORM_EPS = 1e-6
GN_EPS = HEAD_DIM * 1e-5
LOG2E = math.log2(math.e)

LANES = 128
VMEM_LIMIT_BYTES = 56 * 1024 * 1024

ROW_TILE = 256
ATT_BLOCK = 128
ATT_SUPER = MAX_WINDOW
ATT_TILES_PER_ITER = 4
RWKV_CHUNK = 64
RWKV_CHUNKS_PER_STEP = 4


def _cparams(*sem):
    return pltpu.CompilerParams(dimension_semantics=sem, vmem_limit_bytes=VMEM_LIMIT_BYTES)


def _silu(x):
    return x * jax.nn.sigmoid(x)


def _nt(a, b):
    return lax.dot_general(a, b, (((1,), (1,)), ((), ())), preferred_element_type=F32)


def _tn(a, b):
    return lax.dot_general(a, b, (((0,), (0,)), ((), ())), preferred_element_type=F32)


def _nn(a, b, precision=None):
    return jnp.dot(a, b, precision=precision, preferred_element_type=F32)


def _ada_kernel(c_ref, w_ref, b_ref, o_ref):
    s = _silu(c_ref[...])
    o_ref[...] = _nn(s, w_ref[...], HIGHEST) + b_ref[...]


def _ada_mod(c_all, ada_w, ada_b):
    n, d = c_all.shape
    e = ada_w.shape[1]
    tn = 512
    return pl.pallas_call(
        _ada_kernel,
        grid=(e // tn,),
        in_specs=[pl.BlockSpec((n, d), lambda j: (0, 0)),
                  pl.BlockSpec((d, tn), lambda j: (0, j)),
                  pl.BlockSpec((1, tn), lambda j: (0, j))],
        out_specs=pl.BlockSpec((n, tn), lambda j: (0, j)),
        out_shape=jax.ShapeDtypeStruct((n, e), F32),
        compiler_params=_cparams("arbitrary"),
        name="ada_mod",
    )(c_all, ada_w, ada_b.reshape(1, e))


def _inproj_kernel(*refs, att_w, shift_w, rw, features):
    x_ref, shift_ref, scale_ref, nw_ref, w_ref = refs[:5]
    if features:
        prev0_ref, mu_ref, w0_ref, wlb_ref, a0_ref, alb_ref, kk_ref, ka_ref, rk_ref = refs[5:14]
        q_ref, k_ref, v_ref, g_ref, gr_ref = refs[14:19]
        feat_refs, ulast_ref, carry_scr = refs[19:26], refs[26], refs[27]
    else:
        q_ref, k_ref, v_ref, g_ref, gr_ref, u_ref = refs[5:11]
    if features:
        @pl.when(pl.program_id(1) == 0)
        def _():
            carry_scr[...] = prev0_ref[...]

    x = x_ref[...]
    xn = x * lax.rsqrt(jnp.mean(x * x, axis=-1, keepdims=True) + NORM_EPS) * nw_ref[...]
    h = (xn * (1.0 + scale_ref[...]) + shift_ref[...]).astype(BF16)
    u = _nn(h, w_ref[:, 4 * att_w:4 * att_w + shift_w])
    if features:
        rows = u.shape[0]
        shifted = pltpu.roll(u, 1, 0)
        row8 = lax.broadcasted_iota(jnp.int32, (8, 1), 0)
        u_prev = jnp.concatenate([jnp.where(row8 == 0, carry_scr[...], shifted[:8]), shifted[8:]], axis=0)
        carry_scr[...] = u[rows - 1:rows, :]
        ulast_ref[...] = u[rows - 1:rows, :]
        r, k2, v, logw, kk_raw, a = _rwkv_features(u, u_prev, mu_ref[...], w0_ref[...], wlb_ref[...], a0_ref[...],
                                                   alb_ref[...], kk_ref[...], ka_ref[...], rw)
        kk = kk_raw * lax.rsqrt(jnp.maximum(_head_sum(kk_raw * kk_raw), 1e-24))
        bonus = _head_sum(r * k2 * rk_ref[...]) * v
        for ref, val in zip(feat_refs, (r, k2, v, logw, kk, kk * a, bonus)):
            ref[...] = val
    else:
        u_ref[...] = u
    z = _nn(h, w_ref[:, :4 * att_w])
    npair = att_w // LANES
    for p in range(npair):
        q_ref[p] = z[:, p * LANES:(p + 1) * LANES] * (HEAD_DIM ** -0.5 * LOG2E)
        k_ref[p] = z[:, att_w + p * LANES:att_w + (p + 1) * LANES]
        v_ref[p] = z[:, 2 * att_w + p * LANES:2 * att_w + (p + 1) * LANES]
        g_ref[p] = z[:, 3 * att_w + p * LANES:3 * att_w + (p + 1) * LANES]
    gr_ref[...] = _nn(h, w_ref[:, 4 * att_w + shift_w:])


def _inproj(x, shift, scale, norm_w, w_in_bf16, att_w, shift_w, rwkv=None):
    b, t, d = x.shape
    in_w = w_in_bf16.shape[1]
    rw = in_w - 4 * att_w - shift_w
    npair = att_w // LANES
    tm = min(ROW_TILE, t)
    per_row = shift.shape[1] != 1
    mod_spec = (pl.BlockSpec((None, tm, d), lambda i, j: (i, j, 0)) if per_row
                else pl.BlockSpec((None, 1, d), lambda i, j: (i, 0, 0)))
    pair_spec = pl.BlockSpec((None, npair, tm, LANES), lambda i, j: (i, 0, j, 0))
    pair_shape = jax.ShapeDtypeStruct((b, npair, t, LANES), F32)
    row_spec = lambda w: pl.BlockSpec((None, tm, w), lambda i, j: (i, j, 0))
    row_shape = lambda w: jax.ShapeDtypeStruct((b, t, w), F32)
    args = [x, shift, scale, norm_w.reshape(1, d), w_in_bf16]
    in_specs = [row_spec(d), mod_spec, mod_spec, pl.BlockSpec((1, d), lambda i, j: (0, 0)),
                pl.BlockSpec((d, in_w), lambda i, j: (0, 0))]
    out_specs = [pair_spec] * 4 + [row_spec(rw)]
    out_shape = [pair_shape] * 4 + [row_shape(rw)]
    scratch = []
    if rwkv is None:
        out_specs.append(row_spec(shift_w))
        out_shape.append(row_shape(shift_w))
    else:
        p, prev0 = rwkv
        consts = [p[n] for n in ("mu", "w0", "wlb", "a0", "alb", "kk", "ka", "rk")]
        args += [prev0] + consts
        in_specs += [pl.BlockSpec((None, 1, shift_w), lambda i, j: (i, 0, 0))]
        in_specs += [pl.BlockSpec(c.shape, lambda i, j: (0, 0)) for c in consts]
        out_specs += [row_spec(rw)] * 7 + [pl.BlockSpec((None, 1, shift_w), lambda i, j: (i, 0, 0))]
        out_shape += [row_shape(rw)] * 7 + [jax.ShapeDtypeStruct((b, 1, shift_w), F32)]
        scratch = [pltpu.VMEM((1, shift_w), F32)]
    outs = pl.pallas_call(
        functools.partial(_inproj_kernel, att_w=att_w, shift_w=shift_w, rw=rw, features=rwkv is not None),
        grid=(b, t // tm),
        in_specs=in_specs,
        out_specs=out_specs,
        out_shape=out_shape,
        scratch_shapes=scratch,
        compiler_params=_cparams("arbitrary", "arbitrary"),
        name="in_proj",
    )(*args)
    if rwkv is None:
        return outs
    return list(outs[:5]) + [list(outs[5:12]), outs[12]]


def _t5_bucket_np(dist):
    max_exact = N_BUCKETS // 2
    nf = np.maximum(dist, max_exact).astype(np.float32)
    large = max_exact + (np.log(nf / np.float32(max_exact)) / np.float32(math.log(BUCKET_MAX_DIST / max_exact))
                         * np.float32(N_BUCKETS - max_exact)).astype(np.int32)
    large = np.minimum(large, N_BUCKETS - 1)
    return np.where(dist < max_exact, dist, large)


def _branch_bias(rel_bias):
    out = []
    for window, dil in DILATED_BRANCHES:
        dist = dil * np.arange(window // dil + 1, dtype=np.int32)
        out.append(rel_bias[_t5_bucket_np(dist)].T.astype(F32) * LOG2E)
    return jnp.stack(out)


def _prompt_bias_tiles(rel_bias):
    bias = _branch_bias(rel_bias)
    nb, nh, _ = bias.shape
    blk, width = ATT_BLOCK, 2 * ATT_BLOCK
    pad = jnp.full((nb, nh, blk - 1), -jnp.inf, F32)
    strip = jnp.flip(jnp.concatenate([pad, bias, pad], axis=-1), axis=-1)
    length = 3 * blk - 1
    rows = jnp.broadcast_to(jnp.pad(strip, ((0, 0), (0, 0), (0, 1)))[:, :, None, :], (nb, nh, blk, length + 1))
    skew = rows.reshape(nb, nh, blk * (length + 1))[:, :, :blk * length].reshape(nb, nh, blk, length)
    tile = skew[:, :, :, blk - 1:blk - 1 + width]
    first = jnp.where((np.arange(width) >= blk)[None, None, None, :], tile, -jnp.inf)
    return jnp.stack([tile, first], axis=1)


def _att_prompt_kernel(q_ref, k_ref, v_ref, bias_ref, o_ref, m_ref, l_ref, acc_ref, *, super_rows):
    sb = pl.program_id(2)
    lane = lax.broadcasted_iota(jnp.int32, (1, LANES), 1)
    head0 = lane < HEAD_DIM
    group = ATT_TILES_PER_ITER
    ones_b = jnp.ones((2 * ATT_BLOCK, LANES), BF16)

    for bi, (_, dil) in enumerate(DILATED_BRANCHES):
        nblk = super_rows // (ATT_BLOCK * dil)
        rows = lambda s, dil=dil: (pl.ds(s, ATT_BLOCK, stride=dil) if dil > 1 else pl.ds(s, ATT_BLOCK))

        def tiles(it, carry, bi=bi, dil=dil, nblk=nblk, rows=rows):
            locs, firsts, q2, kcat, vcat = [], [], [], [], []
            for j in range(group):
                idx = it * group + j
                res = idx // nblk
                loc = res + dil * ATT_BLOCK * (idx - res * nblk)
                glob = sb * super_rows + loc
                prev = glob - dil * ATT_BLOCK
                first = prev < 0
                pstart = jnp.where(first, glob, prev)
                qt = q_ref[rows(loc), :]
                locs.append(loc)
                firsts.append(first.astype(jnp.int32))
                q2.append(jnp.concatenate([jnp.where(head0, qt, 0.0), jnp.where(head0, 0.0, qt)],
                                          axis=0).astype(BF16))
                kcat.append(jnp.concatenate([k_ref[rows(pstart), :], k_ref[rows(glob), :]], axis=0).astype(BF16))
                vcat.append(jnp.concatenate([v_ref[rows(pstart), :], v_ref[rows(glob), :]], axis=0).astype(BF16))
            s = [_nt(q2[j], kcat[j]) + bias_ref[bi, firsts[j]].reshape(2 * ATT_BLOCK, 2 * ATT_BLOCK)
                 for j in range(group)]
            mt = [jnp.max(x, axis=-1, keepdims=True) for x in s]
            p = [jnp.exp2(s[j] - mt[j]).astype(BF16) for j in range(group)]
            pv = [_nn(p[j], jnp.concatenate([vcat[j], ones_b], axis=1)) for j in range(group)]
            for j in range(group):
                r = rows(locs[j])
                m_ref[bi, r, :] = jnp.where(head0, mt[j][:ATT_BLOCK], mt[j][ATT_BLOCK:])
                l_ref[bi, r, :] = jnp.where(head0, pv[j][:ATT_BLOCK, LANES:], pv[j][ATT_BLOCK:, LANES:])
                acc_ref[bi, r, :] = jnp.where(head0, pv[j][:ATT_BLOCK, :LANES], pv[j][ATT_BLOCK:, :LANES])
            return carry

        lax.fori_loop(0, dil * nblk // group, tiles, 0)

    nb = len(DILATED_BRANCHES)
    chunk = 2 * ATT_BLOCK

    def merge(c, carry):
        r = pl.ds(pl.multiple_of(c * chunk, chunk), chunk)
        ms = [m_ref[n, r, :] for n in range(nb)]
        m = functools.reduce(jnp.maximum, ms)
        ws = [jnp.exp2(x - m) for x in ms]
        den = functools.reduce(jnp.add, [l_ref[n, r, :] * ws[n] for n in range(nb)])
        num = functools.reduce(jnp.add, [acc_ref[n, r, :] * ws[n] for n in range(nb)])
        o_ref[r, :] = num / den
        return carry

    lax.fori_loop(0, super_rows // chunk, merge, 0)


def _att_prompt(q, k, v, bias_tiles):
    b, npair, t, _ = q.shape
    sr = ATT_SUPER
    nb = len(DILATED_BRANCHES)
    return pl.pallas_call(
        functools.partial(_att_prompt_kernel, super_rows=sr),
        grid=(b, npair, t // sr),
        in_specs=[pl.BlockSpec((None, None, sr, LANES), lambda i, p, s: (i, p, s, 0)),
                  pl.BlockSpec((None, None, t, LANES), lambda i, p, s: (i, p, 0, 0)),
                  pl.BlockSpec((None, None, t, LANES), lambda i, p, s: (i, p, 0, 0)),
                  pl.BlockSpec((nb, 2, 2, ATT_BLOCK, 2 * ATT_BLOCK), lambda i, p, s: (0, 0, p, 0, 0))],
        out_specs=pl.BlockSpec((None, None, sr, LANES), lambda i, p, s: (i, p, s, 0)),
        out_shape=jax.ShapeDtypeStruct((b, npair, t, LANES), F32),
        scratch_shapes=[pltpu.VMEM((nb, sr, LANES), F32)] * 3,
        compiler_params=_cparams("arbitrary", "arbitrary", "arbitrary"),
        name="att_prompt",
    )(q, k, v, bias_tiles)


def _att_sample_kernel(q_ref, kn_ref, vn_ref, kt_ref, vt_ref, bias_ref, b0_ref, o_ref):
    nh, hd, l = kt_ref.shape
    nb = bias_ref.shape[0]
    q = q_ref[...]
    qb = q.astype(BF16)
    head = lax.broadcasted_iota(jnp.int32, (nh, 1), 0)
    logit = jnp.zeros((nh, l), F32)
    for h in range(nh):
        s = _nn(qb, kt_ref[h].astype(BF16))
        logit = jnp.where(head == h, s, logit)
    s0 = jnp.sum(q * kn_ref[...], axis=-1, keepdims=True)
    ls = [logit + bias_ref[n] for n in range(nb)]
    s_self = [s0 + b0_ref[n] for n in range(nb)]
    m = functools.reduce(jnp.maximum, [jnp.max(x, axis=-1, keepdims=True) for x in ls] + s_self)
    p = functools.reduce(jnp.add, [jnp.exp2(x - m) for x in ls])
    p0 = functools.reduce(jnp.add, [jnp.exp2(x - m) for x in s_self])
    den = jnp.sum(p, axis=-1, keepdims=True) + p0
    pb = p.astype(BF16)
    out = p0 * vn_ref[...]
    for h in range(nh):
        out = out + jnp.where(head == h, _nt(pb, vt_ref[h].astype(BF16)), 0.0)
    o_ref[...] = out / den


def _att_sample_job(q, k_new, v_new, cache_kt, cache_vt, rel_bias, seq_of_step):
    b, nh, hd = q.shape
    l = cache_kt.shape[-1]
    bias = _branch_bias(rel_bias)
    back = l - np.arange(l)
    tabs = []
    for n, (window, dil) in enumerate(DILATED_BRANCHES):
        used = (back % dil == 0) & (back <= window)
        tabs.append(jnp.where(used[None, :], bias[n][:, np.where(used, back // dil, 0)], -jnp.inf))
    bias_pos = jnp.stack(tabs)
    bias0 = bias[:, :, 0][..., None]
    row = pl.BlockSpec((None, nh, hd), lambda *g: (seq_of_step(*g), 0, 0))
    cache = pl.BlockSpec((None, nh, hd, l), lambda *g: (seq_of_step(*g), 0, 0, 0))
    full = lambda a: pl.BlockSpec(a.shape, lambda *g: (0,) * a.ndim)
    return dict(body=_att_sample_kernel, args=[q, k_new, v_new, cache_kt, cache_vt, bias_pos, bias0],
                in_specs=[row, row, row, cache, cache, full(bias_pos), full(bias0)],
                out_specs=[row], out_shape=[jax.ShapeDtypeStruct((b, nh, hd), F32)])


def _att_sample(q, k_new, v_new, cache_kt, cache_vt, rel_bias):
    job = _att_sample_job(q, k_new, v_new, cache_kt, cache_vt, rel_bias, lambda i: i)
    return pl.pallas_call(
        job["body"],
        grid=(q.shape[0],),
        in_specs=job["in_specs"],
        out_specs=job["out_specs"][0],
        out_shape=job["out_shape"][0],
        compiler_params=_cparams("arbitrary"),
        name="att_sample",
    )(*job["args"])


def _gated_outproj(x_ref, att_ref, g_ref, rwk, gr_ref, gate_ref, w_ref, fnw_ref, final_norm):
    npair = att_ref.shape[0]
    mixed = [(att_ref[p] * _silu(g_ref[p])).astype(BF16) for p in range(npair)]
    mixed.append((rwk * _silu(gr_ref[...])).astype(BF16))
    acc = _nn(jnp.concatenate(mixed, axis=1), w_ref[...])
    xo = x_ref[...] + gate_ref[...] * acc
    if final_norm:
        xo = xo * lax.rsqrt(jnp.mean(xo * xo, axis=-1, keepdims=True) + NORM_EPS) * fnw_ref[...]
    return xo


def _head_sum(x):
    left = lax.broadcasted_iota(jnp.int32, (1, LANES), 1) < HEAD_DIM
    tiles = []
    for q in range(x.shape[1] // LANES):
        t = x[:, q * LANES:(q + 1) * LANES]
        s_left = jnp.sum(jnp.where(left, t, 0.0), axis=-1, keepdims=True)
        s_right = jnp.sum(jnp.where(left, 0.0, t), axis=-1, keepdims=True)
        tiles.append(jnp.where(left, s_left, s_right))
    return jnp.concatenate(tiles, axis=1)


def _rwkv_features(u, u_prev, mu, w0, wlb, a0, alb, kk_scale, ka, rw):
    um = u + (u_prev - u) * mu
    r = um[:, :rw]
    k = um[:, rw:2 * rw]
    v = um[:, 2 * rw:3 * rw]
    xw = um[:, 3 * rw:3 * rw + LORA_W]
    xa = um[:, 3 * rw + LORA_W:3 * rw + 2 * LORA_W]
    wl = w0 + _nn(jnp.tanh(xw).astype(BF16), wlb)
    logw = -math.exp(-0.5) * jax.nn.sigmoid(wl)
    a = jax.nn.sigmoid(a0 + _nn(xa.astype(BF16), alb))
    kk_raw = k * kk_scale
    k2 = k * (1.0 + (a - 1.0) * ka)
    return r, k2, v, logw, kk_raw, a


def _rwkv_prompt_kernel(r_ref, k2_ref, v_ref, logw_ref, kkn_ref, beta_ref, bonus_ref, s0_ref, lw_ref, lb_ref,
                        y_ref, sout_ref, s_scr, *, rw, rider=None, finish=None):
    c = pl.program_id(1)
    nc = pl.num_programs(1)
    rows = r_ref.shape[0]
    ch = RWKV_CHUNK
    ng = rows // ch
    nh = rw // HEAD_DIM
    hd = HEAD_DIM
    ntile = rw // LANES

    @pl.when(c == 0)
    def _():
        s_scr[...] = jnp.zeros(s_scr.shape, F32)
        for h in range(nh):
            s_scr[h, :, (h % 2) * hd:(h % 2 + 1) * hd] = s0_ref[h]

    if rider is not None:
        rider()

    r, k2, v, logw, kk, beta = r_ref[...], k2_ref[...], v_ref[...], logw_ref[...], kkn_ref[...], beta_ref[...]
    left = lax.broadcasted_iota(jnp.int32, (1, LANES), 1) < hd
    v_x = pltpu.roll(v, hd, 1)

    ti = lax.broadcasted_iota(jnp.int32, (ch, ch), 0)
    si = lax.broadcasted_iota(jnp.int32, (ch, ch), 1)
    tri_b = (ti >= si).astype(BF16)
    row_c = lax.broadcasted_iota(jnp.int32, (ch, 1), 0)
    t2 = lax.broadcasted_iota(jnp.int32, (2 * ch, 2 * ch), 0)
    s2 = lax.broadcasted_iota(jnp.int32, (2 * ch, 2 * ch), 1)
    tt = jnp.where(t2 >= ch, t2 - ch, t2)
    ss = jnp.where(s2 >= ch, s2 - ch, s2)
    score_mask = (tt - ss) >= jnp.where(t2 >= ch, 0, 1)
    tcol = lax.broadcasted_iota(jnp.int32, (ch, LANES), 1)
    trow = lax.broadcasted_iota(jnp.int32, (ch, LANES), 0)
    eye_right = (tcol == trow + ch).astype(F32)
    zeros_b = jnp.zeros((ch, LANES), BF16)
    own_of = [left if h % 2 == 0 else jnp.logical_not(left) for h in range(nh)]
    tile_of = [slice((h // 2) * LANES, (h // 2 + 1) * LANES) for h in range(nh)]
    xtile_of = [slice((((h + 1) // 2) % ntile) * LANES, (((h + 1) // 2) % ntile + 1) * LANES) for h in range(nh)]

    nu = ng * nh
    lhs, rhs, bk, vx, at_b, rt_own, pc = [], [], [], [], [], [], []
    for g in range(ng):
        rs = slice(g * ch, (g + 1) * ch)
        l1 = logw[rs].astype(BF16)
        rem = logw[rs] - l1.astype(F32)
        l2 = rem.astype(BF16)
        l3 = (rem - l2.astype(F32)).astype(BF16)
        cum3 = _nn(tri_b, jnp.concatenate([l1, l2, l3], axis=1))
        cum = cum3[:, :rw] + (cum3[:, rw:2 * rw] + cum3[:, 2 * rw:])
        ctot = cum[ch - 1:ch, :]
        pc_g = jnp.exp(ctot)
        e_cur = jnp.exp(cum)
        e_neg = jnp.exp(-cum)
        e_prev = jnp.where(row_c == 0, 1.0, pltpu.roll(e_cur, 1, 0))
        e_end = pc_g * e_neg
        rt_f = r[rs] * e_cur
        at_g = (-kk[rs] * e_prev).astype(BF16)
        rt_g = rt_f.astype(BF16)
        bt_g = (beta[rs] * e_neg).astype(BF16)
        kt_g = (k2[rs] * e_neg).astype(BF16)
        bh_g = (beta[rs] * e_end).astype(BF16)
        kh_g = (k2[rs] * e_end).astype(BF16)
        vx_g = v_x[rs].astype(BF16)
        for h in range(nh):
            own, tl = own_of[h], tile_of[h]
            zb = jnp.zeros((), BF16)
            lhs.append(jnp.concatenate([jnp.where(own, at_g[:, tl], zb), jnp.where(own, rt_g[:, tl], zb)], axis=0))
            rhs.append(jnp.concatenate([bt_g[:, tl], kt_g[:, tl]], axis=0))
            bk.append(jnp.concatenate([jnp.where(own, bh_g[:, tl], zb), jnp.where(own, kh_g[:, tl], zb)], axis=0))
            vx.append(jnp.where(own, zb, vx_g[:, xtile_of[h]]))
            at_b.append(at_g[:, tl])
            rt_own.append(jnp.where(own, rt_f[:, tl], 0.0))
            pc.append(pc_g[:, tl])

    units = range(nu)
    own_u = [own_of[i % nh] for i in units]
    sc = [jnp.where(score_mask, _nt(lhs[i], rhs[i]), 0.0) for i in units]
    top_b = [sc[i][:ch].astype(BF16) for i in units]
    bot_b = [sc[i][ch:].astype(BF16) for i in units]
    aakv = [_nn(top_b[i], jnp.concatenate([zeros_b, vx[i]], axis=0)) for i in units]
    x = [jnp.where(left, sc[i][:ch], eye_right) for i in units]
    npow = 1
    while npow < ch:
        xb = [x[i].astype(BF16) for i in units]
        x = [_nn(xb[i], jnp.concatenate([xb[i], zeros_b], axis=0)) + jnp.where(left, 0.0, x[i]) for i in units]
        npow *= 2
    z_b = [jnp.where(own_u[i], at_b[i], aakv[i].astype(BF16)) for i in units]
    wu = [_nn(x[i].astype(BF16), jnp.concatenate([zeros_b, z_b[i]], axis=0)) for i in units]
    rhs2 = [jnp.concatenate([wu[i].astype(BF16), vx[i]], axis=0) for i in units]
    qy = [_nn(bot_b[i], rhs2[i]) for i in units]
    pg = [_tn(rhs2[i], bk[i]) for i in units]

    s_cur = [s_scr[h] for h in range(nh)]
    y_sw = []
    for g in range(ng):
        y_u = []
        for h in range(nh):
            i = g * nh + h
            own = own_of[h]
            s_b = s_cur[h].astype(BF16)
            s_pad = jnp.concatenate([zeros_b, s_b] if h % 2 == 0 else [s_b, zeros_b], axis=0)
            qhat_b = jnp.where(own, rt_own[i] + qy[i], 0.0).astype(BF16)
            y_u.append(jnp.where(own, 0.0, qy[i] + _nt(qhat_b, s_pad)))
            g0 = pg[i][ch:] if h % 2 == 0 else pg[i][:ch]
            s_cur[h] = s_cur[h] * pc[i] + _nn(s_b, pg[i].astype(BF16)) + g0
        y_sw.append(jnp.concatenate([y_u[(2 * q - 1) % nh] + y_u[2 * q] for q in range(ntile)], axis=1))
    for h in range(nh):
        s_scr[h] = s_cur[h]
    y = pltpu.roll(jnp.concatenate(y_sw, axis=0), rw - hd, 1)
    mean = _head_sum(y) * (1.0 / hd)
    dev = y - mean
    var = _head_sum(dev * dev) * (1.0 / hd)
    y = dev * lax.rsqrt(var + GN_EPS) * lw_ref[...] + lb_ref[...] + bonus_ref[...]
    y_ref[...] = y if finish is None else finish(y)

    @pl.when(c == nc - 1)
    def _():
        for h in range(nh):
            sout_ref[h] = s_scr[h, :, (h % 2) * hd:(h % 2 + 1) * hd]


def _rwkv_fused_kernel(*refs, rw, n_in, rider_body, rider_in, rider_out, post_in, final_norm):
    own_in, rest = refs[:n_in], refs[n_in:]
    r_in, rest = rest[:rider_in], rest[rider_in:]
    p_in, rest = rest[:post_in], rest[post_in:]
    own_out, rest = rest[:2], rest[2:]
    r_out, scratch = rest[:rider_out], rest[rider_out:]
    rider = (lambda: rider_body(*r_in, *r_out)) if rider_body is not None else None
    finish = None
    if post_in:
        x_ref, att_ref, g_ref, gr_ref, gate_ref, w_ref, fnw_ref = p_in
        finish = lambda y: _gated_outproj(x_ref, att_ref, g_ref, y, gr_ref, gate_ref, w_ref, fnw_ref, final_norm)
    _rwkv_prompt_kernel(*own_in, *own_out, *scratch, rw=rw, rider=rider, finish=finish)


def _rwkv_prompt(feats, s0, p, rider=None, post=None):
    b, t, rw = feats[0].shape
    nh = rw // HEAD_DIM
    hd = HEAD_DIM
    rows = RWKV_CHUNK * RWKV_CHUNKS_PER_STEP
    vec = lambda a: pl.BlockSpec(a.shape, lambda i, c: (0,) * a.ndim)
    state_spec = pl.BlockSpec((None, nh, hd, hd), lambda i, c: (i, 0, 0, 0))
    args = list(feats) + [s0, p["lw"], p["lb"]]
    in_specs = ([pl.BlockSpec((None, rows, rw), lambda i, c: (i, c, 0))] * len(feats)
                + [state_spec, vec(p["lw"]), vec(p["lb"])])
    n_in = len(args)
    out_w = rw
    rider_args, rider_in_specs, rider_out_specs, rider_out_shape, rider_body = [], [], [], [], None
    if rider is not None:
        rider_args, rider_in_specs = rider["args"], rider["in_specs"]
        rider_out_specs, rider_out_shape, rider_body = rider["out_specs"], rider["out_shape"], rider["body"]
    post_args, post_specs, final_norm = [], [], False
    if post is not None:
        x, att, g_att, g_rwkv, gate, w_out_bf16, final_norm_w, final_norm = post
        d = x.shape[-1]
        npair = att.shape[1]
        out_w = d
        pair_spec = pl.BlockSpec((None, npair, rows, LANES), lambda i, c: (i, 0, c, 0))
        post_args = [x, att, g_att, g_rwkv, gate, w_out_bf16, final_norm_w.reshape(1, d)]
        post_specs = [pl.BlockSpec((None, rows, d), lambda i, c: (i, c, 0)), pair_spec, pair_spec,
                      pl.BlockSpec((None, rows, rw), lambda i, c: (i, c, 0)),
                      pl.BlockSpec((None, 1, d), lambda i, c: (i, 0, 0)),
                      vec(w_out_bf16), pl.BlockSpec((1, d), lambda i, c: (0, 0))]
    body = functools.partial(_rwkv_fused_kernel, rw=rw, n_in=n_in, rider_body=rider_body,
                             rider_in=len(rider_args), rider_out=len(rider_out_specs),
                             post_in=len(post_args), final_norm=final_norm)
    return pl.pallas_call(
        body,
        grid=(b, t // rows),
        in_specs=in_specs + rider_in_specs + post_specs,
        out_specs=[pl.BlockSpec((None, rows, out_w), lambda i, c: (i, c, 0)), state_spec] + rider_out_specs,
        out_shape=[jax.ShapeDtypeStruct((b, t, out_w), F32), jax.ShapeDtypeStruct((b, nh, hd, hd), F32)]
        + rider_out_shape,
        scratch_shapes=[pltpu.VMEM((nh, hd, LANES), F32)],
        compiler_params=_cparams("arbitrary", "arbitrary"),
        name="rwkv_prompt",
    )(*args, *rider_args, *post_args)


def _rwkv_feat_kernel(u_ref, prev_ref, mu_ref, w0_ref, wlb_ref, a0_ref, alb_ref, kk_ref, ka_ref,
                      r_ref, k_ref, v_ref, w_ref, kkr_ref, a_ref, *, rw):
    r, k2, v, logw, kk_raw, a = _rwkv_features(u_ref[...], prev_ref[...], mu_ref[...], w0_ref[...], wlb_ref[...],
                                               a0_ref[...], alb_ref[...], kk_ref[...], ka_ref[...], rw)
    r_ref[...] = r.T
    k_ref[...] = k2.T
    v_ref[...] = v.T
    w_ref[...] = jnp.exp(logw).T
    kkr_ref[...] = kk_raw.T
    a_ref[...] = a.T


def _rwkv_step_kernel(s_ref, r_ref, k_ref, v_ref, w_ref, kkr_ref, a_ref, rk_ref, lw_ref, lb_ref,
                      y_ref, sout_ref, y_scr):
    hd = s_ref.shape[0]
    kk_raw = kkr_ref[...]
    kk = kk_raw / jnp.maximum(jnp.sqrt(jnp.sum(kk_raw * kk_raw, axis=0, keepdims=True)), 1e-12)
    beta = kk * a_ref[...]
    w, k2, r, v = w_ref[...], k_ref[...], r_ref[...], v_ref[...]

    def value_row(i, carry):
        s = s_ref[i]
        sa = jnp.sum(s * kk, axis=0, keepdims=True)
        s_new = s * w - sa * beta + v_ref[pl.ds(i, 1), :] * k2
        sout_ref[i] = s_new
        y_scr[pl.ds(i, 1), :] = jnp.sum(s_new * r, axis=0, keepdims=True)
        return carry

    lax.fori_loop(0, hd, value_row, 0)
    y = y_scr[...]
    mean = jnp.mean(y, axis=0, keepdims=True)
    var = jnp.mean(jnp.square(y - mean), axis=0, keepdims=True)
    yn = (y - mean) * lax.rsqrt(var + GN_EPS) * lw_ref[...] + lb_ref[...]
    y_ref[...] = yn + jnp.sum(r * k2 * rk_ref[...], axis=0, keepdims=True) * v


def _rwkv_sample(u, prev, s0_t, p):
    b, sw = u.shape
    rw = p["w0"].shape[-1]
    nh = rw // HEAD_DIM
    hd = HEAD_DIM
    full = lambda a: pl.BlockSpec(a.shape, lambda i: (0,) * a.ndim)
    fparams = [p["mu"], p["w0"], p["wlb"], p["a0"], p["alb"], p["kk"], p["ka"]]
    feats = pl.pallas_call(
        functools.partial(_rwkv_feat_kernel, rw=rw),
        grid=(1,),
        in_specs=[full(u), full(prev)] + [full(a) for a in fparams],
        out_specs=[pl.BlockSpec((rw, b), lambda i: (0, 0))] * 6,
        out_shape=[jax.ShapeDtypeStruct((rw, b), F32)] * 6,
        compiler_params=_cparams("arbitrary"),
        name="rwkv_sample_features",
    )(u, prev, *fparams)
    chan = pl.BlockSpec((hd, b), lambda h: (h, 0))
    col = pl.BlockSpec((hd, 1), lambda h: (h, 0))
    state_spec = pl.BlockSpec((None, hd, hd, b), lambda h: (h, 0, 0, 0))
    cols = [p[n].reshape(rw, 1) for n in ("rk", "lw", "lb")]
    y_t, s_new = pl.pallas_call(
        _rwkv_step_kernel,
        grid=(nh,),
        in_specs=[state_spec] + [chan] * 6 + [col] * 3,
        out_specs=[chan, state_spec],
        out_shape=[jax.ShapeDtypeStruct((rw, b), F32), jax.ShapeDtypeStruct((nh, hd, hd, b), F32)],
        scratch_shapes=[pltpu.VMEM((hd, b), F32)],
        compiler_params=_cparams("arbitrary"),
        name="rwkv_sample_step",
    )(s0_t, *feats, *cols)
    return y_t.T, s_new


def _outproj_kernel(x_ref, att_ref, g_ref, rw_ref, gr_ref, gate_ref, w_ref, fnw_ref, o_ref, *, final_norm):
    o_ref[...] = _gated_outproj(x_ref, att_ref, g_ref, rw_ref[...], gr_ref, gate_ref, w_ref, fnw_ref, final_norm)


def _outproj(x, att, g_att, rwk, g_rwkv, gate, w_out_bf16, final_norm_w, final_norm):
    b, t, d = x.shape
    npair = att.shape[1]
    rw = rwk.shape[-1]
    tm = min(ROW_TILE, t)
    per_row = gate.shape[1] != 1
    gate_spec = (pl.BlockSpec((None, tm, d), lambda i, j: (i, j, 0)) if per_row
                 else pl.BlockSpec((None, 1, d), lambda i, j: (i, 0, 0)))
    pair_spec = pl.BlockSpec((None, npair, tm, LANES), lambda i, j: (i, 0, j, 0))
    return pl.pallas_call(
        functools.partial(_outproj_kernel, final_norm=final_norm),
        grid=(b, t // tm),
        in_specs=[pl.BlockSpec((None, tm, d), lambda i, j: (i, j, 0)), pair_spec, pair_spec,
                  pl.BlockSpec((None, tm, rw), lambda i, j: (i, j, 0)),
                  pl.BlockSpec((None, tm, rw), lambda i, j: (i, j, 0)),
                  gate_spec,
                  pl.BlockSpec(w_out_bf16.shape, lambda i, j: (0, 0)),
                  pl.BlockSpec((1, d), lambda i, j: (0, 0))],
        out_specs=pl.BlockSpec((None, tm, d), lambda i, j: (i, j, 0)),
        out_shape=jax.ShapeDtypeStruct((b, t, d), F32),
        compiler_params=_cparams("arbitrary", "arbitrary"),
        name="out_proj",
    )(x, att, g_att, rwk, g_rwkv, gate, w_out_bf16, final_norm_w.reshape(1, d))


def _pairs_to_heads(a):
    b, npair, t, _ = a.shape
    return a.transpose(0, 2, 1, 3).reshape(b, t, npair * (LANES // HEAD_DIM), HEAD_DIM)


def kernel(x_prompt, x_sample, cache_win_k, cache_win_v, state_wkv, state_shift, c_prompt, c_sample, rel_bias, norm_w, ada_w, ada_b, w_in, mu_shift, w0, w_lora_b, a0, a_lora_b, k_k, k_a, r_k, ln_x_w, ln_x_b, w_out, final_norm_w):
    depth = norm_w.shape[0]
    bp, tp, d = x_prompt.shape
    bs, ts, _ = x_sample.shape
    assert ts == 1, "the sample group decodes one token per sequence"
    rw = w0.shape[-1]
    att_w = w_out.shape[1] - rw
    shift_w = mu_shift.shape[-1]
    nh_att = att_w // HEAD_DIM
    nh_rw = rw // HEAD_DIM
    assert tp % ATT_SUPER == 0 and tp % (RWKV_CHUNK * RWKV_CHUNKS_PER_STEP) == 0 and bs % 8 == 0
    assert cache_win_k.shape[2] == MAX_WINDOW
    keep = min(MAX_WINDOW, tp)

    bias_tiles = _prompt_bias_tiles(rel_bias)
    c_all = jnp.concatenate([c_prompt, c_sample], axis=0)
    npad = -c_all.shape[0] % 8
    c_all = jnp.pad(c_all, ((0, npad), (0, 0)))

    xp = x_prompt
    xs = x_sample.reshape(1, bs, d)
    outs = [[] for _ in range(8)]
    for l in range(depth):
        mod = _ada_mod(c_all, ada_w[l], ada_b[l])
        shift, scale, gate = jnp.split(mod, 3, axis=-1)
        w_in_b = w_in[l].astype(BF16)
        w_out_b = w_out[l].astype(BF16)
        p = dict(mu=mu_shift[l].reshape(1, -1), w0=w0[l].reshape(1, -1), wlb=w_lora_b[l].astype(BF16),
                 a0=a0[l].reshape(1, -1), alb=a_lora_b[l].astype(BF16), kk=k_k[l].reshape(1, -1),
                 ka=k_a[l].reshape(1, -1), rk=r_k[l].reshape(1, -1), lw=ln_x_w[l].reshape(1, -1),
                 lb=ln_x_b[l].reshape(1, -1))
        last = l == depth - 1

        sm = lambda a: a[bp:bp + bs].reshape(1, bs, d)
        q_s, k_s, v_s, g_att_s, g_rwkv_s, u_s = _inproj(xs, sm(shift), sm(scale), norm_w[l], w_in_b, att_w, shift_w)
        nat = lambda a: a[0].transpose(1, 0, 2).reshape(bs, nh_att, HEAD_DIM)
        k_new, v_new = nat(k_s), nat(v_s)
        att_s_args = (nat(q_s), k_new, v_new, cache_win_k[l].transpose(0, 2, 3, 1),
                      cache_win_v[l].transpose(0, 2, 3, 1), rel_bias)

        pm = lambda a: a[:bp].reshape(bp, 1, d)
        prev0 = jnp.zeros((bp, 1, shift_w), F32)
        q, k, v, g_att, g_rwkv, feats, u_last = _inproj(xp, pm(shift), pm(scale), norm_w[l], w_in_b, att_w, shift_w,
                                                        rwkv=(p, prev0))
        att = _att_prompt(q, k, v, bias_tiles)
        s0 = jnp.zeros((bp, nh_rw, HEAD_DIM, HEAD_DIM), F32)
        steps = tp // (RWKV_CHUNK * RWKV_CHUNKS_PER_STEP)
        post = (xp, att, g_att, g_rwkv, pm(gate), w_out_b, final_norm_w, last)
        if bs == bp * steps:
            rider = _att_sample_job(*att_s_args, lambda i, c: i * steps + c)
            xp, s_p, att_s = _rwkv_prompt(feats, s0, p, rider, post)
        else:
            xp, s_p = _rwkv_prompt(feats, s0, p, None, post)
            att_s = _att_sample(*att_s_args)
        outs[0].append(_pairs_to_heads(k[:, :, tp - keep:, :]))
        outs[1].append(_pairs_to_heads(v[:, :, tp - keep:, :]))
        outs[4].append(s_p)
        outs[6].append(u_last[:, 0])

        att_s = att_s.reshape(bs, att_w // LANES, LANES).transpose(1, 0, 2)[None]
        y_rw_s, s_s = _rwkv_sample(u_s[0], state_shift[l], state_wkv[l].transpose(1, 2, 3, 0), p)
        xs = _outproj(xs, att_s, g_att_s, y_rw_s[None], g_rwkv_s, sm(gate), w_out_b, final_norm_w, last)
        outs[2].append(k_new.reshape(bs, 1, nh_att, HEAD_DIM))
        outs[3].append(v_new.reshape(bs, 1, nh_att, HEAD_DIM))
        outs[5].append(s_s.transpose(3, 0, 1, 2))
        outs[7].append(u_s[0])

    stack = lambda i: jnp.stack(outs[i])
    return (xp, xs.reshape(bs, 1, d), stack(0), stack(1), stack(2), stack(3), stack(4), stack(5), stack(6),
            stack(7))
```

```python
import functools
import math

import numpy as np
import jax
import jax.numpy as jnp
from jax import lax
from jax.experimental import pallas as pl
from jax.experimental.pallas import tpu as pltpu

F32 = jnp.float32
BF16 = jnp.bfloat16
HIGHEST = lax.Precision.HIGHEST

HEAD_DIM = 64
DILATED_BRANCHES = ((128, 1), (512, 4), (2048, 16))
MAX_WINDOW = max(w for w, _ in DILATED_BRANCHES)
KEYS_PER_BRANCH = 128
N_BUCKETS = 32
BUCKET_MAX_DIST = MAX_WINDOW
LORA_W = 64
NORM_EPS = 1e-6
GN_EPS = HEAD_DIM * 1e-5
LOG2E = math.log2(math.e)

LANES = 128
VMEM_LIMIT_BYTES = 56 * 1024 * 1024

ROW_TILE = 256
ATT_BLOCK = 128
ATT_SUPER = MAX_WINDOW
ATT_TILES_PER_ITER = {1: 8, 4: 8, 16: 4}
RWKV_CHUNK = 64
RWKV_CHUNKS_PER_STEP = 4


def _cparams(*sem):
    return pltpu.CompilerParams(dimension_semantics=sem, vmem_limit_bytes=VMEM_LIMIT_BYTES)


def _silu(x):
    return x * jax.nn.sigmoid(x)


def _nt(a, b):
    return lax.dot_general(a, b, (((1,), (1,)), ((), ())), preferred_element_type=F32)


def _tn(a, b):
    return lax.dot_general(a, b, (((0,), (0,)), ((), ())), preferred_element_type=F32)


def _nn(a, b, precision=None):
    return jnp.dot(a, b, precision=precision, preferred_element_type=F32)


def _ada_kernel(c_ref, w_ref, b_ref, o_ref):
    s = _silu(c_ref[...])
    o_ref[...] = _nn(s, w_ref[...], HIGHEST) + b_ref[...]


def _ada_mod(c_all, ada_w, ada_b):
    n, d = c_all.shape
    e = ada_w.shape[1]
    tn = 512
    return pl.pallas_call(
        _ada_kernel,
        grid=(e // tn,),
        in_specs=[pl.BlockSpec((n, d), lambda j: (0, 0)),
                  pl.BlockSpec((d, tn), lambda j: (0, j)),
                  pl.BlockSpec((1, tn), lambda j: (0, j))],
        out_specs=pl.BlockSpec((n, tn), lambda j: (0, j)),
        out_shape=jax.ShapeDtypeStruct((n, e), F32),
        compiler_params=_cparams("arbitrary"),
        name="ada_mod",
    )(c_all, ada_w, ada_b.reshape(1, e))


def _inproj_kernel(*refs, att_w, shift_w, rw, features):
    x_ref, shift_ref, scale_ref, nw_ref, w_ref = refs[:5]
    if features:
        prev0_ref, mu_ref, w0_ref, wlb_ref, a0_ref, alb_ref, kk_ref, ka_ref, rk_ref = refs[5:14]
        q_ref, k_ref, v_ref, g_ref, gr_ref = refs[14:19]
        feat_refs, ulast_ref, carry_scr = refs[19:26], refs[26], refs[27]
    else:
        q_ref, k_ref, v_ref, g_ref, gr_ref, u_ref = refs[5:11]
    if features:
        @pl.when(pl.program_id(1) == 0)
        def _():
            carry_scr[...] = prev0_ref[...]

    x = x_ref[...]
    xn = x * lax.rsqrt(jnp.mean(x * x, axis=-1, keepdims=True) + NORM_EPS) * nw_ref[...]
    h = (xn * (1.0 + scale_ref[...]) + shift_ref[...]).astype(BF16)
    u = _nn(h, w_ref[:, 4 * att_w:4 * att_w + shift_w])
    if features:
        rows = u.shape[0]
        shifted = pltpu.roll(u, 1, 0)
        row8 = lax.broadcasted_iota(jnp.int32, (8, 1), 0)
        u_prev = jnp.concatenate([jnp.where(row8 == 0, carry_scr[...], shifted[:8]), shifted[8:]], axis=0)
        carry_scr[...] = u[rows - 1:rows, :]
        ulast_ref[...] = u[rows - 1:rows, :]
        r, k2, v, logw, kk_raw, a = _rwkv_features(u, u_prev, mu_ref[...], w0_ref[...], wlb_ref[...], a0_ref[...],
                                                   alb_ref[...], kk_ref[...], ka_ref[...], rw)
        kk = kk_raw * lax.rsqrt(jnp.maximum(_head_sum(kk_raw * kk_raw), 1e-24))
        bonus = _head_sum(r * k2 * rk_ref[...]) * v
        for ref, val in zip(feat_refs, (r, k2, v, logw, kk, kk * a, bonus)):
            ref[...] = val
    else:
        u_ref[...] = u
    z = _nn(h, w_ref[:, :4 * att_w])
    npair = att_w // LANES
    for p in range(npair):
        q_ref[p] = z[:, p * LANES:(p + 1) * LANES] * (HEAD_DIM ** -0.5 * LOG2E)
        k_ref[p] = z[:, att_w + p * LANES:att_w + (p + 1) * LANES]
        v_ref[p] = z[:, 2 * att_w + p * LANES:2 * att_w + (p + 1) * LANES]
        g_ref[p] = z[:, 3 * att_w + p * LANES:3 * att_w + (p + 1) * LANES]
    gr_ref[...] = _nn(h, w_ref[:, 4 * att_w + shift_w:])


def _inproj(x, shift, scale, norm_w, w_in_bf16, att_w, shift_w, rwkv=None):
    b, t, d = x.shape
    in_w = w_in_bf16.shape[1]
    rw = in_w - 4 * att_w - shift_w
    npair = att_w // LANES
    tm = min(ROW_TILE, t)
    per_row = shift.shape[1] != 1
    mod_spec = (pl.BlockSpec((None, tm, d), lambda i, j: (i, j, 0)) if per_row
                else pl.BlockSpec((None, 1, d), lambda i, j: (i, 0, 0)))
    pair_spec = pl.BlockSpec((None, npair, tm, LANES), lambda i, j: (i, 0, j, 0))
    pair_shape = jax.ShapeDtypeStruct((b, npair, t, LANES), F32)
    row_spec = lambda w: pl.BlockSpec((None, tm, w), lambda i, j: (i, j, 0))
    row_shape = lambda w: jax.ShapeDtypeStruct((b, t, w), F32)
    args = [x, shift, scale, norm_w.reshape(1, d), w_in_bf16]
    in_specs = [row_spec(d), mod_spec, mod_spec, pl.BlockSpec((1, d), lambda i, j: (0, 0)),
                pl.BlockSpec((d, in_w), lambda i, j: (0, 0))]
    out_specs = [pair_spec] * 4 + [row_spec(rw)]
    out_shape = [pair_shape] * 4 + [row_shape(rw)]
    scratch = []
    if rwkv is None:
        out_specs.append(row_spec(shift_w))
        out_shape.append(row_shape(shift_w))
    else:
        p, prev0 = rwkv
        consts = [p[n] for n in ("mu", "w0", "wlb", "a0", "alb", "kk", "ka", "rk")]
        args += [prev0] + consts
        in_specs += [pl.BlockSpec((None, 1, shift_w), lambda i, j: (i, 0, 0))]
        in_specs += [pl.BlockSpec(c.shape, lambda i, j: (0, 0)) for c in consts]
        out_specs += [row_spec(rw)] * 7 + [pl.BlockSpec((None, 1, shift_w), lambda i, j: (i, 0, 0))]
        out_shape += [row_shape(rw)] * 7 + [jax.ShapeDtypeStruct((b, 1, shift_w), F32)]
        scratch = [pltpu.VMEM((1, shift_w), F32)]
    outs = pl.pallas_call(
        functools.partial(_inproj_kernel, att_w=att_w, shift_w=shift_w, rw=rw, features=rwkv is not None),
        grid=(b, t // tm),
        in_specs=in_specs,
        out_specs=out_specs,
        out_shape=out_shape,
        scratch_shapes=scratch,
        compiler_params=_cparams("arbitrary", "arbitrary"),
        name="in_proj",
    )(*args)
    if rwkv is None:
        return outs
    return list(outs[:5]) + [list(outs[5:12]), outs[12]]


def _t5_bucket_np(dist):
    max_exact = N_BUCKETS // 2
    nf = np.maximum(dist, max_exact).astype(np.float32)
    large = max_exact + (np.log(nf / np.float32(max_exact)) / np.float32(math.log(BUCKET_MAX_DIST / max_exact))
                         * np.float32(N_BUCKETS - max_exact)).astype(np.int32)
    large = np.minimum(large, N_BUCKETS - 1)
    return np.where(dist < max_exact, dist, large)


def _branch_bias(rel_bias):
    out = []
    for window, dil in DILATED_BRANCHES:
        dist = dil * np.arange(window // dil + 1, dtype=np.int32)
        out.append(rel_bias[_t5_bucket_np(dist)].T.astype(F32) * LOG2E)
    return jnp.stack(out)


def _prompt_bias_tiles(rel_bias):
    bias = _branch_bias(rel_bias)
    nb, nh, _ = bias.shape
    blk, width = ATT_BLOCK, 2 * ATT_BLOCK
    pad = jnp.full((nb, nh, blk - 1), -jnp.inf, F32)
    strip = jnp.flip(jnp.concatenate([pad, bias, pad], axis=-1), axis=-1)
    length = 3 * blk - 1
    rows = jnp.broadcast_to(jnp.pad(strip, ((0, 0), (0, 0), (0, 1)))[:, :, None, :], (nb, nh, blk, length + 1))
    skew = rows.reshape(nb, nh, blk * (length + 1))[:, :, :blk * length].reshape(nb, nh, blk, length)
    tile = skew[:, :, :, blk - 1:blk - 1 + width]
    first = jnp.where((np.arange(width) >= blk)[None, None, None, :], tile, -jnp.inf)
    return jnp.stack([tile, first], axis=1)


def _att_prompt_kernel(q_ref, k_ref, v_ref, bias_ref, o_ref, m_ref, l_ref, acc_ref, *, super_rows):
    sb = pl.program_id(2)
    lane = lax.broadcasted_iota(jnp.int32, (1, LANES), 1)
    head0 = lane < HEAD_DIM
    ones_b = jnp.ones((2 * ATT_BLOCK, LANES), BF16)

    for bi, (_, dil) in enumerate(DILATED_BRANCHES):
        nblk = super_rows // (ATT_BLOCK * dil)
        group = ATT_TILES_PER_ITER[dil]
        rows = lambda s, dil=dil: (pl.ds(s, ATT_BLOCK, stride=dil) if dil > 1 else pl.ds(s, ATT_BLOCK))

        def tiles(it, carry, bi=bi, dil=dil, nblk=nblk, rows=rows, group=group):
            locs, firsts, q2, kcat, vcat = [], [], [], [], []
            for j in range(group):
                idx = it * group + j
                res = idx // nblk
                loc = res + dil * ATT_BLOCK * (idx - res * nblk)
                glob = sb * super_rows + loc
                prev = glob - dil * ATT_BLOCK
                first = prev < 0
                pstart = jnp.where(first, glob, prev)
                qt = q_ref[rows(loc), :]
                locs.append(loc)
                firsts.append(first.astype(jnp.int32))
                q2.append(jnp.concatenate([jnp.where(head0, qt, 0.0), jnp.where(head0, 0.0, qt)],
                                          axis=0).astype(BF16))
                kcat.append(jnp.concatenate([k_ref[rows(pstart), :], k_ref[rows(glob), :]], axis=0).astype(BF16))
                vcat.append(jnp.concatenate([v_ref[rows(pstart), :], v_ref[rows(glob), :]], axis=0).astype(BF16))
            s = [_nt(q2[j], kcat[j]) + bias_ref[bi, firsts[j]].reshape(2 * ATT_BLOCK, 2 * ATT_BLOCK)
                 for j in range(group)]
            mt = [jnp.max(x, axis=-1, keepdims=True) for x in s]
            p = [jnp.exp2(s[j] - mt[j]).astype(BF16) for j in range(group)]
            pv = [_nn(p[j], jnp.concatenate([vcat[j], ones_b], axis=1)) for j in range(group)]
            for j in range(group):
                r = rows(locs[j])
                m_ref[bi, r, :] = jnp.where(head0, mt[j][:ATT_BLOCK], mt[j][ATT_BLOCK:])
                l_ref[bi, r, :] = jnp.where(head0, pv[j][:ATT_BLOCK, LANES:], pv[j][ATT_BLOCK:, LANES:])
                acc_ref[bi, r, :] = jnp.where(head0, pv[j][:ATT_BLOCK, :LANES], pv[j][ATT_BLOCK:, :LANES])
            return carry

        lax.fori_loop(0, dil * nblk // group, tiles, 0)

    nb = len(DILATED_BRANCHES)
    chunk = 2 * ATT_BLOCK

    def merge(c, carry):
        r = pl.ds(pl.multiple_of(c * chunk, chunk), chunk)
        ms = [m_ref[n, r, :] for n in range(nb)]
        m = functools.reduce(jnp.maximum, ms)
        ws = [jnp.exp2(x - m) for x in ms]
        den = functools.reduce(jnp.add, [l_ref[n, r, :] * ws[n] for n in range(nb)])
        num = functools.reduce(jnp.add, [acc_ref[n, r, :] * ws[n] for n in range(nb)])
        o_ref[r, :] = num / den
        return carry

    lax.fori_loop(0, super_rows // chunk, merge, 0)


def _att_prompt(q, k, v, bias_tiles):
    b, npair, t, _ = q.shape
    sr = ATT_SUPER
    nb = len(DILATED_BRANCHES)
    return pl.pallas_call(
        functools.partial(_att_prompt_kernel, super_rows=sr),
        grid=(b, npair, t // sr),
        in_specs=[pl.BlockSpec((None, None, sr, LANES), lambda i, p, s: (i, p, s, 0)),
                  pl.BlockSpec((None, None, t, LANES), lambda i, p, s: (i, p, 0, 0)),
                  pl.BlockSpec((None, None, t, LANES), lambda i, p, s: (i, p, 0, 0)),
                  pl.BlockSpec((nb, 2, 2, ATT_BLOCK, 2 * ATT_BLOCK), lambda i, p, s: (0, 0, p, 0, 0))],
        out_specs=pl.BlockSpec((None, None, sr, LANES), lambda i, p, s: (i, p, s, 0)),
        out_shape=jax.ShapeDtypeStruct((b, npair, t, LANES), F32),
        scratch_shapes=[pltpu.VMEM((nb, sr, LANES), F32)] * 3,
        compiler_params=_cparams("arbitrary", "arbitrary", "arbitrary"),
        name="att_prompt",
    )(q, k, v, bias_tiles)


def _att_sample_kernel(q_ref, kn_ref, vn_ref, kt_ref, vt_ref, bias_ref, b0_ref, o_ref):
    nh, hd, l = kt_ref.shape
    nb = bias_ref.shape[0]
    q = q_ref[...]
    qb = q.astype(BF16)
    head = lax.broadcasted_iota(jnp.int32, (nh, 1), 0)
    logit = jnp.zeros((nh, l), F32)
    for h in range(nh):
        s = _nn(qb, kt_ref[h].astype(BF16))
        logit = jnp.where(head == h, s, logit)
    s0 = jnp.sum(q * kn_ref[...], axis=-1, keepdims=True)
    ls = [logit + bias_ref[n] for n in range(nb)]
    s_self = [s0 + b0_ref[n] for n in range(nb)]
    m = functools.reduce(jnp.maximum, [jnp.max(x, axis=-1, keepdims=True) for x in ls] + s_self)
    p = functools.reduce(jnp.add, [jnp.exp2(x - m) for x in ls])
    p0 = functools.reduce(jnp.add, [jnp.exp2(x - m) for x in s_self])
    den = jnp.sum(p, axis=-1, keepdims=True) + p0
    pb = p.astype(BF16)
    out = p0 * vn_ref[...]
    for h in range(nh):
        out = out + jnp.where(head == h, _nt(pb, vt_ref[h].astype(BF16)), 0.0)
    o_ref[...] = out / den


def _att_sample_job(q, k_new, v_new, cache_kt, cache_vt, rel_bias, seq_of_step):
    b, nh, hd = q.shape
    l = cache_kt.shape[-1]
    bias = _branch_bias(rel_bias)
    back = l - np.arange(l)
    tabs = []
    for n, (window, dil) in enumerate(DILATED_BRANCHES):
        used = (back % dil == 0) & (back <= window)
        tabs.append(jnp.where(used[None, :], bias[n][:, np.where(used, back // dil, 0)], -jnp.inf))
    bias_pos = jnp.stack(tabs)
    bias0 = bias[:, :, 0][..., None]
    row = pl.BlockSpec((None, nh, hd), lambda *g: (seq_of_step(*g), 0, 0))
    cache = pl.BlockSpec((None, nh, hd, l), lambda *g: (seq_of_step(*g), 0, 0, 0))
    full = lambda a: pl.BlockSpec(a.shape, lambda *g: (0,) * a.ndim)
    return dict(body=_att_sample_kernel, args=[q, k_new, v_new, cache_kt, cache_vt, bias_pos, bias0],
                in_specs=[row, row, row, cache, cache, full(bias_pos), full(bias0)],
                out_specs=[row], out_shape=[jax.ShapeDtypeStruct((b, nh, hd), F32)])


def _att_sample(q, k_new, v_new, cache_kt, cache_vt, rel_bias):
    job = _att_sample_job(q, k_new, v_new, cache_kt, cache_vt, rel_bias, lambda i: i)
    return pl.pallas_call(
        job["body"],
        grid=(q.shape[0],),
        in_specs=job["in_specs"],
        out_specs=job["out_specs"][0],
        out_shape=job["out_shape"][0],
        compiler_params=_cparams("arbitrary"),
        name="att_sample",
    )(*job["args"])


def _gated_outproj(x_ref, att_ref, g_ref, rwk, gr_ref, gate_ref, w_ref, fnw_ref, final_norm):
    npair = att_ref.shape[0]
    mixed = [(att_ref[p] * _silu(g_ref[p])).astype(BF16) for p in range(npair)]
    mixed.append((rwk * _silu(gr_ref[...])).astype(BF16))
    acc = _nn(jnp.concatenate(mixed, axis=1), w_ref[...])
    xo = x_ref[...] + gate_ref[...] * acc
    if final_norm:
        xo = xo * lax.rsqrt(jnp.mean(xo * xo, axis=-1, keepdims=True) + NORM_EPS) * fnw_ref[...]
    return xo


def _head_sum(x):
    left = lax.broadcasted_iota(jnp.int32, (1, LANES), 1) < HEAD_DIM
    tiles = []
    for q in range(x.shape[1] // LANES):
        t = x[:, q * LANES:(q + 1) * LANES]
        s_left = jnp.sum(jnp.where(left, t, 0.0), axis=-1, keepdims=True)
        s_right = jnp.sum(jnp.where(left, 0.0, t), axis=-1, keepdims=True)
        tiles.append(jnp.where(left, s_left, s_right))
    return jnp.concatenate(tiles, axis=1)


def _rwkv_features(u, u_prev, mu, w0, wlb, a0, alb, kk_scale, ka, rw):
    um = u + (u_prev - u) * mu
    r = um[:, :rw]
    k = um[:, rw:2 * rw]
    v = um[:, 2 * rw:3 * rw]
    xw = um[:, 3 * rw:3 * rw + LORA_W]
    xa = um[:, 3 * rw + LORA_W:3 * rw + 2 * LORA_W]
    wl = w0 + _nn(jnp.tanh(xw).astype(BF16), wlb)
    logw = -math.exp(-0.5) * jax.nn.sigmoid(wl)
    a = jax.nn.sigmoid(a0 + _nn(xa.astype(BF16), alb))
    kk_raw = k * kk_scale
    k2 = k * (1.0 + (a - 1.0) * ka)
    return r, k2, v, logw, kk_raw, a


def _rwkv_prompt_kernel(r_ref, k2_ref, v_ref, logw_ref, kkn_ref, beta_ref, bonus_ref, s0_ref, lw_ref, lb_ref,
                        y_ref, sout_ref, s_scr, *, rw, rider=None, finish=None):
    c = pl.program_id(1)
    nc = pl.num_programs(1)
    rows = r_ref.shape[0]
    ch = RWKV_CHUNK
    ng = rows // ch
    nh = rw // HEAD_DIM
    hd = HEAD_DIM
    ntile = rw // LANES

    @pl.when(c == 0)
    def _():
        s_scr[...] = jnp.zeros(s_scr.shape, F32)
        for h in range(nh):
            s_scr[h, :, (h % 2) * hd:(h % 2 + 1) * hd] = s0_ref[h]

    if rider is not None:
        rider()

    r, k2, v, logw, kk, beta = r_ref[...], k2_ref[...], v_ref[...], logw_ref[...], kkn_ref[...], beta_ref[...]
    left = lax.broadcasted_iota(jnp.int32, (1, LANES), 1) < hd
    v_x = pltpu.roll(v, hd, 1)

    ti = lax.broadcasted_iota(jnp.int32, (ch, ch), 0)
    si = lax.broadcasted_iota(jnp.int32, (ch, ch), 1)
    tri_b = (ti >= si).astype(BF16)
    row_c = lax.broadcasted_iota(jnp.int32, (ch, 1), 0)
    t2 = lax.broadcasted_iota(jnp.int32, (2 * ch, 2 * ch), 0)
    s2 = lax.broadcasted_iota(jnp.int32, (2 * ch, 2 * ch), 1)
    tt = jnp.where(t2 >= ch, t2 - ch, t2)
    ss = jnp.where(s2 >= ch, s2 - ch, s2)
    score_mask = (tt - ss) >= jnp.where(t2 >= ch, 0, 1)
    tcol = lax.broadcasted_iota(jnp.int32, (ch, LANES), 1)
    trow = lax.broadcasted_iota(jnp.int32, (ch, LANES), 0)
    eye_right = (tcol == trow + ch).astype(F32)
    zeros_b = jnp.zeros((ch, LANES), BF16)
    own_of = [left if h % 2 == 0 else jnp.logical_not(left) for h in range(nh)]
    tile_of = [slice((h // 2) * LANES, (h // 2 + 1) * LANES) for h in range(nh)]
    xtile_of = [slice((((h + 1) // 2) % ntile) * LANES, (((h + 1) // 2) % ntile + 1) * LANES) for h in range(nh)]

    nu = ng * nh
    lhs, rhs, bk, vx, at_b, rt_own, pc = [], [], [], [], [], [], []
    for g in range(ng):
        rs = slice(g * ch, (g + 1) * ch)
        l1 = logw[rs].astype(BF16)
        rem = logw[rs] - l1.astype(F32)
        l2 = rem.astype(BF16)
        l3 = (rem - l2.astype(F32)).astype(BF16)
        cum3 = _nn(tri_b, jnp.concatenate([l1, l2, l3], axis=1))
        cum = cum3[:, :rw] + (cum3[:, rw:2 * rw] + cum3[:, 2 * rw:])
        ctot = cum[ch - 1:ch, :]
        pc_g = jnp.exp(ctot)
        e_cur = jnp.exp(cum)
        e_neg = jnp.exp(-cum)
        e_prev = jnp.where(row_c == 0, 1.0, pltpu.roll(e_cur, 1, 0))
        e_end = pc_g * e_neg
        rt_f = r[rs] * e_cur
        at_g = (-kk[rs] * e_prev).astype(BF16)
        rt_g = rt_f.astype(BF16)
        bt_g = (beta[rs] * e_neg).astype(BF16)
        kt_g = (k2[rs] * e_neg).astype(BF16)
        bh_g = (beta[rs] * e_end).astype(BF16)
        kh_g = (k2[rs] * e_end).astype(BF16)
        vx_g = v_x[rs].astype(BF16)
        for h in range(nh):
            own, tl = own_of[h], tile_of[h]
            zb = jnp.zeros((), BF16)
            lhs.append(jnp.concatenate([jnp.where(own, at_g[:, tl], zb), jnp.where(own, rt_g[:, tl], zb)], axis=0))
            rhs.append(jnp.concatenate([bt_g[:, tl], kt_g[:, tl]], axis=0))
            bk.append(jnp.concatenate([jnp.where(own, bh_g[:, tl], zb), jnp.where(own, kh_g[:, tl], zb)], axis=0))
            vx.append(jnp.where(own, zb, vx_g[:, xtile_of[h]]))
            at_b.append(at_g[:, tl])
            rt_own.append(jnp.where(own, rt_f[:, tl], 0.0))
            pc.append(pc_g[:, tl])

    units = range(nu)
    own_u = [own_of[i % nh] for i in units]
    sc = [jnp.where(score_mask, _nt(lhs[i], rhs[i]), 0.0) for i in units]
    top_b = [sc[i][:ch].astype(BF16) for i in units]
    bot_b = [sc[i][ch:].astype(BF16) for i in units]
    aakv = [_nn(top_b[i], jnp.concatenate([zeros_b, vx[i]], axis=0)) for i in units]
    x = [jnp.where(left, sc[i][:ch], eye_right) for i in units]
    npow = 1
    while npow < ch:
        xb = [x[i].astype(BF16) for i in units]
        x = [_nn(xb[i], jnp.concatenate([xb[i], zeros_b], axis=0)) + jnp.where(left, 0.0, x[i]) for i in units]
        npow *= 2
    z_b = [jnp.where(own_u[i], at_b[i], aakv[i].astype(BF16)) for i in units]
    wu = [_nn(x[i].astype(BF16), jnp.concatenate([zeros_b, z_b[i]], axis=0)) for i in units]
    rhs2 = [jnp.concatenate([wu[i].astype(BF16), vx[i]], axis=0) for i in units]
    qy = [_nn(bot_b[i], rhs2[i]) for i in units]
    pg = [_tn(rhs2[i], bk[i]) for i in units]

    s_cur = [s_scr[h] for h in range(nh)]
    y_sw = []
    for g in range(ng):
        y_u = []
        for h in range(nh):
            i = g * nh + h
            own = own_of[h]
            s_b = s_cur[h].astype(BF16)
            s_pad = jnp.concatenate([zeros_b, s_b] if h % 2 == 0 else [s_b, zeros_b], axis=0)
            qhat_b = jnp.where(own, rt_own[i] + qy[i], 0.0).astype(BF16)
            y_u.append(jnp.where(own, 0.0, qy[i] + _nt(qhat_b, s_pad)))
            g0 = pg[i][ch:] if h % 2 == 0 else pg[i][:ch]
            s_cur[h] = s_cur[h] * pc[i] + _nn(s_b, pg[i].astype(BF16)) + g0
        y_sw.append(jnp.concatenate([y_u[(2 * q - 1) % nh] + y_u[2 * q] for q in range(ntile)], axis=1))
    for h in range(nh):
        s_scr[h] = s_cur[h]
    y = pltpu.roll(jnp.concatenate(y_sw, axis=0), rw - hd, 1)
    mean = _head_sum(y) * (1.0 / hd)
    dev = y - mean
    var = _head_sum(dev * dev) * (1.0 / hd)
    y = dev * lax.rsqrt(var + GN_EPS) * lw_ref[...] + lb_ref[...] + bonus_ref[...]
    y_ref[...] = y if finish is None else finish(y)

    @pl.when(c == nc - 1)
    def _():
        for h in range(nh):
            sout_ref[h] = s_scr[h, :, (h % 2) * hd:(h % 2 + 1) * hd]


def _rwkv_fused_kernel(*refs, rw, n_in, rider_body, rider_in, rider_out, post_in, final_norm):
    own_in, rest = refs[:n_in], refs[n_in:]
    r_in, rest = rest[:rider_in], rest[rider_in:]
    p_in, rest = rest[:post_in], rest[post_in:]
    own_out, rest = rest[:2], rest[2:]
    r_out, scratch = rest[:rider_out], rest[rider_out:]
    rider = (lambda: rider_body(*r_in, *r_out)) if rider_body is not None else None
    finish = None
    if post_in:
        x_ref, att_ref, g_ref, gr_ref, gate_ref, w_ref, fnw_ref = p_in
        finish = lambda y: _gated_outproj(x_ref, att_ref, g_ref, y, gr_ref, gate_ref, w_ref, fnw_ref, final_norm)
    _rwkv_prompt_kernel(*own_in, *own_out, *scratch, rw=rw, rider=rider, finish=finish)


def _rwkv_prompt(feats, s0, p, rider=None, post=None):
    b, t, rw = feats[0].shape
    nh = rw // HEAD_DIM
    hd = HEAD_DIM
    rows = RWKV_CHUNK * RWKV_CHUNKS_PER_STEP
    vec = lambda a: pl.BlockSpec(a.shape, lambda i, c: (0,) * a.ndim)
    state_spec = pl.BlockSpec((None, nh, hd, hd), lambda i, c: (i, 0, 0, 0))
    args = list(feats) + [s0, p["lw"], p["lb"]]
    in_specs = ([pl.BlockSpec((None, rows, rw), lambda i, c: (i, c, 0))] * len(feats)
                + [state_spec, vec(p["lw"]), vec(p["lb"])])
    n_in = len(args)
    out_w = rw
    rider_args, rider_in_specs, rider_out_specs, rider_out_shape, rider_body = [], [], [], [], None
    if rider is not None:
        rider_args, rider_in_specs = rider["args"], rider["in_specs"]
        rider_out_specs, rider_out_shape, rider_body = rider["out_specs"], rider["out_shape"], rider["body"]
    post_args, post_specs, final_norm = [], [], False
    if post is not None:
        x, att, g_att, g_rwkv, gate, w_out_bf16, final_norm_w, final_norm = post
        d = x.shape[-1]
        npair = att.shape[1]
        out_w = d
        pair_spec = pl.BlockSpec((None, npair, rows, LANES), lambda i, c: (i, 0, c, 0))
        post_args = [x, att, g_att, g_rwkv, gate, w_out_bf16, final_norm_w.reshape(1, d)]
        post_specs = [pl.BlockSpec((None, rows, d), lambda i, c: (i, c, 0)), pair_spec, pair_spec,
                      pl.BlockSpec((None, rows, rw), lambda i, c: (i, c, 0)),
                      pl.BlockSpec((None, 1, d), lambda i, c: (i, 0, 0)),
                      vec(w_out_bf16), pl.BlockSpec((1, d), lambda i, c: (0, 0))]
    body = functools.partial(_rwkv_fused_kernel, rw=rw, n_in=n_in, rider_body=rider_body,
                             rider_in=len(rider_args), rider_out=len(rider_out_specs),
                             post_in=len(post_args), final_norm=final_norm)
    return pl.pallas_call(
        body,
        grid=(b, t // rows),
        in_specs=in_specs + rider_in_specs + post_specs,
        out_specs=[pl.BlockSpec((None, rows, out_w), lambda i, c: (i, c, 0)), state_spec] + rider_out_specs,
        out_shape=[jax.ShapeDtypeStruct((b, t, out_w), F32), jax.ShapeDtypeStruct((b, nh, hd, hd), F32)]
        + rider_out_shape,
        scratch_shapes=[pltpu.VMEM((nh, hd, LANES), F32)],
        compiler_params=_cparams("arbitrary", "arbitrary"),
        name="rwkv_prompt",
    )(*args, *rider_args, *post_args)


def _rwkv_feat_kernel(u_ref, prev_ref, mu_ref, w0_ref, wlb_ref, a0_ref, alb_ref, kk_ref, ka_ref,
                      r_ref, k_ref, v_ref, w_ref, kkr_ref, a_ref, *, rw):
    r, k2, v, logw, kk_raw, a = _rwkv_features(u_ref[...], prev_ref[...], mu_ref[...], w0_ref[...], wlb_ref[...],
                                               a0_ref[...], alb_ref[...], kk_ref[...], ka_ref[...], rw)
    r_ref[...] = r.T
    k_ref[...] = k2.T
    v_ref[...] = v.T
    w_ref[...] = jnp.exp(logw).T
    kkr_ref[...] = kk_raw.T
    a_ref[...] = a.T


def _rwkv_step_kernel(s_ref, r_ref, k_ref, v_ref, w_ref, kkr_ref, a_ref, rk_ref, lw_ref, lb_ref,
                      y_ref, sout_ref, y_scr):
    hd = s_ref.shape[0]
    kk_raw = kkr_ref[...]
    kk = kk_raw / jnp.maximum(jnp.sqrt(jnp.sum(kk_raw * kk_raw, axis=0, keepdims=True)), 1e-12)
    beta = kk * a_ref[...]
    w, k2, r, v = w_ref[...], k_ref[...], r_ref[...], v_ref[...]

    def value_row(i, carry):
        s = s_ref[i]
        sa = jnp.sum(s * kk, axis=0, keepdims=True)
        s_new = s * w - sa * beta + v_ref[pl.ds(i, 1), :] * k2
        sout_ref[i] = s_new
        y_scr[pl.ds(i, 1), :] = jnp.sum(s_new * r, axis=0, keepdims=True)
        return carry

    lax.fori_loop(0, hd, value_row, 0)
    y = y_scr[...]
    mean = jnp.mean(y, axis=0, keepdims=True)
    var = jnp.mean(jnp.square(y - mean), axis=0, keepdims=True)
    yn = (y - mean) * lax.rsqrt(var + GN_EPS) * lw_ref[...] + lb_ref[...]
    y_ref[...] = yn + jnp.sum(r * k2 * rk_ref[...], axis=0, keepdims=True) * v


def _rwkv_sample(u, prev, s0_t, p):
    b, sw = u.shape
    rw = p["w0"].shape[-1]
    nh = rw // HEAD_DIM
    hd = HEAD_DIM
    full = lambda a: pl.BlockSpec(a.shape, lambda i: (0,) * a.ndim)
    fparams = [p["mu"], p["w0"], p["wlb"], p["a0"], p["alb"], p["kk"], p["ka"]]
    feats = pl.pallas_call(
        functools.partial(_rwkv_feat_kernel, rw=rw),
        grid=(1,),
        in_specs=[full(u), full(prev)] + [full(a) for a in fparams],
        out_specs=[pl.BlockSpec((rw, b), lambda i: (0, 0))] * 6,
        out_shape=[jax.ShapeDtypeStruct((rw, b), F32)] * 6,
        compiler_params=_cparams("arbitrary"),
        name="rwkv_sample_features",
    )(u, prev, *fparams)
    chan = pl.BlockSpec((hd, b), lambda h: (h, 0))
    col = pl.BlockSpec((hd, 1), lambda h: (h, 0))
    state_spec = pl.BlockSpec((None, hd, hd, b), lambda h: (h, 0, 0, 0))
    cols = [p[n].reshape(rw, 1) for n in ("rk", "lw", "lb")]
    y_t, s_new = pl.pallas_call(
        _rwkv_step_kernel,
        grid=(nh,),
        in_specs=[state_spec] + [chan] * 6 + [col] * 3,
        out_specs=[chan, state_spec],
        out_shape=[jax.ShapeDtypeStruct((rw, b), F32), jax.ShapeDtypeStruct((nh, hd, hd, b), F32)],
        scratch_shapes=[pltpu.VMEM((hd, b), F32)],
        compiler_params=_cparams("arbitrary"),
        name="rwkv_sample_step",
    )(s0_t, *feats, *cols)
    return y_t.T, s_new


def _outproj_kernel(x_ref, att_ref, g_ref, rw_ref, gr_ref, gate_ref, w_ref, fnw_ref, o_ref, *, final_norm):
    o_ref[...] = _gated_outproj(x_ref, att_ref, g_ref, rw_ref[...], gr_ref, gate_ref, w_ref, fnw_ref, final_norm)


def _outproj(x, att, g_att, rwk, g_rwkv, gate, w_out_bf16, final_norm_w, final_norm):
    b, t, d = x.shape
    npair = att.shape[1]
    rw = rwk.shape[-1]
    tm = min(ROW_TILE, t)
    per_row = gate.shape[1] != 1
    gate_spec = (pl.BlockSpec((None, tm, d), lambda i, j: (i, j, 0)) if per_row
                 else pl.BlockSpec((None, 1, d), lambda i, j: (i, 0, 0)))
    pair_spec = pl.BlockSpec((None, npair, tm, LANES), lambda i, j: (i, 0, j, 0))
    return pl.pallas_call(
        functools.partial(_outproj_kernel, final_norm=final_norm),
        grid=(b, t // tm),
        in_specs=[pl.BlockSpec((None, tm, d), lambda i, j: (i, j, 0)), pair_spec, pair_spec,
                  pl.BlockSpec((None, tm, rw), lambda i, j: (i, j, 0)),
                  pl.BlockSpec((None, tm, rw), lambda i, j: (i, j, 0)),
                  gate_spec,
                  pl.BlockSpec(w_out_bf16.shape, lambda i, j: (0, 0)),
                  pl.BlockSpec((1, d), lambda i, j: (0, 0))],
        out_specs=pl.BlockSpec((None, tm, d), lambda i, j: (i, j, 0)),
        out_shape=jax.ShapeDtypeStruct((b, t, d), F32),
        compiler_params=_cparams("arbitrary", "arbitrary"),
        name="out_proj",
    )(x, att, g_att, rwk, g_rwkv, gate, w_out_bf16, final_norm_w.reshape(1, d))


def _pairs_to_heads(a):
    b, npair, t, _ = a.shape
    return a.transpose(0, 2, 1, 3).reshape(b, t, npair * (LANES // HEAD_DIM), HEAD_DIM)


def kernel(x_prompt, x_sample, cache_win_k, cache_win_v, state_wkv, state_shift, c_prompt, c_sample, rel_bias, norm_w, ada_w, ada_b, w_in, mu_shift, w0, w_lora_b, a0, a_lora_b, k_k, k_a, r_k, ln_x_w, ln_x_b, w_out, final_norm_w):
    depth = norm_w.shape[0]
    bp, tp, d = x_prompt.shape
    bs, ts, _ = x_sample.shape
    assert ts == 1, "the sample group decodes one token per sequence"
    rw = w0.shape[-1]
    att_w = w_out.shape[1] - rw
    shift_w = mu_shift.shape[-1]
    nh_att = att_w // HEAD_DIM
    nh_rw = rw // HEAD_DIM
    assert tp % ATT_SUPER == 0 and tp % (RWKV_CHUNK * RWKV_CHUNKS_PER_STEP) == 0 and bs % 8 == 0
    assert cache_win_k.shape[2] == MAX_WINDOW
    keep = min(MAX_WINDOW, tp)

    bias_tiles = _prompt_bias_tiles(rel_bias)
    c_all = jnp.concatenate([c_prompt, c_sample], axis=0)
    npad = -c_all.shape[0] % 8
    c_all = jnp.pad(c_all, ((0, npad), (0, 0)))

    xp = x_prompt
    xs = x_sample.reshape(1, bs, d)
    outs = [[] for _ in range(8)]
    for l in range(depth):
        mod = _ada_mod(c_all, ada_w[l], ada_b[l])
        shift, scale, gate = jnp.split(mod, 3, axis=-1)
        w_in_b = w_in[l].astype(BF16)
        w_out_b = w_out[l].astype(BF16)
        p = dict(mu=mu_shift[l].reshape(1, -1), w0=w0[l].reshape(1, -1), wlb=w_lora_b[l].astype(BF16),
                 a0=a0[l].reshape(1, -1), alb=a_lora_b[l].astype(BF16), kk=k_k[l].reshape(1, -1),
                 ka=k_a[l].reshape(1, -1), rk=r_k[l].reshape(1, -1), lw=ln_x_w[l].reshape(1, -1),
                 lb=ln_x_b[l].reshape(1, -1))
        last = l == depth - 1

        sm = lambda a: a[bp:bp + bs].reshape(1, bs, d)
        q_s, k_s, v_s, g_att_s, g_rwkv_s, u_s = _inproj(xs, sm(shift), sm(scale), norm_w[l], w_in_b, att_w, shift_w)
        nat = lambda a: a[0].transpose(1, 0, 2).reshape(bs, nh_att, HEAD_DIM)
        k_new, v_new = nat(k_s), nat(v_s)
        att_s_args = (nat(q_s), k_new, v_new, cache_win_k[l].transpose(0, 2, 3, 1),
                      cache_win_v[l].transpose(0, 2, 3, 1), rel_bias)

        pm = lambda a: a[:bp].reshape(bp, 1, d)
        prev0 = jnp.zeros((bp, 1, shift_w), F32)
        q, k, v, g_att, g_rwkv, feats, u_last = _inproj(xp, pm(shift), pm(scale), norm_w[l], w_in_b, att_w, shift_w,
                                                        rwkv=(p, prev0))
        att = _att_prompt(q, k, v, bias_tiles)
        s0 = jnp.zeros((bp, nh_rw, HEAD_DIM, HEAD_DIM), F32)
        steps = tp // (RWKV_CHUNK * RWKV_CHUNKS_PER_STEP)
        post = (xp, att, g_att, g_rwkv, pm(gate), w_out_b, final_norm_w, last)
        if bs == bp * steps:
            rider = _att_sample_job(*att_s_args, lambda i, c: i * steps + c)
            xp, s_p, att_s = _rwkv_prompt(feats, s0, p, rider, post)
        else:
            xp, s_p = _rwkv_prompt(feats, s0, p, None, post)
            att_s = _att_sample(*att_s_args)
        outs[0].append(_pairs_to_heads(k[:, :, tp - keep:, :]))
        outs[1].append(_pairs_to_heads(v[:, :, tp - keep:, :]))
        outs[4].append(s_p)
        outs[6].append(u_last[:, 0])

        att_s = att_s.reshape(bs, att_w // LANES, LANES).transpose(1, 0, 2)[None]
        y_rw_s, s_s = _rwkv_sample(u_s[0], state_shift[l], state_wkv[l].transpose(1, 2, 3, 0), p)
        xs = _outproj(xs, att_s, g_att_s, y_rw_s[None], g_rwkv_s, sm(gate), w_out_b, final_norm_w, last)
        outs[2].append(k_new.reshape(bs, 1, nh_att, HEAD_DIM))
        outs[3].append(v_new.reshape(bs, 1, nh_att, HEAD_DIM))
        outs[5].append(s_s.transpose(3, 0, 1, 2))
        outs[7].append(u_s[0])

    stack = lambda i: jnp.stack(outs[i])
    return (xp, xs.reshape(bs, 1, d), stack(0), stack(1), stack(2), stack(3), stack(4), stack(5), stack(6),
            stack(7))
```

```python
import functools
import math

import numpy as np
import jax
import jax.numpy as jnp
from jax import lax
from jax.experimental import pallas as pl
from jax.experimental.pallas import tpu as pltpu

F32 = jnp.float32
BF16 = jnp.bfloat16
HIGHEST = lax.Precision.HIGHEST

HEAD_DIM = 64
DILATED_BRANCHES = ((128, 1), (512, 4), (2048, 16))
MAX_WINDOW = max(w for w, _ in DILATED_BRANCHES)
KEYS_PER_BRANCH = 128
N_BUCKETS = 32
BUCKET_MAX_DIST = MAX_WINDOW
LORA_W = 64
NORM_EPS = 1e-6
GN_EPS = HEAD_DIM * 1e-5
LOG2E = math.log2(math.e)

LANES = 128
VMEM_LIMIT_BYTES = 56 * 1024 * 1024

ROW_TILE = 256
ATT_BLOCK = 128
ATT_SUPER = MAX_WINDOW
ATT_TILES_PER_ITER = {1: 16, 4: 16, 16: 4}
RWKV_CHUNK = 64
RWKV_CHUNKS_PER_STEP = 4


def _cparams(*sem):
    return pltpu.CompilerParams(dimension_semantics=sem, vmem_limit_bytes=VMEM_LIMIT_BYTES)


def _silu(x):
    return x * jax.nn.sigmoid(x)


def _nt(a, b):
    return lax.dot_general(a, b, (((1,), (1,)), ((), ())), preferred_element_type=F32)


def _tn(a, b):
    return lax.dot_general(a, b, (((0,), (0,)), ((), ())), preferred_element_type=F32)


def _nn(a, b, precision=None):
    return jnp.dot(a, b, precision=precision, preferred_element_type=F32)


def _ada_kernel(c_ref, w_ref, b_ref, o_ref):
    s = _silu(c_ref[...])
    o_ref[...] = _nn(s, w_ref[...], HIGHEST) + b_ref[...]


def _ada_mod(c_all, ada_w, ada_b):
    n, d = c_all.shape
    e = ada_w.shape[1]
    tn = 512
    return pl.pallas_call(
        _ada_kernel,
        grid=(e // tn,),
        in_specs=[pl.BlockSpec((n, d), lambda j: (0, 0)),
                  pl.BlockSpec((d, tn), lambda j: (0, j)),
                  pl.BlockSpec((1, tn), lambda j: (0, j))],
        out_specs=pl.BlockSpec((n, tn), lambda j: (0, j)),
        out_shape=jax.ShapeDtypeStruct((n, e), F32),
        compiler_params=_cparams("arbitrary"),
        name="ada_mod",
    )(c_all, ada_w, ada_b.reshape(1, e))


def _inproj_kernel(*refs, att_w, shift_w, rw, features):
    x_ref, shift_ref, scale_ref, nw_ref, w_ref = refs[:5]
    if features:
        prev0_ref, mu_ref, w0_ref, wlb_ref, a0_ref, alb_ref, kk_ref, ka_ref, rk_ref = refs[5:14]
        q_ref, k_ref, v_ref, g_ref, gr_ref = refs[14:19]
        feat_refs, ulast_ref, carry_scr = refs[19:26], refs[26], refs[27]
    else:
        q_ref, k_ref, v_ref, g_ref, gr_ref, u_ref = refs[5:11]
    if features:
        @pl.when(pl.program_id(1) == 0)
        def _():
            carry_scr[...] = prev0_ref[...]

    x = x_ref[...]
    xn = x * lax.rsqrt(jnp.mean(x * x, axis=-1, keepdims=True) + NORM_EPS) * nw_ref[...]
    h = (xn * (1.0 + scale_ref[...]) + shift_ref[...]).astype(BF16)
    u = _nn(h, w_ref[:, 4 * att_w:4 * att_w + shift_w])
    if features:
        rows = u.shape[0]
        shifted = pltpu.roll(u, 1, 0)
        row8 = lax.broadcasted_iota(jnp.int32, (8, 1), 0)
        u_prev = jnp.concatenate([jnp.where(row8 == 0, carry_scr[...], shifted[:8]), shifted[8:]], axis=0)
        carry_scr[...] = u[rows - 1:rows, :]
        ulast_ref[...] = u[rows - 1:rows, :]
        r, k2, v, logw, kk_raw, a = _rwkv_features(u, u_prev, mu_ref[...], w0_ref[...], wlb_ref[...], a0_ref[...],
                                                   alb_ref[...], kk_ref[...], ka_ref[...], rw)
        kk = kk_raw * lax.rsqrt(jnp.maximum(_head_sum(kk_raw * kk_raw), 1e-24))
        bonus = _head_sum(r * k2 * rk_ref[...]) * v
        for ref, val in zip(feat_refs, (r, k2, v, logw, kk, kk * a, bonus)):
            ref[...] = val
    else:
        u_ref[...] = u
    z = _nn(h, w_ref[:, :4 * att_w])
    npair = att_w // LANES
    for p in range(npair):
        q_ref[p] = z[:, p * LANES:(p + 1) * LANES] * (HEAD_DIM ** -0.5 * LOG2E)
        k_ref[p] = z[:, att_w + p * LANES:att_w + (p + 1) * LANES]
        v_ref[p] = z[:, 2 * att_w + p * LANES:2 * att_w + (p + 1) * LANES]
        g_ref[p] = z[:, 3 * att_w + p * LANES:3 * att_w + (p + 1) * LANES]
    gr_ref[...] = _nn(h, w_ref[:, 4 * att_w + shift_w:])


def _inproj(x, shift, scale, norm_w, w_in_bf16, att_w, shift_w, rwkv=None):
    b, t, d = x.shape
    in_w = w_in_bf16.shape[1]
    rw = in_w - 4 * att_w - shift_w
    npair = att_w // LANES
    tm = min(ROW_TILE, t)
    per_row = shift.shape[1] != 1
    mod_spec = (pl.BlockSpec((None, tm, d), lambda i, j: (i, j, 0)) if per_row
                else pl.BlockSpec((None, 1, d), lambda i, j: (i, 0, 0)))
    pair_spec = pl.BlockSpec((None, npair, tm, LANES), lambda i, j: (i, 0, j, 0))
    pair_shape = jax.ShapeDtypeStruct((b, npair, t, LANES), F32)
    row_spec = lambda w: pl.BlockSpec((None, tm, w), lambda i, j: (i, j, 0))
    row_shape = lambda w: jax.ShapeDtypeStruct((b, t, w), F32)
    args = [x, shift, scale, norm_w.reshape(1, d), w_in_bf16]
    in_specs = [row_spec(d), mod_spec, mod_spec, pl.BlockSpec((1, d), lambda i, j: (0, 0)),
                pl.BlockSpec((d, in_w), lambda i, j: (0, 0))]
    out_specs = [pair_spec] * 4 + [row_spec(rw)]
    out_shape = [pair_shape] * 4 + [row_shape(rw)]
    scratch = []
    if rwkv is None:
        out_specs.append(row_spec(shift_w))
        out_shape.append(row_shape(shift_w))
    else:
        p, prev0 = rwkv
        consts = [p[n] for n in ("mu", "w0", "wlb", "a0", "alb", "kk", "ka", "rk")]
        args += [prev0] + consts
        in_specs += [pl.BlockSpec((None, 1, shift_w), lambda i, j: (i, 0, 0))]
        in_specs += [pl.BlockSpec(c.shape, lambda i, j: (0, 0)) for c in consts]
        out_specs += [row_spec(rw)] * 7 + [pl.BlockSpec((None, 1, shift_w), lambda i, j: (i, 0, 0))]
        out_shape += [row_shape(rw)] * 7 + [jax.ShapeDtypeStruct((b, 1, shift_w), F32)]
        scratch = [pltpu.VMEM((1, shift_w), F32)]
    outs = pl.pallas_call(
        functools.partial(_inproj_kernel, att_w=att_w, shift_w=shift_w, rw=rw, features=rwkv is not None),
        grid=(b, t // tm),
        in_specs=in_specs,
        out_specs=out_specs,
        out_shape=out_shape,
        scratch_shapes=scratch,
        compiler_params=_cparams("arbitrary", "arbitrary"),
        name="in_proj",
    )(*args)
    if rwkv is None:
        return outs
    return list(outs[:5]) + [list(outs[5:12]), outs[12]]


def _t5_bucket_np(dist):
    max_exact = N_BUCKETS // 2
    nf = np.maximum(dist, max_exact).astype(np.float32)
    large = max_exact + (np.log(nf / np.float32(max_exact)) / np.float32(math.log(BUCKET_MAX_DIST / max_exact))
                         * np.float32(N_BUCKETS - max_exact)).astype(np.int32)
    large = np.minimum(large, N_BUCKETS - 1)
    return np.where(dist < max_exact, dist, large)


def _branch_bias(rel_bias):
    out = []
    for window, dil in DILATED_BRANCHES:
        dist = dil * np.arange(window // dil + 1, dtype=np.int32)
        out.append(rel_bias[_t5_bucket_np(dist)].T.astype(F32) * LOG2E)
    return jnp.stack(out)


def _prompt_bias_tiles(rel_bias):
    bias = _branch_bias(rel_bias)
    nb, nh, _ = bias.shape
    blk, width = ATT_BLOCK, 2 * ATT_BLOCK
    pad = jnp.full((nb, nh, blk - 1), -jnp.inf, F32)
    strip = jnp.flip(jnp.concatenate([pad, bias, pad], axis=-1), axis=-1)
    length = 3 * blk - 1
    rows = jnp.broadcast_to(jnp.pad(strip, ((0, 0), (0, 0), (0, 1)))[:, :, None, :], (nb, nh, blk, length + 1))
    skew = rows.reshape(nb, nh, blk * (length + 1))[:, :, :blk * length].reshape(nb, nh, blk, length)
    tile = skew[:, :, :, blk - 1:blk - 1 + width]
    first = jnp.where((np.arange(width) >= blk)[None, None, None, :], tile, -jnp.inf)
    return jnp.stack([tile, first], axis=1)


def _att_prompt_kernel(q_ref, k_ref, v_ref, bias_ref, o_ref, m_ref, l_ref, acc_ref, *, super_rows):
    sb = pl.program_id(2)
    lane = lax.broadcasted_iota(jnp.int32, (1, LANES), 1)
    head0 = lane < HEAD_DIM
    ones_b = jnp.ones((2 * ATT_BLOCK, LANES), BF16)

    for bi, (_, dil) in enumerate(DILATED_BRANCHES):
        nblk = super_rows // (ATT_BLOCK * dil)
        group = ATT_TILES_PER_ITER[dil]
        rows = lambda s, dil=dil: (pl.ds(s, ATT_BLOCK, stride=dil) if dil > 1 else pl.ds(s, ATT_BLOCK))

        def tiles(it, carry, bi=bi, dil=dil, nblk=nblk, rows=rows, group=group):
            locs, firsts, q2, kcat, vcat = [], [], [], [], []
            for j in range(group):
                idx = it * group + j
                res = idx // nblk
                loc = res + dil * ATT_BLOCK * (idx - res * nblk)
                glob = sb * super_rows + loc
                prev = glob - dil * ATT_BLOCK
                first = prev < 0
                pstart = jnp.where(first, glob, prev)
                qt = q_ref[rows(loc), :]
                locs.append(loc)
                firsts.append(first.astype(jnp.int32))
                q2.append(jnp.concatenate([jnp.where(head0, qt, 0.0), jnp.where(head0, 0.0, qt)],
                                          axis=0).astype(BF16))
                kcat.append(jnp.concatenate([k_ref[rows(pstart), :], k_ref[rows(glob), :]], axis=0).astype(BF16))
                vcat.append(jnp.concatenate([v_ref[rows(pstart), :], v_ref[rows(glob), :]], axis=0).astype(BF16))
            s = [_nt(q2[j], kcat[j]) + bias_ref[bi, firsts[j]].reshape(2 * ATT_BLOCK, 2 * ATT_BLOCK)
                 for j in range(group)]
            mt = [jnp.max(x, axis=-1, keepdims=True) for x in s]
            p = [jnp.exp2(s[j] - mt[j]).astype(BF16) for j in range(group)]
            pv = [_nn(p[j], jnp.concatenate([vcat[j], ones_b], axis=1)) for j in range(group)]
            for j in range(group):
                r = rows(locs[j])
                m_ref[bi, r, :] = jnp.where(head0, mt[j][:ATT_BLOCK], mt[j][ATT_BLOCK:])
                l_ref[bi, r, :] = jnp.where(head0, pv[j][:ATT_BLOCK, LANES:], pv[j][ATT_BLOCK:, LANES:])
                acc_ref[bi, r, :] = jnp.where(head0, pv[j][:ATT_BLOCK, :LANES], pv[j][ATT_BLOCK:, :LANES])
            return carry

        lax.fori_loop(0, dil * nblk // group, tiles, 0)

    nb = len(DILATED_BRANCHES)
    chunk = 2 * ATT_BLOCK

    def merge(c, carry):
        r = pl.ds(pl.multiple_of(c * chunk, chunk), chunk)
        ms = [m_ref[n, r, :] for n in range(nb)]
        m = functools.reduce(jnp.maximum, ms)
        ws = [jnp.exp2(x - m) for x in ms]
        den = functools.reduce(jnp.add, [l_ref[n, r, :] * ws[n] for n in range(nb)])
        num = functools.reduce(jnp.add, [acc_ref[n, r, :] * ws[n] for n in range(nb)])
        o_ref[r, :] = num / den
        return carry

    lax.fori_loop(0, super_rows // chunk, merge, 0)


def _att_prompt(q, k, v, bias_tiles):
    b, npair, t, _ = q.shape
    sr = ATT_SUPER
    nb = len(DILATED_BRANCHES)
    return pl.pallas_call(
        functools.partial(_att_prompt_kernel, super_rows=sr),
        grid=(b, npair, t // sr),
        in_specs=[pl.BlockSpec((None, None, sr, LANES), lambda i, p, s: (i, p, s, 0)),
                  pl.BlockSpec((None, None, t, LANES), lambda i, p, s: (i, p, 0, 0)),
                  pl.BlockSpec((None, None, t, LANES), lambda i, p, s: (i, p, 0, 0)),
                  pl.BlockSpec((nb, 2, 2, ATT_BLOCK, 2 * ATT_BLOCK), lambda i, p, s: (0, 0, p, 0, 0))],
        out_specs=pl.BlockSpec((None, None, sr, LANES), lambda i, p, s: (i, p, s, 0)),
        out_shape=jax.ShapeDtypeStruct((b, npair, t, LANES), F32),
        scratch_shapes=[pltpu.VMEM((nb, sr, LANES), F32)] * 3,
        compiler_params=_cparams("arbitrary", "arbitrary", "arbitrary"),
        name="att_prompt",
    )(q, k, v, bias_tiles)


def _att_sample_kernel(q_ref, kn_ref, vn_ref, kt_ref, vt_ref, bias_ref, b0_ref, o_ref):
    nh, hd, l = kt_ref.shape
    nb = bias_ref.shape[0]
    q = q_ref[...]
    qb = q.astype(BF16)
    head = lax.broadcasted_iota(jnp.int32, (nh, 1), 0)
    logit = jnp.zeros((nh, l), F32)
    for h in range(nh):
        s = _nn(qb, kt_ref[h].astype(BF16))
        logit = jnp.where(head == h, s, logit)
    s0 = jnp.sum(q * kn_ref[...], axis=-1, keepdims=True)
    ls = [logit + bias_ref[n] for n in range(nb)]
    s_self = [s0 + b0_ref[n] for n in range(nb)]
    m = functools.reduce(jnp.maximum, [jnp.max(x, axis=-1, keepdims=True) for x in ls] + s_self)
    p = functools.reduce(jnp.add, [jnp.exp2(x - m) for x in ls])
    p0 = functools.reduce(jnp.add, [jnp.exp2(x - m) for x in s_self])
    den = jnp.sum(p, axis=-1, keepdims=True) + p0
    pb = p.astype(BF16)
    out = p0 * vn_ref[...]
    for h in range(nh):
        out = out + jnp.where(head == h, _nt(pb, vt_ref[h].astype(BF16)), 0.0)
    o_ref[...] = out / den


def _att_sample_job(q, k_new, v_new, cache_kt, cache_vt, rel_bias, seq_of_step):
    b, nh, hd = q.shape
    l = cache_kt.shape[-1]
    bias = _branch_bias(rel_bias)
    back = l - np.arange(l)
    tabs = []
    for n, (window, dil) in enumerate(DILATED_BRANCHES):
        used = (back % dil == 0) & (back <= window)
        tabs.append(jnp.where(used[None, :], bias[n][:, np.where(used, back // dil, 0)], -jnp.inf))
    bias_pos = jnp.stack(tabs)
    bias0 = bias[:, :, 0][..., None]
    row = pl.BlockSpec((None, nh, hd), lambda *g: (seq_of_step(*g), 0, 0))
    cache = pl.BlockSpec((None, nh, hd, l), lambda *g: (seq_of_step(*g), 0, 0, 0))
    full = lambda a: pl.BlockSpec(a.shape, lambda *g: (0,) * a.ndim)
    return dict(body=_att_sample_kernel, args=[q, k_new, v_new, cache_kt, cache_vt, bias_pos, bias0],
                in_specs=[row, row, row, cache, cache, full(bias_pos), full(bias0)],
                out_specs=[row], out_shape=[jax.ShapeDtypeStruct((b, nh, hd), F32)])


def _att_sample(q, k_new, v_new, cache_kt, cache_vt, rel_bias):
    job = _att_sample_job(q, k_new, v_new, cache_kt, cache_vt, rel_bias, lambda i: i)
    return pl.pallas_call(
        job["body"],
        grid=(q.shape[0],),
        in_specs=job["in_specs"],
        out_specs=job["out_specs"][0],
        out_shape=job["out_shape"][0],
        compiler_params=_cparams("arbitrary"),
        name="att_sample",
    )(*job["args"])


def _gated_outproj(x_ref, att_ref, g_ref, rwk, gr_ref, gate_ref, w_ref, fnw_ref, final_norm):
    npair = att_ref.shape[0]
    mixed = [(att_ref[p] * _silu(g_ref[p])).astype(BF16) for p in range(npair)]
    mixed.append((rwk * _silu(gr_ref[...])).astype(BF16))
    acc = _nn(jnp.concatenate(mixed, axis=1), w_ref[...])
    xo = x_ref[...] + gate_ref[...] * acc
    if final_norm:
        xo = xo * lax.rsqrt(jnp.mean(xo * xo, axis=-1, keepdims=True) + NORM_EPS) * fnw_ref[...]
    return xo


def _head_sum(x):
    left = lax.broadcasted_iota(jnp.int32, (1, LANES), 1) < HEAD_DIM
    tiles = []
    for q in range(x.shape[1] // LANES):
        t = x[:, q * LANES:(q + 1) * LANES]
        s_left = jnp.sum(jnp.where(left, t, 0.0), axis=-1, keepdims=True)
        s_right = jnp.sum(jnp.where(left, 0.0, t), axis=-1, keepdims=True)
        tiles.append(jnp.where(left, s_left, s_right))
    return jnp.concatenate(tiles, axis=1)


def _rwkv_features(u, u_prev, mu, w0, wlb, a0, alb, kk_scale, ka, rw):
    um = u + (u_prev - u) * mu
    r = um[:, :rw]
    k = um[:, rw:2 * rw]
    v = um[:, 2 * rw:3 * rw]
    xw = um[:, 3 * rw:3 * rw + LORA_W]
    xa = um[:, 3 * rw + LORA_W:3 * rw + 2 * LORA_W]
    wl = w0 + _nn(jnp.tanh(xw).astype(BF16), wlb)
    logw = -math.exp(-0.5) * jax.nn.sigmoid(wl)
    a = jax.nn.sigmoid(a0 + _nn(xa.astype(BF16), alb))
    kk_raw = k * kk_scale
    k2 = k * (1.0 + (a - 1.0) * ka)
    return r, k2, v, logw, kk_raw, a


def _rwkv_prompt_kernel(r_ref, k2_ref, v_ref, logw_ref, kkn_ref, beta_ref, bonus_ref, s0_ref, lw_ref, lb_ref,
                        y_ref, sout_ref, s_scr, *, rw, rider=None, finish=None):
    c = pl.program_id(1)
    nc = pl.num_programs(1)
    rows = r_ref.shape[0]
    ch = RWKV_CHUNK
    ng = rows // ch
    nh = rw // HEAD_DIM
    hd = HEAD_DIM
    ntile = rw // LANES

    @pl.when(c == 0)
    def _():
        s_scr[...] = jnp.zeros(s_scr.shape, F32)
        for h in range(nh):
            s_scr[h, :, (h % 2) * hd:(h % 2 + 1) * hd] = s0_ref[h]

    if rider is not None:
        rider()

    r, k2, v, logw, kk, beta = r_ref[...], k2_ref[...], v_ref[...], logw_ref[...], kkn_ref[...], beta_ref[...]
    left = lax.broadcasted_iota(jnp.int32, (1, LANES), 1) < hd
    v_x = pltpu.roll(v, hd, 1)

    ti = lax.broadcasted_iota(jnp.int32, (ch, ch), 0)
    si = lax.broadcasted_iota(jnp.int32, (ch, ch), 1)
    tri_b = (ti >= si).astype(BF16)
    row_c = lax.broadcasted_iota(jnp.int32, (ch, 1), 0)
    t2 = lax.broadcasted_iota(jnp.int32, (2 * ch, 2 * ch), 0)
    s2 = lax.broadcasted_iota(jnp.int32, (2 * ch, 2 * ch), 1)
    tt = jnp.where(t2 >= ch, t2 - ch, t2)
    ss = jnp.where(s2 >= ch, s2 - ch, s2)
    score_mask = (tt - ss) >= jnp.where(t2 >= ch, 0, 1)
    tcol = lax.broadcasted_iota(jnp.int32, (ch, LANES), 1)
    trow = lax.broadcasted_iota(jnp.int32, (ch, LANES), 0)
    eye_right = (tcol == trow + ch).astype(F32)
    zeros_b = jnp.zeros((ch, LANES), BF16)
    own_of = [left if h % 2 == 0 else jnp.logical_not(left) for h in range(nh)]
    tile_of = [slice((h // 2) * LANES, (h // 2 + 1) * LANES) for h in range(nh)]
    xtile_of = [slice((((h + 1) // 2) % ntile) * LANES, (((h + 1) // 2) % ntile + 1) * LANES) for h in range(nh)]

    nu = ng * nh
    lhs, rhs, bk, vx, at_b, rt_own, pc = [], [], [], [], [], [], []
    for g in range(ng):
        rs = slice(g * ch, (g + 1) * ch)
        l1 = logw[rs].astype(BF16)
        rem = logw[rs] - l1.astype(F32)
        l2 = rem.astype(BF16)
        l3 = (rem - l2.astype(F32)).astype(BF16)
        cum3 = _nn(tri_b, jnp.concatenate([l1, l2, l3], axis=1))
        cum = cum3[:, :rw] + (cum3[:, rw:2 * rw] + cum3[:, 2 * rw:])
        ctot = cum[ch - 1:ch, :]
        pc_g = jnp.exp(ctot)
        e_cur = jnp.exp(cum)
        e_neg = jnp.exp(-cum)
        e_prev = jnp.where(row_c == 0, 1.0, pltpu.roll(e_cur, 1, 0))
        e_end = pc_g * e_neg
        rt_f = r[rs] * e_cur
        at_g = (-kk[rs] * e_prev).astype(BF16)
        rt_g = rt_f.astype(BF16)
        bt_g = (beta[rs] * e_neg).astype(BF16)
        kt_g = (k2[rs] * e_neg).astype(BF16)
        bh_g = (beta[rs] * e_end).astype(BF16)
        kh_g = (k2[rs] * e_end).astype(BF16)
        vx_g = v_x[rs].astype(BF16)
        for h in range(nh):
            own, tl = own_of[h], tile_of[h]
            zb = jnp.zeros((), BF16)
            lhs.append(jnp.concatenate([jnp.where(own, at_g[:, tl], zb), jnp.where(own, rt_g[:, tl], zb)], axis=0))
            rhs.append(jnp.concatenate([bt_g[:, tl], kt_g[:, tl]], axis=0))
            bk.append(jnp.concatenate([jnp.where(own, bh_g[:, tl], zb), jnp.where(own, kh_g[:, tl], zb)], axis=0))
            vx.append(jnp.where(own, zb, vx_g[:, xtile_of[h]]))
            at_b.append(at_g[:, tl])
            rt_own.append(jnp.where(own, rt_f[:, tl], 0.0))
            pc.append(pc_g[:, tl])

    units = range(nu)
    own_u = [own_of[i % nh] for i in units]
    sc = [jnp.where(score_mask, _nt(lhs[i], rhs[i]), 0.0) for i in units]
    top_b = [sc[i][:ch].astype(BF16) for i in units]
    bot_b = [sc[i][ch:].astype(BF16) for i in units]
    aakv = [_nn(top_b[i], jnp.concatenate([zeros_b, vx[i]], axis=0)) for i in units]
    x = [jnp.where(left, sc[i][:ch], eye_right) for i in units]
    npow = 1
    while npow < ch:
        xb = [x[i].astype(BF16) for i in units]
        x = [_nn(xb[i], jnp.concatenate([xb[i], zeros_b], axis=0)) + jnp.where(left, 0.0, x[i]) for i in units]
        npow *= 2
    z_b = [jnp.where(own_u[i], at_b[i], aakv[i].astype(BF16)) for i in units]
    wu = [_nn(x[i].astype(BF16), jnp.concatenate([zeros_b, z_b[i]], axis=0)) for i in units]
    rhs2 = [jnp.concatenate([wu[i].astype(BF16), vx[i]], axis=0) for i in units]
    qy = [_nn(bot_b[i], rhs2[i]) for i in units]
    pg = [_tn(rhs2[i], bk[i]) for i in units]

    s_cur = [s_scr[h] for h in range(nh)]
    y_sw = []
    for g in range(ng):
        y_u = []
        for h in range(nh):
            i = g * nh + h
            own = own_of[h]
            s_b = s_cur[h].astype(BF16)
            s_pad = jnp.concatenate([zeros_b, s_b] if h % 2 == 0 else [s_b, zeros_b], axis=0)
            qhat_b = jnp.where(own, rt_own[i] + qy[i], 0.0).astype(BF16)
            y_u.append(jnp.where(own, 0.0, qy[i] + _nt(qhat_b, s_pad)))
            g0 = pg[i][ch:] if h % 2 == 0 else pg[i][:ch]
            s_cur[h] = s_cur[h] * pc[i] + _nn(s_b, pg[i].astype(BF16)) + g0
        y_sw.append(jnp.concatenate([y_u[(2 * q - 1) % nh] + y_u[2 * q] for q in range(ntile)], axis=1))
    for h in range(nh):
        s_scr[h] = s_cur[h]
    y = pltpu.roll(jnp.concatenate(y_sw, axis=0), rw - hd, 1)
    mean = _head_sum(y) * (1.0 / hd)
    dev = y - mean
    var = _head_sum(dev * dev) * (1.0 / hd)
    y = dev * lax.rsqrt(var + GN_EPS) * lw_ref[...] + lb_ref[...] + bonus_ref[...]
    y_ref[...] = y if finish is None else finish(y)

    @pl.when(c == nc - 1)
    def _():
        for h in range(nh):
            sout_ref[h] = s_scr[h, :, (h % 2) * hd:(h % 2 + 1) * hd]


def _rwkv_fused_kernel(*refs, rw, n_in, rider_body, rider_in, rider_out, post_in, final_norm):
    own_in, rest = refs[:n_in], refs[n_in:]
    r_in, rest = rest[:rider_in], rest[rider_in:]
    p_in, rest = rest[:post_in], rest[post_in:]
    own_out, rest = rest[:2], rest[2:]
    r_out, scratch = rest[:rider_out], rest[rider_out:]
    rider = (lambda: rider_body(*r_in, *r_out)) if rider_body is not None else None
    finish = None
    if post_in:
        x_ref, att_ref, g_ref, gr_ref, gate_ref, w_ref, fnw_ref = p_in
        finish = lambda y: _gated_outproj(x_ref, att_ref, g_ref, y, gr_ref, gate_ref, w_ref, fnw_ref, final_norm)
    _rwkv_prompt_kernel(*own_in, *own_out, *scratch, rw=rw, rider=rider, finish=finish)


def _rwkv_prompt(feats, s0, p, rider=None, post=None):
    b, t, rw = feats[0].shape
    nh = rw // HEAD_DIM
    hd = HEAD_DIM
    rows = RWKV_CHUNK * RWKV_CHUNKS_PER_STEP
    vec = lambda a: pl.BlockSpec(a.shape, lambda i, c: (0,) * a.ndim)
    state_spec = pl.BlockSpec((None, nh, hd, hd), lambda i, c: (i, 0, 0, 0))
    args = list(feats) + [s0, p["lw"], p["lb"]]
    in_specs = ([pl.BlockSpec((None, rows, rw), lambda i, c: (i, c, 0))] * len(feats)
                + [state_spec, vec(p["lw"]), vec(p["lb"])])
    n_in = len(args)
    out_w = rw
    rider_args, rider_in_specs, rider_out_specs, rider_out_shape, rider_body = [], [], [], [], None
    if rider is not None:
        rider_args, rider_in_specs = rider["args"], rider["in_specs"]
        rider_out_specs, rider_out_shape, rider_body = rider["out_specs"], rider["out_shape"], rider["body"]
    post_args, post_specs, final_norm = [], [], False
    if post is not None:
        x, att, g_att, g_rwkv, gate, w_out_bf16, final_norm_w, final_norm = post
        d = x.shape[-1]
        npair = att.shape[1]
        out_w = d
        pair_spec = pl.BlockSpec((None, npair, rows, LANES), lambda i, c: (i, 0, c, 0))
        post_args = [x, att, g_att, g_rwkv, gate, w_out_bf16, final_norm_w.reshape(1, d)]
        post_specs = [pl.BlockSpec((None, rows, d), lambda i, c: (i, c, 0)), pair_spec, pair_spec,
                      pl.BlockSpec((None, rows, rw), lambda i, c: (i, c, 0)),
                      pl.BlockSpec((None, 1, d), lambda i, c: (i, 0, 0)),
                      vec(w_out_bf16), pl.BlockSpec((1, d), lambda i, c: (0, 0))]
    body = functools.partial(_rwkv_fused_kernel, rw=rw, n_in=n_in, rider_body=rider_body,
                             rider_in=len(rider_args), rider_out=len(rider_out_specs),
                             post_in=len(post_args), final_norm=final_norm)
    return pl.pallas_call(
        body,
        grid=(b, t // rows),
        in_specs=in_specs + rider_in_specs + post_specs,
        out_specs=[pl.BlockSpec((None, rows, out_w), lambda i, c: (i, c, 0)), state_spec] + rider_out_specs,
        out_shape=[jax.ShapeDtypeStruct((b, t, out_w), F32), jax.ShapeDtypeStruct((b, nh, hd, hd), F32)]
        + rider_out_shape,
        scratch_shapes=[pltpu.VMEM((nh, hd, LANES), F32)],
        compiler_params=_cparams("arbitrary", "arbitrary"),
        name="rwkv_prompt",
    )(*args, *rider_args, *post_args)


def _rwkv_feat_kernel(u_ref, prev_ref, mu_ref, w0_ref, wlb_ref, a0_ref, alb_ref, kk_ref, ka_ref,
                      r_ref, k_ref, v_ref, w_ref, kkr_ref, a_ref, *, rw):
    r, k2, v, logw, kk_raw, a = _rwkv_features(u_ref[...], prev_ref[...], mu_ref[...], w0_ref[...], wlb_ref[...],
                                               a0_ref[...], alb_ref[...], kk_ref[...], ka_ref[...], rw)
    r_ref[...] = r.T
    k_ref[...] = k2.T
    v_ref[...] = v.T
    w_ref[...] = jnp.exp(logw).T
    kkr_ref[...] = kk_raw.T
    a_ref[...] = a.T


def _rwkv_step_kernel(s_ref, r_ref, k_ref, v_ref, w_ref, kkr_ref, a_ref, rk_ref, lw_ref, lb_ref,
                      y_ref, sout_ref, y_scr):
    hd = s_ref.shape[0]
    kk_raw = kkr_ref[...]
    kk = kk_raw / jnp.maximum(jnp.sqrt(jnp.sum(kk_raw * kk_raw, axis=0, keepdims=True)), 1e-12)
    beta = kk * a_ref[...]
    w, k2, r, v = w_ref[...], k_ref[...], r_ref[...], v_ref[...]

    def value_row(i, carry):
        s = s_ref[i]
        sa = jnp.sum(s * kk, axis=0, keepdims=True)
        s_new = s * w - sa * beta + v_ref[pl.ds(i, 1), :] * k2
        sout_ref[i] = s_new
        y_scr[pl.ds(i, 1), :] = jnp.sum(s_new * r, axis=0, keepdims=True)
        return carry

    lax.fori_loop(0, hd, value_row, 0)
    y = y_scr[...]
    mean = jnp.mean(y, axis=0, keepdims=True)
    var = jnp.mean(jnp.square(y - mean), axis=0, keepdims=True)
    yn = (y - mean) * lax.rsqrt(var + GN_EPS) * lw_ref[...] + lb_ref[...]
    y_ref[...] = yn + jnp.sum(r * k2 * rk_ref[...], axis=0, keepdims=True) * v


def _rwkv_sample(u, prev, s0_t, p):
    b, sw = u.shape
    rw = p["w0"].shape[-1]
    nh = rw // HEAD_DIM
    hd = HEAD_DIM
    full = lambda a: pl.BlockSpec(a.shape, lambda i: (0,) * a.ndim)
    fparams = [p["mu"], p["w0"], p["wlb"], p["a0"], p["alb"], p["kk"], p["ka"]]
    feats = pl.pallas_call(
        functools.partial(_rwkv_feat_kernel, rw=rw),
        grid=(1,),
        in_specs=[full(u), full(prev)] + [full(a) for a in fparams],
        out_specs=[pl.BlockSpec((rw, b), lambda i: (0, 0))] * 6,
        out_shape=[jax.ShapeDtypeStruct((rw, b), F32)] * 6,
        compiler_params=_cparams("arbitrary"),
        name="rwkv_sample_features",
    )(u, prev, *fparams)
    chan = pl.BlockSpec((hd, b), lambda h: (h, 0))
    col = pl.BlockSpec((hd, 1), lambda h: (h, 0))
    state_spec = pl.BlockSpec((None, hd, hd, b), lambda h: (h, 0, 0, 0))
    cols = [p[n].reshape(rw, 1) for n in ("rk", "lw", "lb")]
    y_t, s_new = pl.pallas_call(
        _rwkv_step_kernel,
        grid=(nh,),
        in_specs=[state_spec] + [chan] * 6 + [col] * 3,
        out_specs=[chan, state_spec],
        out_shape=[jax.ShapeDtypeStruct((rw, b), F32), jax.ShapeDtypeStruct((nh, hd, hd, b), F32)],
        scratch_shapes=[pltpu.VMEM((hd, b), F32)],
        compiler_params=_cparams("arbitrary"),
        name="rwkv_sample_step",
    )(s0_t, *feats, *cols)
    return y_t.T, s_new


def _outproj_kernel(x_ref, att_ref, g_ref, rw_ref, gr_ref, gate_ref, w_ref, fnw_ref, o_ref, *, final_norm):
    o_ref[...] = _gated_outproj(x_ref, att_ref, g_ref, rw_ref[...], gr_ref, gate_ref, w_ref, fnw_ref, final_norm)


def _outproj(x, att, g_att, rwk, g_rwkv, gate, w_out_bf16, final_norm_w, final_norm):
    b, t, d = x.shape
    npair = att.shape[1]
    rw = rwk.shape[-1]
    tm = min(ROW_TILE, t)
    per_row = gate.shape[1] != 1
    gate_spec = (pl.BlockSpec((None, tm, d), lambda i, j: (i, j, 0)) if per_row
                 else pl.BlockSpec((None, 1, d), lambda i, j: (i, 0, 0)))
    pair_spec = pl.BlockSpec((None, npair, tm, LANES), lambda i, j: (i, 0, j, 0))
    return pl.pallas_call(
        functools.partial(_outproj_kernel, final_norm=final_norm),
        grid=(b, t // tm),
        in_specs=[pl.BlockSpec((None, tm, d), lambda i, j: (i, j, 0)), pair_spec, pair_spec,
                  pl.BlockSpec((None, tm, rw), lambda i, j: (i, j, 0)),
                  pl.BlockSpec((None, tm, rw), lambda i, j: (i, j, 0)),
                  gate_spec,
                  pl.BlockSpec(w_out_bf16.shape, lambda i, j: (0, 0)),
                  pl.BlockSpec((1, d), lambda i, j: (0, 0))],
        out_specs=pl.BlockSpec((None, tm, d), lambda i, j: (i, j, 0)),
        out_shape=jax.ShapeDtypeStruct((b, t, d), F32),
        compiler_params=_cparams("arbitrary", "arbitrary"),
        name="out_proj",
    )(x, att, g_att, rwk, g_rwkv, gate, w_out_bf16, final_norm_w.reshape(1, d))


def _pairs_to_heads(a):
    b, npair, t, _ = a.shape
    return a.transpose(0, 2, 1, 3).reshape(b, t, npair * (LANES // HEAD_DIM), HEAD_DIM)


def kernel(x_prompt, x_sample, cache_win_k, cache_win_v, state_wkv, state_shift, c_prompt, c_sample, rel_bias, norm_w, ada_w, ada_b, w_in, mu_shift, w0, w_lora_b, a0, a_lora_b, k_k, k_a, r_k, ln_x_w, ln_x_b, w_out, final_norm_w):
    depth = norm_w.shape[0]
    bp, tp, d = x_prompt.shape
    bs, ts, _ = x_sample.shape
    assert ts == 1, "the sample group decodes one token per sequence"
    rw = w0.shape[-1]
    att_w = w_out.shape[1] - rw
    shift_w = mu_shift.shape[-1]
    nh_att = att_w // HEAD_DIM
    nh_rw = rw // HEAD_DIM
    assert tp % ATT_SUPER == 0 and tp % (RWKV_CHUNK * RWKV_CHUNKS_PER_STEP) == 0 and bs % 8 == 0
    assert cache_win_k.shape[2] == MAX_WINDOW
    keep = min(MAX_WINDOW, tp)

    bias_tiles = _prompt_bias_tiles(rel_bias)
    c_all = jnp.concatenate([c_prompt, c_sample], axis=0)
    npad = -c_all.shape[0] % 8
    c_all = jnp.pad(c_all, ((0, npad), (0, 0)))

    xp = x_prompt
    xs = x_sample.reshape(1, bs, d)
    outs = [[] for _ in range(8)]
    for l in range(depth):
        mod = _ada_mod(c_all, ada_w[l], ada_b[l])
        shift, scale, gate = jnp.split(mod, 3, axis=-1)
        w_in_b = w_in[l].astype(BF16)
        w_out_b = w_out[l].astype(BF16)
        p = dict(mu=mu_shift[l].reshape(1, -1), w0=w0[l].reshape(1, -1), wlb=w_lora_b[l].astype(BF16),
                 a0=a0[l].reshape(1, -1), alb=a_lora_b[l].astype(BF16), kk=k_k[l].reshape(1, -1),
                 ka=k_a[l].reshape(1, -1), rk=r_k[l].reshape(1, -1), lw=ln_x_w[l].reshape(1, -1),
                 lb=ln_x_b[l].reshape(1, -1))
        last = l == depth - 1

        sm = lambda a: a[bp:bp + bs].reshape(1, bs, d)
        q_s, k_s, v_s, g_att_s, g_rwkv_s, u_s = _inproj(xs, sm(shift), sm(scale), norm_w[l], w_in_b, att_w, shift_w)
        nat = lambda a: a[0].transpose(1, 0, 2).reshape(bs, nh_att, HEAD_DIM)
        k_new, v_new = nat(k_s), nat(v_s)
        att_s_args = (nat(q_s), k_new, v_new, cache_win_k[l].transpose(0, 2, 3, 1),
                      cache_win_v[l].transpose(0, 2, 3, 1), rel_bias)

        pm = lambda a: a[:bp].reshape(bp, 1, d)
        prev0 = jnp.zeros((bp, 1, shift_w), F32)
        q, k, v, g_att, g_rwkv, feats, u_last = _inproj(xp, pm(shift), pm(scale), norm_w[l], w_in_b, att_w, shift_w,
                                                        rwkv=(p, prev0))
        att = _att_prompt(q, k, v, bias_tiles)
        s0 = jnp.zeros((bp, nh_rw, HEAD_DIM, HEAD_DIM), F32)
        steps = tp // (RWKV_CHUNK * RWKV_CHUNKS_PER_STEP)
        post = (xp, att, g_att, g_rwkv, pm(gate), w_out_b, final_norm_w, last)
        if bs == bp * steps:
            rider = _att_sample_job(*att_s_args, lambda i, c: i * steps + c)
            xp, s_p, att_s = _rwkv_prompt(feats, s0, p, rider, post)
        else:
            xp, s_p = _rwkv_prompt(feats, s0, p, None, post)
            att_s = _att_sample(*att_s_args)
        outs[0].append(_pairs_to_heads(k[:, :, tp - keep:, :]))
        outs[1].append(_pairs_to_heads(v[:, :, tp - keep:, :]))
        outs[4].append(s_p)
        outs[6].append(u_last[:, 0])

        att_s = att_s.reshape(bs, att_w // LANES, LANES).transpose(1, 0, 2)[None]
        y_rw_s, s_s = _rwkv_sample(u_s[0], state_shift[l], state_wkv[l].transpose(1, 2, 3, 0), p)
        xs = _outproj(xs, att_s, g_att_s, y_rw_s[None], g_rwkv_s, sm(gate), w_out_b, final_norm_w, last)
        outs[2].append(k_new.reshape(bs, 1, nh_att, HEAD_DIM))
        outs[3].append(v_new.reshape(bs, 1, nh_att, HEAD_DIM))
        outs[5].append(s_s.transpose(3, 0, 1, 2))
        outs[7].append(u_s[0])

    stack = lambda i: jnp.stack(outs[i])
    return (xp, xs.reshape(bs, 1, d), stack(0), stack(1), stack(2), stack(3), stack(4), stack(5), stack(6),
            stack(7))
```

```python
import functools
import math

import numpy as np
import jax
import jax.numpy as jnp
from jax import lax
from jax.experimental import pallas as pl
from jax.experimental.pallas import tpu as pltpu

F32 = jnp.float32
BF16 = jnp.bfloat16
HIGHEST = lax.Precision.HIGHEST

HEAD_DIM = 64
DILATED_BRANCHES = ((128, 1), (512, 4), (2048, 16))
MAX_WINDOW = max(w for w, _ in DILATED_BRANCHES)
KEYS_PER_BRANCH = 128
N_BUCKETS = 32
BUCKET_MAX_DIST = MAX_WINDOW
LORA_W = 64
NORM_EPS = 1e-6
GN_EPS = HEAD_DIM * 1e-5
LOG2E = math.log2(math.e)

LANES = 128
VMEM_LIMIT_BYTES = 56 * 1024 * 1024

ROW_TILE = 256
ATT_BLOCK = 128
ATT_SUPER = MAX_WINDOW
ATT_TILES_PER_ITER = {1: 16, 4: 16, 16: 4}
RWKV_CHUNK = 64
RWKV_CHUNKS_PER_STEP = 4


def _cparams(*sem):
    return pltpu.CompilerParams(dimension_semantics=sem, vmem_limit_bytes=VMEM_LIMIT_BYTES)


def _silu(x):
    return x * jax.nn.sigmoid(x)


def _nt(a, b):
    return lax.dot_general(a, b, (((1,), (1,)), ((), ())), preferred_element_type=F32)


def _tn(a, b):
    return lax.dot_general(a, b, (((0,), (0,)), ((), ())), preferred_element_type=F32)


def _nn(a, b, precision=None):
    return jnp.dot(a, b, precision=precision, preferred_element_type=F32)


def _ada_kernel(c_ref, w_ref, b_ref, o_ref):
    s = _silu(c_ref[...])
    o_ref[...] = _nn(s.astype(BF16), w_ref[...].astype(BF16)) + b_ref[...]


def _ada_mod(c_all, ada_w, ada_b):
    n, d = c_all.shape
    e = ada_w.shape[1]
    tn = 512
    return pl.pallas_call(
        _ada_kernel,
        grid=(e // tn,),
        in_specs=[pl.BlockSpec((n, d), lambda j: (0, 0)),
                  pl.BlockSpec((d, tn), lambda j: (0, j)),
                  pl.BlockSpec((1, tn), lambda j: (0, j))],
        out_specs=pl.BlockSpec((n, tn), lambda j: (0, j)),
        out_shape=jax.ShapeDtypeStruct((n, e), F32),
        compiler_params=_cparams("arbitrary"),
        name="ada_mod",
    )(c_all, ada_w, ada_b.reshape(1, e))


def _inproj_kernel(*refs, att_w, shift_w, rw, features):
    x_ref, shift_ref, scale_ref, nw_ref, w_ref = refs[:5]
    if features:
        prev0_ref, mu_ref, w0_ref, wlb_ref, a0_ref, alb_ref, kk_ref, ka_ref, rk_ref = refs[5:14]
        q_ref, k_ref, v_ref, g_ref, gr_ref = refs[14:19]
        feat_refs, ulast_ref, carry_scr = refs[19:26], refs[26], refs[27]
    else:
        q_ref, k_ref, v_ref, g_ref, gr_ref, u_ref = refs[5:11]
    if features:
        @pl.when(pl.program_id(1) == 0)
        def _():
            carry_scr[...] = prev0_ref[...]

    x = x_ref[...]
    xn = x * lax.rsqrt(jnp.mean(x * x, axis=-1, keepdims=True) + NORM_EPS) * nw_ref[...]
    h = (xn * (1.0 + scale_ref[...]) + shift_ref[...]).astype(BF16)
    u = _nn(h, w_ref[:, 4 * att_w:4 * att_w + shift_w])
    if features:
        rows = u.shape[0]
        shifted = pltpu.roll(u, 1, 0)
        row8 = lax.broadcasted_iota(jnp.int32, (8, 1), 0)
        u_prev = jnp.concatenate([jnp.where(row8 == 0, carry_scr[...], shifted[:8]), shifted[8:]], axis=0)
        carry_scr[...] = u[rows - 1:rows, :]
        ulast_ref[...] = u[rows - 1:rows, :]
        r, k2, v, logw, kk_raw, a = _rwkv_features(u, u_prev, mu_ref[...], w0_ref[...], wlb_ref[...], a0_ref[...],
                                                   alb_ref[...], kk_ref[...], ka_ref[...], rw)
        kk = kk_raw * lax.rsqrt(jnp.maximum(_head_sum(kk_raw * kk_raw), 1e-24))
        bonus = _head_sum(r * k2 * rk_ref[...]) * v
        for ref, val in zip(feat_refs, (r, k2, v, logw, kk, kk * a, bonus)):
            ref[...] = val
    else:
        u_ref[...] = u
    z = _nn(h, w_ref[:, :4 * att_w])
    npair = att_w // LANES
    for p in range(npair):
        q_ref[p] = z[:, p * LANES:(p + 1) * LANES] * (HEAD_DIM ** -0.5 * LOG2E)
        k_ref[p] = z[:, att_w + p * LANES:att_w + (p + 1) * LANES]
        v_ref[p] = z[:, 2 * att_w + p * LANES:2 * att_w + (p + 1) * LANES]
        g_ref[p] = z[:, 3 * att_w + p * LANES:3 * att_w + (p + 1) * LANES]
    gr_ref[...] = _nn(h, w_ref[:, 4 * att_w + shift_w:])


def _inproj(x, shift, scale, norm_w, w_in_bf16, att_w, shift_w, rwkv=None):
    b, t, d = x.shape
    in_w = w_in_bf16.shape[1]
    rw = in_w - 4 * att_w - shift_w
    npair = att_w // LANES
    tm = min(ROW_TILE, t)
    per_row = shift.shape[1] != 1
    mod_spec = (pl.BlockSpec((None, tm, d), lambda i, j: (i, j, 0)) if per_row
                else pl.BlockSpec((None, 1, d), lambda i, j: (i, 0, 0)))
    pair_spec = pl.BlockSpec((None, npair, tm, LANES), lambda i, j: (i, 0, j, 0))
    pair_shape = jax.ShapeDtypeStruct((b, npair, t, LANES), F32)
    row_spec = lambda w: pl.BlockSpec((None, tm, w), lambda i, j: (i, j, 0))
    row_shape = lambda w: jax.ShapeDtypeStruct((b, t, w), F32)
    args = [x, shift, scale, norm_w.reshape(1, d), w_in_bf16]
    in_specs = [row_spec(d), mod_spec, mod_spec, pl.BlockSpec((1, d), lambda i, j: (0, 0)),
                pl.BlockSpec((d, in_w), lambda i, j: (0, 0))]
    out_specs = [pair_spec] * 4 + [row_spec(rw)]
    out_shape = [pair_shape] * 4 + [row_shape(rw)]
    scratch = []
    if rwkv is None:
        out_specs.append(row_spec(shift_w))
        out_shape.append(row_shape(shift_w))
    else:
        p, prev0 = rwkv
        consts = [p[n] for n in ("mu", "w0", "wlb", "a0", "alb", "kk", "ka", "rk")]
        args += [prev0] + consts
        in_specs += [pl.BlockSpec((None, 1, shift_w), lambda i, j: (i, 0, 0))]
        in_specs += [pl.BlockSpec(c.shape, lambda i, j: (0, 0)) for c in consts]
        out_specs += [row_spec(rw)] * 7 + [pl.BlockSpec((None, 1, shift_w), lambda i, j: (i, 0, 0))]
        out_shape += [row_shape(rw)] * 7 + [jax.ShapeDtypeStruct((b, 1, shift_w), F32)]
        scratch = [pltpu.VMEM((1, shift_w), F32)]
    outs = pl.pallas_call(
        functools.partial(_inproj_kernel, att_w=att_w, shift_w=shift_w, rw=rw, features=rwkv is not None),
        grid=(b, t // tm),
        in_specs=in_specs,
        out_specs=out_specs,
        out_shape=out_shape,
        scratch_shapes=scratch,
        compiler_params=_cparams("arbitrary", "arbitrary"),
        name="in_proj",
    )(*args)
    if rwkv is None:
        return outs
    return list(outs[:5]) + [list(outs[5:12]), outs[12]]


def _t5_bucket_np(dist):
    max_exact = N_BUCKETS // 2
    nf = np.maximum(dist, max_exact).astype(np.float32)
    large = max_exact + (np.log(nf / np.float32(max_exact)) / np.float32(math.log(BUCKET_MAX_DIST / max_exact))
                         * np.float32(N_BUCKETS - max_exact)).astype(np.int32)
    large = np.minimum(large, N_BUCKETS - 1)
    return np.where(dist < max_exact, dist, large)


def _branch_bias(rel_bias):
    out = []
    for window, dil in DILATED_BRANCHES:
        dist = dil * np.arange(window // dil + 1, dtype=np.int32)
        out.append(rel_bias[_t5_bucket_np(dist)].T.astype(F32) * LOG2E)
    return jnp.stack(out)


def _prompt_bias_tiles(rel_bias):
    bias = _branch_bias(rel_bias)
    nb, nh, _ = bias.shape
    blk, width = ATT_BLOCK, 2 * ATT_BLOCK
    pad = jnp.full((nb, nh, blk - 1), -jnp.inf, F32)
    strip = jnp.flip(jnp.concatenate([pad, bias, pad], axis=-1), axis=-1)
    length = 3 * blk - 1
    rows = jnp.broadcast_to(jnp.pad(strip, ((0, 0), (0, 0), (0, 1)))[:, :, None, :], (nb, nh, blk, length + 1))
    skew = rows.reshape(nb, nh, blk * (length + 1))[:, :, :blk * length].reshape(nb, nh, blk, length)
    tile = skew[:, :, :, blk - 1:blk - 1 + width]
    first = jnp.where((np.arange(width) >= blk)[None, None, None, :], tile, -jnp.inf)
    return jnp.stack([tile, first], axis=1)


def _att_prompt_kernel(q_ref, k_ref, v_ref, bias_ref, o_ref, m_ref, l_ref, acc_ref, *, super_rows):
    sb = pl.program_id(2)
    lane = lax.broadcasted_iota(jnp.int32, (1, LANES), 1)
    head0 = lane < HEAD_DIM
    ones_b = jnp.ones((2 * ATT_BLOCK, LANES), BF16)

    for bi, (_, dil) in enumerate(DILATED_BRANCHES):
        nblk = super_rows // (ATT_BLOCK * dil)
        group = ATT_TILES_PER_ITER[dil]
        rows = lambda s, dil=dil: (pl.ds(s, ATT_BLOCK, stride=dil) if dil > 1 else pl.ds(s, ATT_BLOCK))

        def tiles(it, carry, bi=bi, dil=dil, nblk=nblk, rows=rows, group=group):
            locs, firsts, q2, kcat, vcat = [], [], [], [], []
            for j in range(group):
                idx = it * group + j
                res = idx // nblk
                loc = res + dil * ATT_BLOCK * (idx - res * nblk)
                glob = sb * super_rows + loc
                prev = glob - dil * ATT_BLOCK
                first = prev < 0
                pstart = jnp.where(first, glob, prev)
                qt = q_ref[rows(loc), :]
                locs.append(loc)
                firsts.append(first.astype(jnp.int32))
                q2.append(jnp.concatenate([jnp.where(head0, qt, 0.0), jnp.where(head0, 0.0, qt)],
                                          axis=0).astype(BF16))
                kcat.append(jnp.concatenate([k_ref[rows(pstart), :], k_ref[rows(glob), :]], axis=0).astype(BF16))
                vcat.append(jnp.concatenate([v_ref[rows(pstart), :], v_ref[rows(glob), :]], axis=0).astype(BF16))
            s = [_nt(q2[j], kcat[j]) + bias_ref[bi, firsts[j]].reshape(2 * ATT_BLOCK, 2 * ATT_BLOCK)
                 for j in range(group)]
            mt = [jnp.max(x, axis=-1, keepdims=True) for x in s]
            p = [jnp.exp2(s[j] - mt[j]).astype(BF16) for j in range(group)]
            pv = [_nn(p[j], jnp.concatenate([vcat[j], ones_b], axis=1)) for j in range(group)]
            for j in range(group):
                r = rows(locs[j])
                m_ref[bi, r, :] = jnp.where(head0, mt[j][:ATT_BLOCK], mt[j][ATT_BLOCK:])
                l_ref[bi, r, :] = jnp.where(head0, pv[j][:ATT_BLOCK, LANES:], pv[j][ATT_BLOCK:, LANES:])
                acc_ref[bi, r, :] = jnp.where(head0, pv[j][:ATT_BLOCK, :LANES], pv[j][ATT_BLOCK:, :LANES])
            return carry

        lax.fori_loop(0, dil * nblk // group, tiles, 0)

    nb = len(DILATED_BRANCHES)
    chunk = 2 * ATT_BLOCK

    def merge(c, carry):
        r = pl.ds(pl.multiple_of(c * chunk, chunk), chunk)
        ms = [m_ref[n, r, :] for n in range(nb)]
        m = functools.reduce(jnp.maximum, ms)
        ws = [jnp.exp2(x - m) for x in ms]
        den = functools.reduce(jnp.add, [l_ref[n, r, :] * ws[n] for n in range(nb)])
        num = functools.reduce(jnp.add, [acc_ref[n, r, :] * ws[n] for n in range(nb)])
        o_ref[r, :] = num / den
        return carry

    lax.fori_loop(0, super_rows // chunk, merge, 0)


def _att_prompt(q, k, v, bias_tiles):
    b, npair, t, _ = q.shape
    sr = ATT_SUPER
    nb = len(DILATED_BRANCHES)
    return pl.pallas_call(
        functools.partial(_att_prompt_kernel, super_rows=sr),
        grid=(b, npair, t // sr),
        in_specs=[pl.BlockSpec((None, None, sr, LANES), lambda i, p, s: (i, p, s, 0)),
                  pl.BlockSpec((None, None, t, LANES), lambda i, p, s: (i, p, 0, 0)),
                  pl.BlockSpec((None, None, t, LANES), lambda i, p, s: (i, p, 0, 0)),
                  pl.BlockSpec((nb, 2, 2, ATT_BLOCK, 2 * ATT_BLOCK), lambda i, p, s: (0, 0, p, 0, 0))],
        out_specs=pl.BlockSpec((None, None, sr, LANES), lambda i, p, s: (i, p, s, 0)),
        out_shape=jax.ShapeDtypeStruct((b, npair, t, LANES), F32),
        scratch_shapes=[pltpu.VMEM((nb, sr, LANES), F32)] * 3,
        compiler_params=_cparams("arbitrary", "arbitrary", "arbitrary"),
        name="att_prompt",
    )(q, k, v, bias_tiles)


def _att_sample_kernel(q_ref, kn_ref, vn_ref, kt_ref, vt_ref, bias_ref, b0_ref, o_ref):
    nh, hd, l = kt_ref.shape
    nb = bias_ref.shape[0]
    q = q_ref[...]
    qb = q.astype(BF16)
    head = lax.broadcasted_iota(jnp.int32, (nh, 1), 0)
    logit = jnp.zeros((nh, l), F32)
    for h in range(nh):
        s = _nn(qb, kt_ref[h].astype(BF16))
        logit = jnp.where(head == h, s, logit)
    s0 = jnp.sum(q * kn_ref[...], axis=-1, keepdims=True)
    ls = [logit + bias_ref[n] for n in range(nb)]
    s_self = [s0 + b0_ref[n] for n in range(nb)]
    m = functools.reduce(jnp.maximum, [jnp.max(x, axis=-1, keepdims=True) for x in ls] + s_self)
    p = functools.reduce(jnp.add, [jnp.exp2(x - m) for x in ls])
    p0 = functools.reduce(jnp.add, [jnp.exp2(x - m) for x in s_self])
    den = jnp.sum(p, axis=-1, keepdims=True) + p0
    pb = p.astype(BF16)
    out = p0 * vn_ref[...]
    for h in range(nh):
        out = out + jnp.where(head == h, _nt(pb, vt_ref[h].astype(BF16)), 0.0)
    o_ref[...] = out / den


def _att_sample_job(q, k_new, v_new, cache_kt, cache_vt, rel_bias, seq_of_step):
    b, nh, hd = q.shape
    l = cache_kt.shape[-1]
    bias = _branch_bias(rel_bias)
    back = l - np.arange(l)
    tabs = []
    for n, (window, dil) in enumerate(DILATED_BRANCHES):
        used = (back % dil == 0) & (back <= window)
        tabs.append(jnp.where(used[None, :], bias[n][:, np.where(used, back // dil, 0)], -jnp.inf))
    bias_pos = jnp.stack(tabs)
    bias0 = bias[:, :, 0][..., None]
    row = pl.BlockSpec((None, nh, hd), lambda *g: (seq_of_step(*g), 0, 0))
    cache = pl.BlockSpec((None, nh, hd, l), lambda *g: (seq_of_step(*g), 0, 0, 0))
    full = lambda a: pl.BlockSpec(a.shape, lambda *g: (0,) * a.ndim)
    return dict(body=_att_sample_kernel, args=[q, k_new, v_new, cache_kt, cache_vt, bias_pos, bias0],
                in_specs=[row, row, row, cache, cache, full(bias_pos), full(bias0)],
                out_specs=[row], out_shape=[jax.ShapeDtypeStruct((b, nh, hd), F32)])


def _att_sample(q, k_new, v_new, cache_kt, cache_vt, rel_bias):
    job = _att_sample_job(q, k_new, v_new, cache_kt, cache_vt, rel_bias, lambda i: i)
    return pl.pallas_call(
        job["body"],
        grid=(q.shape[0],),
        in_specs=job["in_specs"],
        out_specs=job["out_specs"][0],
        out_shape=job["out_shape"][0],
        compiler_params=_cparams("arbitrary"),
        name="att_sample",
    )(*job["args"])


def _gated_outproj(x_ref, att_ref, g_ref, rwk, gr_ref, gate_ref, w_ref, fnw_ref, final_norm):
    npair = att_ref.shape[0]
    mixed = [(att_ref[p] * _silu(g_ref[p])).astype(BF16) for p in range(npair)]
    mixed.append((rwk * _silu(gr_ref[...])).astype(BF16))
    acc = _nn(jnp.concatenate(mixed, axis=1), w_ref[...])
    xo = x_ref[...] + gate_ref[...] * acc
    if final_norm:
        xo = xo * lax.rsqrt(jnp.mean(xo * xo, axis=-1, keepdims=True) + NORM_EPS) * fnw_ref[...]
    return xo


def _head_sum(x):
    left = lax.broadcasted_iota(jnp.int32, (1, LANES), 1) < HEAD_DIM
    tiles = []
    for q in range(x.shape[1] // LANES):
        t = x[:, q * LANES:(q + 1) * LANES]
        s_left = jnp.sum(jnp.where(left, t, 0.0), axis=-1, keepdims=True)
        s_right = jnp.sum(jnp.where(left, 0.0, t), axis=-1, keepdims=True)
        tiles.append(jnp.where(left, s_left, s_right))
    return jnp.concatenate(tiles, axis=1)


def _rwkv_features(u, u_prev, mu, w0, wlb, a0, alb, kk_scale, ka, rw):
    um = u + (u_prev - u) * mu
    r = um[:, :rw]
    k = um[:, rw:2 * rw]
    v = um[:, 2 * rw:3 * rw]
    xw = um[:, 3 * rw:3 * rw + LORA_W]
    xa = um[:, 3 * rw + LORA_W:3 * rw + 2 * LORA_W]
    wl = w0 + _nn(jnp.tanh(xw).astype(BF16), wlb)
    logw = -math.exp(-0.5) * jax.nn.sigmoid(wl)
    a = jax.nn.sigmoid(a0 + _nn(xa.astype(BF16), alb))
    kk_raw = k * kk_scale
    k2 = k * (1.0 + (a - 1.0) * ka)
    return r, k2, v, logw, kk_raw, a


def _rwkv_prompt_kernel(r_ref, k2_ref, v_ref, logw_ref, kkn_ref, beta_ref, bonus_ref, s0_ref, lw_ref, lb_ref,
                        y_ref, sout_ref, s_scr, *, rw, rider=None, finish=None):
    c = pl.program_id(1)
    nc = pl.num_programs(1)
    rows = r_ref.shape[0]
    ch = RWKV_CHUNK
    ng = rows // ch
    nh = rw // HEAD_DIM
    hd = HEAD_DIM
    ntile = rw // LANES

    @pl.when(c == 0)
    def _():
        s_scr[...] = jnp.zeros(s_scr.shape, F32)
        for h in range(nh):
            s_scr[h, :, (h % 2) * hd:(h % 2 + 1) * hd] = s0_ref[h]

    if rider is not None:
        rider()

    r, k2, v, logw, kk, beta = r_ref[...], k2_ref[...], v_ref[...], logw_ref[...], kkn_ref[...], beta_ref[...]
    left = lax.broadcasted_iota(jnp.int32, (1, LANES), 1) < hd
    v_x = pltpu.roll(v, hd, 1)

    ti = lax.broadcasted_iota(jnp.int32, (ch, ch), 0)
    si = lax.broadcasted_iota(jnp.int32, (ch, ch), 1)
    tri_b = (ti >= si).astype(BF16)
    row_c = lax.broadcasted_iota(jnp.int32, (ch, 1), 0)
    t2 = lax.broadcasted_iota(jnp.int32, (2 * ch, 2 * ch), 0)
    s2 = lax.broadcasted_iota(jnp.int32, (2 * ch, 2 * ch), 1)
    tt = jnp.where(t2 >= ch, t2 - ch, t2)
    ss = jnp.where(s2 >= ch, s2 - ch, s2)
    score_mask = (tt - ss) >= jnp.where(t2 >= ch, 0, 1)
    tcol = lax.broadcasted_iota(jnp.int32, (ch, LANES), 1)
    trow = lax.broadcasted_iota(jnp.int32, (ch, LANES), 0)
    eye_right = (tcol == trow + ch).astype(F32)
    zeros_b = jnp.zeros((ch, LANES), BF16)
    own_of = [left if h % 2 == 0 else jnp.logical_not(left) for h in range(nh)]
    tile_of = [slice((h // 2) * LANES, (h // 2 + 1) * LANES) for h in range(nh)]
    xtile_of = [slice((((h + 1) // 2) % ntile) * LANES, (((h + 1) // 2) % ntile + 1) * LANES) for h in range(nh)]

    nu = ng * nh
    lhs, rhs, bk, vx, at_b, rt_own, pc = [], [], [], [], [], [], []
    for g in range(ng):
        rs = slice(g * ch, (g + 1) * ch)
        l1 = logw[rs].astype(BF16)
        rem = logw[rs] - l1.astype(F32)
        l2 = rem.astype(BF16)
        l3 = (rem - l2.astype(F32)).astype(BF16)
        cum3 = _nn(tri_b, jnp.concatenate([l1, l2, l3], axis=1))
        cum = cum3[:, :rw] + (cum3[:, rw:2 * rw] + cum3[:, 2 * rw:])
        ctot = cum[ch - 1:ch, :]
        pc_g = jnp.exp(ctot)
        e_cur = jnp.exp(cum)
        e_neg = jnp.exp(-cum)
        e_prev = jnp.where(row_c == 0, 1.0, pltpu.roll(e_cur, 1, 0))
        e_end = pc_g * e_neg
        rt_f = r[rs] * e_cur
        at_g = (-kk[rs] * e_prev).astype(BF16)
        rt_g = rt_f.astype(BF16)
        bt_g = (beta[rs] * e_neg).astype(BF16)
        kt_g = (k2[rs] * e_neg).astype(BF16)
        bh_g = (beta[rs] * e_end).astype(BF16)
        kh_g = (k2[rs] * e_end).astype(BF16)
        vx_g = v_x[rs].astype(BF16)
        for h in range(nh):
            own, tl = own_of[h], tile_of[h]
            zb = jnp.zeros((), BF16)
            lhs.append(jnp.concatenate([jnp.where(own, at_g[:, tl], zb), jnp.where(own, rt_g[:, tl], zb)], axis=0))
            rhs.append(jnp.concatenate([bt_g[:, tl], kt_g[:, tl]], axis=0))
            bk.append(jnp.concatenate([jnp.where(own, bh_g[:, tl], zb), jnp.where(own, kh_g[:, tl], zb)], axis=0))
            vx.append(jnp.where(own, zb, vx_g[:, xtile_of[h]]))
            at_b.append(at_g[:, tl])
            rt_own.append(jnp.where(own, rt_f[:, tl], 0.0))
            pc.append(pc_g[:, tl])

    units = range(nu)
    own_u = [own_of[i % nh] for i in units]
    sc = [jnp.where(score_mask, _nt(lhs[i], rhs[i]), 0.0) for i in units]
    top_b = [sc[i][:ch].astype(BF16) for i in units]
    bot_b = [sc[i][ch:].astype(BF16) for i in units]
    aakv = [_nn(top_b[i], jnp.concatenate([zeros_b, vx[i]], axis=0)) for i in units]
    x = [jnp.where(left, sc[i][:ch], eye_right) for i in units]
    npow = 1
    while npow < ch:
        xb = [x[i].astype(BF16) for i in units]
        x = [_nn(xb[i], jnp.concatenate([xb[i], zeros_b], axis=0)) + jnp.where(left, 0.0, x[i]) for i in units]
        npow *= 2
    z_b = [jnp.where(own_u[i], at_b[i], aakv[i].astype(BF16)) for i in units]
    wu = [_nn(x[i].astype(BF16), jnp.concatenate([zeros_b, z_b[i]], axis=0)) for i in units]
    rhs2 = [jnp.concatenate([wu[i].astype(BF16), vx[i]], axis=0) for i in units]
    qy = [_nn(bot_b[i], rhs2[i]) for i in units]
    pg = [_tn(rhs2[i], bk[i]) for i in units]

    s_cur = [s_scr[h] for h in range(nh)]
    y_sw = []
    for g in range(ng):
        y_u = []
        for h in range(nh):
            i = g * nh + h
            own = own_of[h]
            s_b = s_cur[h].astype(BF16)
            s_pad = jnp.concatenate([zeros_b, s_b] if h % 2 == 0 else [s_b, zeros_b], axis=0)
            qhat_b = jnp.where(own, rt_own[i] + qy[i], 0.0).astype(BF16)
            y_u.append(jnp.where(own, 0.0, qy[i] + _nt(qhat_b, s_pad)))
            g0 = pg[i][ch:] if h % 2 == 0 else pg[i][:ch]
            s_cur[h] = s_cur[h] * pc[i] + _nn(s_b, pg[i].astype(BF16)) + g0
        y_sw.append(jnp.concatenate([y_u[(2 * q - 1) % nh] + y_u[2 * q] for q in range(ntile)], axis=1))
    for h in range(nh):
        s_scr[h] = s_cur[h]
    y = pltpu.roll(jnp.concatenate(y_sw, axis=0), rw - hd, 1)
    mean = _head_sum(y) * (1.0 / hd)
    dev = y - mean
    var = _head_sum(dev * dev) * (1.0 / hd)
    y = dev * lax.rsqrt(var + GN_EPS) * lw_ref[...] + lb_ref[...] + bonus_ref[...]
    y_ref[...] = y if finish is None else finish(y)

    @pl.when(c == nc - 1)
    def _():
        for h in range(nh):
            sout_ref[h] = s_scr[h, :, (h % 2) * hd:(h % 2 + 1) * hd]


def _rwkv_fused_kernel(*refs, rw, n_in, rider_body, rider_in, rider_out, post_in, final_norm):
    own_in, rest = refs[:n_in], refs[n_in:]
    r_in, rest = rest[:rider_in], rest[rider_in:]
    p_in, rest = rest[:post_in], rest[post_in:]
    own_out, rest = rest[:2], rest[2:]
    r_out, scratch = rest[:rider_out], rest[rider_out:]
    rider = (lambda: rider_body(*r_in, *r_out)) if rider_body is not None else None
    finish = None
    if post_in:
        x_ref, att_ref, g_ref, gr_ref, gate_ref, w_ref, fnw_ref = p_in
        finish = lambda y: _gated_outproj(x_ref, att_ref, g_ref, y, gr_ref, gate_ref, w_ref, fnw_ref, final_norm)
    _rwkv_prompt_kernel(*own_in, *own_out, *scratch, rw=rw, rider=rider, finish=finish)


def _rwkv_prompt(feats, s0, p, rider=None, post=None):
    b, t, rw = feats[0].shape
    nh = rw // HEAD_DIM
    hd = HEAD_DIM
    rows = RWKV_CHUNK * RWKV_CHUNKS_PER_STEP
    vec = lambda a: pl.BlockSpec(a.shape, lambda i, c: (0,) * a.ndim)
    state_spec = pl.BlockSpec((None, nh, hd, hd), lambda i, c: (i, 0, 0, 0))
    args = list(feats) + [s0, p["lw"], p["lb"]]
    in_specs = ([pl.BlockSpec((None, rows, rw), lambda i, c: (i, c, 0))] * len(feats)
                + [state_spec, vec(p["lw"]), vec(p["lb"])])
    n_in = len(args)
    out_w = rw
    rider_args, rider_in_specs, rider_out_specs, rider_out_shape, rider_body = [], [], [], [], None
    if rider is not None:
        rider_args, rider_in_specs = rider["args"], rider["in_specs"]
        rider_out_specs, rider_out_shape, rider_body = rider["out_specs"], rider["out_shape"], rider["body"]
    post_args, post_specs, final_norm = [], [], False
    if post is not None:
        x, att, g_att, g_rwkv, gate, w_out_bf16, final_norm_w, final_norm = post
        d = x.shape[-1]
        npair = att.shape[1]
        out_w = d
        pair_spec = pl.BlockSpec((None, npair, rows, LANES), lambda i, c: (i, 0, c, 0))
        post_args = [x, att, g_att, g_rwkv, gate, w_out_bf16, final_norm_w.reshape(1, d)]
        post_specs = [pl.BlockSpec((None, rows, d), lambda i, c: (i, c, 0)), pair_spec, pair_spec,
                      pl.BlockSpec((None, rows, rw), lambda i, c: (i, c, 0)),
                      pl.BlockSpec((None, 1, d), lambda i, c: (i, 0, 0)),
                      vec(w_out_bf16), pl.BlockSpec((1, d), lambda i, c: (0, 0))]
    body = functools.partial(_rwkv_fused_kernel, rw=rw, n_in=n_in, rider_body=rider_body,
                             rider_in=len(rider_args), rider_out=len(rider_out_specs),
                             post_in=len(post_args), final_norm=final_norm)
    return pl.pallas_call(
        body,
        grid=(b, t // rows),
        in_specs=in_specs + rider_in_specs + post_specs,
        out_specs=[pl.BlockSpec((None, rows, out_w), lambda i, c: (i, c, 0)), state_spec] + rider_out_specs,
        out_shape=[jax.ShapeDtypeStruct((b, t, out_w), F32), jax.ShapeDtypeStruct((b, nh, hd, hd), F32)]
        + rider_out_shape,
        scratch_shapes=[pltpu.VMEM((nh, hd, LANES), F32)],
        compiler_params=_cparams("arbitrary", "arbitrary"),
        name="rwkv_prompt",
    )(*args, *rider_args, *post_args)


def _rwkv_feat_kernel(u_ref, prev_ref, mu_ref, w0_ref, wlb_ref, a0_ref, alb_ref, kk_ref, ka_ref,
                      r_ref, k_ref, v_ref, w_ref, kkr_ref, a_ref, *, rw):
    r, k2, v, logw, kk_raw, a = _rwkv_features(u_ref[...], prev_ref[...], mu_ref[...], w0_ref[...], wlb_ref[...],
                                               a0_ref[...], alb_ref[...], kk_ref[...], ka_ref[...], rw)
    r_ref[...] = r.T
    k_ref[...] = k2.T
    v_ref[...] = v.T
    w_ref[...] = jnp.exp(logw).T
    kkr_ref[...] = kk_raw.T
    a_ref[...] = a.T


def _rwkv_step_kernel(s_ref, r_ref, k_ref, v_ref, w_ref, kkr_ref, a_ref, rk_ref, lw_ref, lb_ref,
                      y_ref, sout_ref, y_scr):
    hd = s_ref.shape[0]
    kk_raw = kkr_ref[...]
    kk = kk_raw / jnp.maximum(jnp.sqrt(jnp.sum(kk_raw * kk_raw, axis=0, keepdims=True)), 1e-12)
    beta = kk * a_ref[...]
    w, k2, r, v = w_ref[...], k_ref[...], r_ref[...], v_ref[...]

    def value_row(i, carry):
        s = s_ref[i]
        sa = jnp.sum(s * kk, axis=0, keepdims=True)
        s_new = s * w - sa * beta + v_ref[pl.ds(i, 1), :] * k2
        sout_ref[i] = s_new
        y_scr[pl.ds(i, 1), :] = jnp.sum(s_new * r, axis=0, keepdims=True)
        return carry

    lax.fori_loop(0, hd, value_row, 0, unroll=8)
    y = y_scr[...]
    mean = jnp.mean(y, axis=0, keepdims=True)
    var = jnp.mean(jnp.square(y - mean), axis=0, keepdims=True)
    yn = (y - mean) * lax.rsqrt(var + GN_EPS) * lw_ref[...] + lb_ref[...]
    y_ref[...] = yn + jnp.sum(r * k2 * rk_ref[...], axis=0, keepdims=True) * v


def _rwkv_sample(u, prev, s0_t, p):
    b, sw = u.shape
    rw = p["w0"].shape[-1]
    nh = rw // HEAD_DIM
    hd = HEAD_DIM
    full = lambda a: pl.BlockSpec(a.shape, lambda i: (0,) * a.ndim)
    fparams = [p["mu"], p["w0"], p["wlb"], p["a0"], p["alb"], p["kk"], p["ka"]]
    feats = pl.pallas_call(
        functools.partial(_rwkv_feat_kernel, rw=rw),
        grid=(1,),
        in_specs=[full(u), full(prev)] + [full(a) for a in fparams],
        out_specs=[pl.BlockSpec((rw, b), lambda i: (0, 0))] * 6,
        out_shape=[jax.ShapeDtypeStruct((rw, b), F32)] * 6,
        compiler_params=_cparams("arbitrary"),
        name="rwkv_sample_features",
    )(u, prev, *fparams)
    chan = pl.BlockSpec((hd, b), lambda h: (h, 0))
    col = pl.BlockSpec((hd, 1), lambda h: (h, 0))
    state_spec = pl.BlockSpec((None, hd, hd, b), lambda h: (h, 0, 0, 0))
    cols = [p[n].reshape(rw, 1) for n in ("rk", "lw", "lb")]
    y_t, s_new = pl.pallas_call(
        _rwkv_step_kernel,
        grid=(nh,),
        in_specs=[state_spec] + [chan] * 6 + [col] * 3,
        out_specs=[chan, state_spec],
        out_shape=[jax.ShapeDtypeStruct((rw, b), F32), jax.ShapeDtypeStruct((nh, hd, hd, b), F32)],
        scratch_shapes=[pltpu.VMEM((hd, b), F32)],
        compiler_params=_cparams("arbitrary"),
        name="rwkv_sample_step",
    )(s0_t, *feats, *cols)
    return y_t.T, s_new


def _outproj_kernel(x_ref, att_ref, g_ref, rw_ref, gr_ref, gate_ref, w_ref, fnw_ref, o_ref, *, final_norm):
    o_ref[...] = _gated_outproj(x_ref, att_ref, g_ref, rw_ref[...], gr_ref, gate_ref, w_ref, fnw_ref, final_norm)


def _outproj(x, att, g_att, rwk, g_rwkv, gate, w_out_bf16, final_norm_w, final_norm):
    b, t, d = x.shape
    npair = att.shape[1]
    rw = rwk.shape[-1]
    tm = min(ROW_TILE, t)
    per_row = gate.shape[1] != 1
    gate_spec = (pl.BlockSpec((None, tm, d), lambda i, j: (i, j, 0)) if per_row
                 else pl.BlockSpec((None, 1, d), lambda i, j: (i, 0, 0)))
    pair_spec = pl.BlockSpec((None, npair, tm, LANES), lambda i, j: (i, 0, j, 0))
    return pl.pallas_call(
        functools.partial(_outproj_kernel, final_norm=final_norm),
        grid=(b, t // tm),
        in_specs=[pl.BlockSpec((None, tm, d), lambda i, j: (i, j, 0)), pair_spec, pair_spec,
                  pl.BlockSpec((None, tm, rw), lambda i, j: (i, j, 0)),
                  pl.BlockSpec((None, tm, rw), lambda i, j: (i, j, 0)),
                  gate_spec,
                  pl.BlockSpec(w_out_bf16.shape, lambda i, j: (0, 0)),
                  pl.BlockSpec((1, d), lambda i, j: (0, 0))],
        out_specs=pl.BlockSpec((None, tm, d), lambda i, j: (i, j, 0)),
        out_shape=jax.ShapeDtypeStruct((b, t, d), F32),
        compiler_params=_cparams("arbitrary", "arbitrary"),
        name="out_proj",
    )(x, att, g_att, rwk, g_rwkv, gate, w_out_bf16, final_norm_w.reshape(1, d))


def _pairs_to_heads(a):
    b, npair, t, _ = a.shape
    return a.transpose(0, 2, 1, 3).reshape(b, t, npair * (LANES // HEAD_DIM), HEAD_DIM)


def kernel(x_prompt, x_sample, cache_win_k, cache_win_v, state_wkv, state_shift, c_prompt, c_sample, rel_bias, norm_w, ada_w, ada_b, w_in, mu_shift, w0, w_lora_b, a0, a_lora_b, k_k, k_a, r_k, ln_x_w, ln_x_b, w_out, final_norm_w):
    depth = norm_w.shape[0]
    bp, tp, d = x_prompt.shape
    bs, ts, _ = x_sample.shape
    assert ts == 1, "the sample group decodes one token per sequence"
    rw = w0.shape[-1]
    att_w = w_out.shape[1] - rw
    shift_w = mu_shift.shape[-1]
    nh_att = att_w // HEAD_DIM
    nh_rw = rw // HEAD_DIM
    assert tp % ATT_SUPER == 0 and tp % (RWKV_CHUNK * RWKV_CHUNKS_PER_STEP) == 0 and bs % 8 == 0
    assert cache_win_k.shape[2] == MAX_WINDOW
    keep = min(MAX_WINDOW, tp)

    bias_tiles = _prompt_bias_tiles(rel_bias)
    c_all = jnp.concatenate([c_prompt, c_sample], axis=0)
    npad = -c_all.shape[0] % 8
    c_all = jnp.pad(c_all, ((0, npad), (0, 0)))

    xp = x_prompt
    xs = x_sample.reshape(1, bs, d)
    outs = [[] for _ in range(8)]
    for l in range(depth):
        mod = _ada_mod(c_all, ada_w[l], ada_b[l])
        shift, scale, gate = jnp.split(mod, 3, axis=-1)
        w_in_b = w_in[l].astype(BF16)
        w_out_b = w_out[l].astype(BF16)
        p = dict(mu=mu_shift[l].reshape(1, -1), w0=w0[l].reshape(1, -1), wlb=w_lora_b[l].astype(BF16),
                 a0=a0[l].reshape(1, -1), alb=a_lora_b[l].astype(BF16), kk=k_k[l].reshape(1, -1),
                 ka=k_a[l].reshape(1, -1), rk=r_k[l].reshape(1, -1), lw=ln_x_w[l].reshape(1, -1),
                 lb=ln_x_b[l].reshape(1, -1))
        last = l == depth - 1

        sm = lambda a: a[bp:bp + bs].reshape(1, bs, d)
        q_s, k_s, v_s, g_att_s, g_rwkv_s, u_s = _inproj(xs, sm(shift), sm(scale), norm_w[l], w_in_b, att_w, shift_w)
        nat = lambda a: a[0].transpose(1, 0, 2).reshape(bs, nh_att, HEAD_DIM)
        k_new, v_new = nat(k_s), nat(v_s)
        att_s_args = (nat(q_s), k_new, v_new, cache_win_k[l].transpose(0, 2, 3, 1),
                      cache_win_v[l].transpose(0, 2, 3, 1), rel_bias)

        pm = lambda a: a[:bp].reshape(bp, 1, d)
        prev0 = jnp.zeros((bp, 1, shift_w), F32)
        q, k, v, g_att, g_rwkv, feats, u_last = _inproj(xp, pm(shift), pm(scale), norm_w[l], w_in_b, att_w, shift_w,
                                                        rwkv=(p, prev0))
        att = _att_prompt(q, k, v, bias_tiles)
        s0 = jnp.zeros((bp, nh_rw, HEAD_DIM, HEAD_DIM), F32)
        steps = tp // (RWKV_CHUNK * RWKV_CHUNKS_PER_STEP)
        post = (xp, att, g_att, g_rwkv, pm(gate), w_out_b, final_norm_w, last)
        if bs == bp * steps:
            rider = _att_sample_job(*att_s_args, lambda i, c: i * steps + c)
            xp, s_p, att_s = _rwkv_prompt(feats, s0, p, rider, post)
        else:
            xp, s_p = _rwkv_prompt(feats, s0, p, None, post)
            att_s = _att_sample(*att_s_args)
        outs[0].append(_pairs_to_heads(k[:, :, tp - keep:, :]))
        outs[1].append(_pairs_to_heads(v[:, :, tp - keep:, :]))
        outs[4].append(s_p)
        outs[6].append(u_last[:, 0])

        att_s = att_s.reshape(bs, att_w // LANES, LANES).transpose(1, 0, 2)[None]
        y_rw_s, s_s = _rwkv_sample(u_s[0], state_shift[l], state_wkv[l].transpose(1, 2, 3, 0), p)
        xs = _outproj(xs, att_s, g_att_s, y_rw_s[None], g_rwkv_s, sm(gate), w_out_b, final_norm_w, last)
        outs[2].append(k_new.reshape(bs, 1, nh_att, HEAD_DIM))
        outs[3].append(v_new.reshape(bs, 1, nh_att, HEAD_DIM))
        outs[5].append(s_s.transpose(3, 0, 1, 2))
        outs[7].append(u_s[0])

    stack = lambda i: jnp.stack(outs[i])
    return (xp, xs.reshape(bs, 1, d), stack(0), stack(1), stack(2), stack(3), stack(4), stack(5), stack(6),
            stack(7))
```

```python
import functools
import math

import numpy as np
import jax
import jax.numpy as jnp
from jax import lax
from jax.experimental import pallas as pl
from jax.experimental.pallas import tpu as pltpu

F32 = jnp.float32
BF16 = jnp.bfloat16
HIGHEST = lax.Precision.HIGHEST

HEAD_DIM = 64
DILATED_BRANCHES = ((128, 1), (512, 4), (2048, 16))
MAX_WINDOW = max(w for w, _ in DILATED_BRANCHES)
KEYS_PER_BRANCH = 128
N_BUCKETS = 32
BUCKET_MAX_DIST = MAX_WINDOW
LORA_W = 64
NORM_EPS = 1e-6
GN_EPS = HEAD_DIM * 1e-5
LOG2E = math.log2(math.e)

LANES = 128
VMEM_LIMIT_BYTES = 56 * 1024 * 1024

ROW_TILE = 256
ATT_BLOCK = 128
ATT_SUPER = MAX_WINDOW
ATT_TILES_PER_ITER = {1: 16, 4: 16, 16: 4}
RWKV_CHUNK = 64
RWKV_CHUNKS_PER_STEP = 4


def _cparams(*sem):
    return pltpu.CompilerParams(dimension_semantics=sem, vmem_limit_bytes=VMEM_LIMIT_BYTES)


def _silu(x):
    return x * jax.nn.sigmoid(x)


def _nt(a, b):
    return lax.dot_general(a, b, (((1,), (1,)), ((), ())), preferred_element_type=F32)


def _tn(a, b):
    return lax.dot_general(a, b, (((0,), (0,)), ((), ())), preferred_element_type=F32)


def _nn(a, b, precision=None):
    return jnp.dot(a, b, precision=precision, preferred_element_type=F32)


def _ada_kernel(c_ref, w_ref, b_ref, o_ref):
    s = _silu(c_ref[...])
    o_ref[...] = _nn(s.astype(BF16), w_ref[...].astype(BF16)) + b_ref[...]


def _ada_mod(c_all, ada_w, ada_b):
    n, d = c_all.shape
    e = ada_w.shape[1]
    tn = 512
    return pl.pallas_call(
        _ada_kernel,
        grid=(e // tn,),
        in_specs=[pl.BlockSpec((n, d), lambda j: (0, 0)),
                  pl.BlockSpec((d, tn), lambda j: (0, j)),
                  pl.BlockSpec((1, tn), lambda j: (0, j))],
        out_specs=pl.BlockSpec((n, tn), lambda j: (0, j)),
        out_shape=jax.ShapeDtypeStruct((n, e), F32),
        compiler_params=_cparams("arbitrary"),
        name="ada_mod",
    )(c_all, ada_w, ada_b.reshape(1, e))


def _inproj_kernel(*refs, att_w, shift_w, rw, features):
    x_ref, shift_ref, scale_ref, nw_ref, w_ref = refs[:5]
    if features:
        prev0_ref, mu_ref, w0_ref, wlb_ref, a0_ref, alb_ref, kk_ref, ka_ref, rk_ref = refs[5:14]
        q_ref, k_ref, v_ref, g_ref, gr_ref = refs[14:19]
        feat_refs, ulast_ref, carry_scr = refs[19:26], refs[26], refs[27]
    else:
        q_ref, k_ref, v_ref, g_ref, gr_ref, u_ref = refs[5:11]
    if features:
        @pl.when(pl.program_id(1) == 0)
        def _():
            carry_scr[...] = prev0_ref[...]

    x = x_ref[...]
    xn = x * lax.rsqrt(jnp.mean(x * x, axis=-1, keepdims=True) + NORM_EPS) * nw_ref[...]
    h = (xn * (1.0 + scale_ref[...]) + shift_ref[...]).astype(BF16)
    u = _nn(h, w_ref[:, 4 * att_w:4 * att_w + shift_w])
    if features:
        rows = u.shape[0]
        shifted = pltpu.roll(u, 1, 0)
        row8 = lax.broadcasted_iota(jnp.int32, (8, 1), 0)
        u_prev = jnp.concatenate([jnp.where(row8 == 0, carry_scr[...], shifted[:8]), shifted[8:]], axis=0)
        carry_scr[...] = u[rows - 1:rows, :]
        ulast_ref[...] = u[rows - 1:rows, :]
        r, k2, v, logw, kk_raw, a = _rwkv_features(u, u_prev, mu_ref[...], w0_ref[...], wlb_ref[...], a0_ref[...],
                                                   alb_ref[...], kk_ref[...], ka_ref[...], rw)
        kk = kk_raw * lax.rsqrt(jnp.maximum(_head_sum(kk_raw * kk_raw), 1e-24))
        bonus = _head_sum(r * k2 * rk_ref[...]) * v
        for ref, val in zip(feat_refs, (r, k2, v, logw, kk, kk * a, bonus)):
            ref[...] = val
    else:
        u_ref[...] = u
    z = _nn(h, w_ref[:, :4 * att_w])
    npair = att_w // LANES
    for p in range(npair):
        q_ref[p] = z[:, p * LANES:(p + 1) * LANES] * (HEAD_DIM ** -0.5 * LOG2E)
        k_ref[p] = z[:, att_w + p * LANES:att_w + (p + 1) * LANES]
        v_ref[p] = z[:, 2 * att_w + p * LANES:2 * att_w + (p + 1) * LANES]
        g_ref[p] = z[:, 3 * att_w + p * LANES:3 * att_w + (p + 1) * LANES]
    gr_ref[...] = _nn(h, w_ref[:, 4 * att_w + shift_w:])


def _inproj(x, shift, scale, norm_w, w_in_bf16, att_w, shift_w, rwkv=None):
    b, t, d = x.shape
    in_w = w_in_bf16.shape[1]
    rw = in_w - 4 * att_w - shift_w
    npair = att_w // LANES
    tm = min(ROW_TILE, t)
    per_row = shift.shape[1] != 1
    mod_spec = (pl.BlockSpec((None, tm, d), lambda i, j: (i, j, 0)) if per_row
                else pl.BlockSpec((None, 1, d), lambda i, j: (i, 0, 0)))
    pair_spec = pl.BlockSpec((None, npair, tm, LANES), lambda i, j: (i, 0, j, 0))
    pair_shape = jax.ShapeDtypeStruct((b, npair, t, LANES), F32)
    row_spec = lambda w: pl.BlockSpec((None, tm, w), lambda i, j: (i, j, 0))
    row_shape = lambda w: jax.ShapeDtypeStruct((b, t, w), F32)
    args = [x, shift, scale, norm_w.reshape(1, d), w_in_bf16]
    in_specs = [row_spec(d), mod_spec, mod_spec, pl.BlockSpec((1, d), lambda i, j: (0, 0)),
                pl.BlockSpec((d, in_w), lambda i, j: (0, 0))]
    out_specs = [pair_spec] * 4 + [row_spec(rw)]
    out_shape = [pair_shape] * 4 + [row_shape(rw)]
    scratch = []
    if rwkv is None:
        out_specs.append(row_spec(shift_w))
        out_shape.append(row_shape(shift_w))
    else:
        p, prev0 = rwkv
        consts = [p[n] for n in ("mu", "w0", "wlb", "a0", "alb", "kk", "ka", "rk")]
        args += [prev0] + consts
        in_specs += [pl.BlockSpec((None, 1, shift_w), lambda i, j: (i, 0, 0))]
        in_specs += [pl.BlockSpec(c.shape, lambda i, j: (0, 0)) for c in consts]
        out_specs += [row_spec(rw)] * 7 + [pl.BlockSpec((None, 1, shift_w), lambda i, j: (i, 0, 0))]
        out_shape += [row_shape(rw)] * 7 + [jax.ShapeDtypeStruct((b, 1, shift_w), F32)]
        scratch = [pltpu.VMEM((1, shift_w), F32)]
    outs = pl.pallas_call(
        functools.partial(_inproj_kernel, att_w=att_w, shift_w=shift_w, rw=rw, features=rwkv is not None),
        grid=(b, t // tm),
        in_specs=in_specs,
        out_specs=out_specs,
        out_shape=out_shape,
        scratch_shapes=scratch,
        compiler_params=_cparams("arbitrary", "arbitrary"),
        name="in_proj",
    )(*args)
    if rwkv is None:
        return outs
    return list(outs[:5]) + [list(outs[5:12]), outs[12]]


def _t5_bucket_np(dist):
    max_exact = N_BUCKETS // 2
    nf = np.maximum(dist, max_exact).astype(np.float32)
    large = max_exact + (np.log(nf / np.float32(max_exact)) / np.float32(math.log(BUCKET_MAX_DIST / max_exact))
                         * np.float32(N_BUCKETS - max_exact)).astype(np.int32)
    large = np.minimum(large, N_BUCKETS - 1)
    return np.where(dist < max_exact, dist, large)


def _branch_bias(rel_bias):
    out = []
    for window, dil in DILATED_BRANCHES:
        dist = dil * np.arange(window // dil + 1, dtype=np.int32)
        out.append(rel_bias[_t5_bucket_np(dist)].T.astype(F32) * LOG2E)
    return jnp.stack(out)


def _prompt_bias_tiles(rel_bias):
    bias = _branch_bias(rel_bias)
    nb, nh, _ = bias.shape
    blk, width = ATT_BLOCK, 2 * ATT_BLOCK
    pad = jnp.full((nb, nh, blk - 1), -jnp.inf, F32)
    strip = jnp.flip(jnp.concatenate([pad, bias, pad], axis=-1), axis=-1)
    length = 3 * blk - 1
    rows = jnp.broadcast_to(jnp.pad(strip, ((0, 0), (0, 0), (0, 1)))[:, :, None, :], (nb, nh, blk, length + 1))
    skew = rows.reshape(nb, nh, blk * (length + 1))[:, :, :blk * length].reshape(nb, nh, blk, length)
    tile = skew[:, :, :, blk - 1:blk - 1 + width]
    first = jnp.where((np.arange(width) >= blk)[None, None, None, :], tile, -jnp.inf)
    return jnp.stack([tile, first], axis=1)


def _att_prompt_kernel(q_ref, k_ref, v_ref, bias_ref, o_ref, m_ref, l_ref, acc_ref, *, super_rows):
    sb = pl.program_id(2)
    lane = lax.broadcasted_iota(jnp.int32, (1, LANES), 1)
    head0 = lane < HEAD_DIM
    ones_b = jnp.ones((2 * ATT_BLOCK, LANES), BF16)

    for bi, (_, dil) in enumerate(DILATED_BRANCHES):
        nblk = super_rows // (ATT_BLOCK * dil)
        group = ATT_TILES_PER_ITER[dil]
        rows = lambda s, dil=dil: (pl.ds(s, ATT_BLOCK, stride=dil) if dil > 1 else pl.ds(s, ATT_BLOCK))

        def tiles(it, carry, bi=bi, dil=dil, nblk=nblk, rows=rows, group=group):
            locs, firsts, q2, kcat, vcat = [], [], [], [], []
            for j in range(group):
                idx = it * group + j
                res = idx // nblk
                loc = res + dil * ATT_BLOCK * (idx - res * nblk)
                glob = sb * super_rows + loc
                prev = glob - dil * ATT_BLOCK
                first = prev < 0
                pstart = jnp.where(first, glob, prev)
                qt = q_ref[rows(loc), :]
                locs.append(loc)
                firsts.append(first.astype(jnp.int32))
                q2.append(jnp.concatenate([jnp.where(head0, qt, 0.0), jnp.where(head0, 0.0, qt)],
                                          axis=0).astype(BF16))
                kcat.append(jnp.concatenate([k_ref[rows(pstart), :], k_ref[rows(glob), :]], axis=0).astype(BF16))
                vcat.append(jnp.concatenate([v_ref[rows(pstart), :], v_ref[rows(glob), :]], axis=0).astype(BF16))
            s = [_nt(q2[j], kcat[j]) + bias_ref[bi, firsts[j]].reshape(2 * ATT_BLOCK, 2 * ATT_BLOCK)
                 for j in range(group)]
            mt = [jnp.max(x, axis=-1, keepdims=True) for x in s]
            p = [jnp.exp2(s[j] - mt[j]).astype(BF16) for j in range(group)]
            pv = [_nn(p[j], jnp.concatenate([vcat[j], ones_b], axis=1)) for j in range(group)]
            for j in range(group):
                r = rows(locs[j])
                m_ref[bi, r, :] = jnp.where(head0, mt[j][:ATT_BLOCK], mt[j][ATT_BLOCK:])
                l_ref[bi, r, :] = jnp.where(head0, pv[j][:ATT_BLOCK, LANES:], pv[j][ATT_BLOCK:, LANES:])
                acc_ref[bi, r, :] = jnp.where(head0, pv[j][:ATT_BLOCK, :LANES], pv[j][ATT_BLOCK:, :LANES])
            return carry

        lax.fori_loop(0, dil * nblk // group, tiles, 0)

    nb = len(DILATED_BRANCHES)
    chunk = 2 * ATT_BLOCK

    def merge(c, carry):
        r = pl.ds(pl.multiple_of(c * chunk, chunk), chunk)
        ms = [m_ref[n, r, :] for n in range(nb)]
        m = functools.reduce(jnp.maximum, ms)
        ws = [jnp.exp2(x - m) for x in ms]
        den = functools.reduce(jnp.add, [l_ref[n, r, :] * ws[n] for n in range(nb)])
        num = functools.reduce(jnp.add, [acc_ref[n, r, :] * ws[n] for n in range(nb)])
        o_ref[r, :] = num / den
        return carry

    lax.fori_loop(0, super_rows // chunk, merge, 0)


def _att_prompt(q, k, v, bias_tiles):
    b, npair, t, _ = q.shape
    sr = ATT_SUPER
    nb = len(DILATED_BRANCHES)
    return pl.pallas_call(
        functools.partial(_att_prompt_kernel, super_rows=sr),
        grid=(b, npair, t // sr),
        in_specs=[pl.BlockSpec((None, None, sr, LANES), lambda i, p, s: (i, p, s, 0)),
                  pl.BlockSpec((None, None, t, LANES), lambda i, p, s: (i, p, 0, 0)),
                  pl.BlockSpec((None, None, t, LANES), lambda i, p, s: (i, p, 0, 0)),
                  pl.BlockSpec((nb, 2, 2, ATT_BLOCK, 2 * ATT_BLOCK), lambda i, p, s: (0, 0, p, 0, 0))],
        out_specs=pl.BlockSpec((None, None, sr, LANES), lambda i, p, s: (i, p, s, 0)),
        out_shape=jax.ShapeDtypeStruct((b, npair, t, LANES), F32),
        scratch_shapes=[pltpu.VMEM((nb, sr, LANES), F32)] * 3,
        compiler_params=_cparams("arbitrary", "arbitrary", "arbitrary"),
        name="att_prompt",
    )(q, k, v, bias_tiles)


def _att_sample_kernel(q_ref, kn_ref, vn_ref, kt_ref, vt_ref, bias_ref, b0_ref, o_ref):
    nh, hd, l = kt_ref.shape
    nb = bias_ref.shape[0]
    q = q_ref[...]
    qb = q.astype(BF16)
    head = lax.broadcasted_iota(jnp.int32, (nh, 1), 0)
    logit = jnp.zeros((nh, l), F32)
    for h in range(nh):
        s = _nn(qb, kt_ref[h].astype(BF16))
        logit = jnp.where(head == h, s, logit)
    s0 = jnp.sum(q * kn_ref[...], axis=-1, keepdims=True)
    ls = [logit + bias_ref[n] for n in range(nb)]
    s_self = [s0 + b0_ref[n] for n in range(nb)]
    m = functools.reduce(jnp.maximum, [jnp.max(x, axis=-1, keepdims=True) for x in ls] + s_self)
    p = functools.reduce(jnp.add, [jnp.exp2(x - m) for x in ls])
    p0 = functools.reduce(jnp.add, [jnp.exp2(x - m) for x in s_self])
    den = jnp.sum(p, axis=-1, keepdims=True) + p0
    pb = p.astype(BF16)
    out = p0 * vn_ref[...]
    for h in range(nh):
        out = out + jnp.where(head == h, _nt(pb, vt_ref[h].astype(BF16)), 0.0)
    o_ref[...] = out / den


def _att_sample_job(q, k_new, v_new, cache_kt, cache_vt, rel_bias, seq_of_step):
    b, nh, hd = q.shape
    l = cache_kt.shape[-1]
    bias = _branch_bias(rel_bias)
    back = l - np.arange(l)
    tabs = []
    for n, (window, dil) in enumerate(DILATED_BRANCHES):
        used = (back % dil == 0) & (back <= window)
        tabs.append(jnp.where(used[None, :], bias[n][:, np.where(used, back // dil, 0)], -jnp.inf))
    bias_pos = jnp.stack(tabs)
    bias0 = bias[:, :, 0][..., None]
    row = pl.BlockSpec((None, nh, hd), lambda *g: (seq_of_step(*g), 0, 0))
    cache = pl.BlockSpec((None, nh, hd, l), lambda *g: (seq_of_step(*g), 0, 0, 0))
    full = lambda a: pl.BlockSpec(a.shape, lambda *g: (0,) * a.ndim)
    return dict(body=_att_sample_kernel, args=[q, k_new, v_new, cache_kt, cache_vt, bias_pos, bias0],
                in_specs=[row, row, row, cache, cache, full(bias_pos), full(bias0)],
                out_specs=[row], out_shape=[jax.ShapeDtypeStruct((b, nh, hd), F32)])


def _att_sample(q, k_new, v_new, cache_kt, cache_vt, rel_bias):
    job = _att_sample_job(q, k_new, v_new, cache_kt, cache_vt, rel_bias, lambda i: i)
    return pl.pallas_call(
        job["body"],
        grid=(q.shape[0],),
        in_specs=job["in_specs"],
        out_specs=job["out_specs"][0],
        out_shape=job["out_shape"][0],
        compiler_params=_cparams("arbitrary"),
        name="att_sample",
    )(*job["args"])


def _gated_outproj(x_ref, att_ref, g_ref, rwk, gr_ref, gate_ref, w_ref, fnw_ref, final_norm):
    npair = att_ref.shape[0]
    mixed = [(att_ref[p] * _silu(g_ref[p])).astype(BF16) for p in range(npair)]
    mixed.append((rwk * _silu(gr_ref[...])).astype(BF16))
    acc = _nn(jnp.concatenate(mixed, axis=1), w_ref[...])
    xo = x_ref[...] + gate_ref[...] * acc
    if final_norm:
        xo = xo * lax.rsqrt(jnp.mean(xo * xo, axis=-1, keepdims=True) + NORM_EPS) * fnw_ref[...]
    return xo


def _head_sum(x):
    left = lax.broadcasted_iota(jnp.int32, (1, LANES), 1) < HEAD_DIM
    tiles = []
    for q in range(x.shape[1] // LANES):
        t = x[:, q * LANES:(q + 1) * LANES]
        s_left = jnp.sum(jnp.where(left, t, 0.0), axis=-1, keepdims=True)
        s_right = jnp.sum(jnp.where(left, 0.0, t), axis=-1, keepdims=True)
        tiles.append(jnp.where(left, s_left, s_right))
    return jnp.concatenate(tiles, axis=1)


def _rwkv_features(u, u_prev, mu, w0, wlb, a0, alb, kk_scale, ka, rw):
    um = u + (u_prev - u) * mu
    r = um[:, :rw]
    k = um[:, rw:2 * rw]
    v = um[:, 2 * rw:3 * rw]
    xw = um[:, 3 * rw:3 * rw + LORA_W]
    xa = um[:, 3 * rw + LORA_W:3 * rw + 2 * LORA_W]
    wl = w0 + _nn(jnp.tanh(xw).astype(BF16), wlb)
    logw = -math.exp(-0.5) * jax.nn.sigmoid(wl)
    a = jax.nn.sigmoid(a0 + _nn(xa.astype(BF16), alb))
    kk_raw = k * kk_scale
    k2 = k * (1.0 + (a - 1.0) * ka)
    return r, k2, v, logw, kk_raw, a


def _rwkv_prompt_kernel(r_ref, k2_ref, v_ref, logw_ref, kkn_ref, beta_ref, bonus_ref, s0_ref, lw_ref, lb_ref,
                        y_ref, sout_ref, s_scr, *, rw, rider=None, finish=None):
    c = pl.program_id(1)
    nc = pl.num_programs(1)
    rows = r_ref.shape[0]
    ch = RWKV_CHUNK
    ng = rows // ch
    nh = rw // HEAD_DIM
    hd = HEAD_DIM
    ntile = rw // LANES

    @pl.when(c == 0)
    def _():
        s_scr[...] = jnp.zeros(s_scr.shape, F32)
        for h in range(nh):
            s_scr[h, :, (h % 2) * hd:(h % 2 + 1) * hd] = s0_ref[h]

    if rider is not None:
        rider()

    r, k2, v, logw, kk, beta = r_ref[...], k2_ref[...], v_ref[...], logw_ref[...], kkn_ref[...], beta_ref[...]
    left = lax.broadcasted_iota(jnp.int32, (1, LANES), 1) < hd
    v_x = pltpu.roll(v, hd, 1)

    ti = lax.broadcasted_iota(jnp.int32, (ch, ch), 0)
    si = lax.broadcasted_iota(jnp.int32, (ch, ch), 1)
    tri_b = (ti >= si).astype(BF16)
    row_c = lax.broadcasted_iota(jnp.int32, (ch, 1), 0)
    t2 = lax.broadcasted_iota(jnp.int32, (2 * ch, 2 * ch), 0)
    s2 = lax.broadcasted_iota(jnp.int32, (2 * ch, 2 * ch), 1)
    tt = jnp.where(t2 >= ch, t2 - ch, t2)
    ss = jnp.where(s2 >= ch, s2 - ch, s2)
    score_mask = (tt - ss) >= jnp.where(t2 >= ch, 0, 1)
    tcol = lax.broadcasted_iota(jnp.int32, (ch, LANES), 1)
    trow = lax.broadcasted_iota(jnp.int32, (ch, LANES), 0)
    eye_right = (tcol == trow + ch).astype(F32)
    zeros_b = jnp.zeros((ch, LANES), BF16)
    own_of = [left if h % 2 == 0 else jnp.logical_not(left) for h in range(nh)]
    tile_of = [slice((h // 2) * LANES, (h // 2 + 1) * LANES) for h in range(nh)]
    xtile_of = [slice((((h + 1) // 2) % ntile) * LANES, (((h + 1) // 2) % ntile + 1) * LANES) for h in range(nh)]

    nu = ng * nh
    lhs, rhs, bk, vx, at_b, rt_own, pc = [], [], [], [], [], [], []
    for g in range(ng):
        rs = slice(g * ch, (g + 1) * ch)
        l1 = logw[rs].astype(BF16)
        rem = logw[rs] - l1.astype(F32)
        l2 = rem.astype(BF16)
        l3 = (rem - l2.astype(F32)).astype(BF16)
        cum3 = _nn(tri_b, jnp.concatenate([l1, l2, l3], axis=1))
        cum = cum3[:, :rw] + (cum3[:, rw:2 * rw] + cum3[:, 2 * rw:])
        ctot = cum[ch - 1:ch, :]
        pc_g = jnp.exp(ctot)
        e_cur = jnp.exp(cum)
        e_neg = jnp.exp(-cum)
        e_prev = jnp.where(row_c == 0, 1.0, pltpu.roll(e_cur, 1, 0))
        e_end = pc_g * e_neg
        rt_f = r[rs] * e_cur
        at_g = (-kk[rs] * e_prev).astype(BF16)
        rt_g = rt_f.astype(BF16)
        bt_g = (beta[rs] * e_neg).astype(BF16)
        kt_g = (k2[rs] * e_neg).astype(BF16)
        bh_g = (beta[rs] * e_end).astype(BF16)
        kh_g = (k2[rs] * e_end).astype(BF16)
        vx_g = v_x[rs].astype(BF16)
        for h in range(nh):
            own, tl = own_of[h], tile_of[h]
            zb = jnp.zeros((), BF16)
            lhs.append(jnp.concatenate([jnp.where(own, at_g[:, tl], zb), jnp.where(own, rt_g[:, tl], zb)], axis=0))
            rhs.append(jnp.concatenate([bt_g[:, tl], kt_g[:, tl]], axis=0))
            bk.append(jnp.concatenate([jnp.where(own, bh_g[:, tl], zb), jnp.where(own, kh_g[:, tl], zb)], axis=0))
            vx.append(jnp.where(own, zb, vx_g[:, xtile_of[h]]))
            at_b.append(at_g[:, tl])
            rt_own.append(jnp.where(own, rt_f[:, tl], 0.0))
            pc.append(pc_g[:, tl])

    units = range(nu)
    own_u = [own_of[i % nh] for i in units]
    sc = [jnp.where(score_mask, _nt(lhs[i], rhs[i]), 0.0) for i in units]
    top_b = [sc[i][:ch].astype(BF16) for i in units]
    bot_b = [sc[i][ch:].astype(BF16) for i in units]
    aakv = [_nn(top_b[i], jnp.concatenate([zeros_b, vx[i]], axis=0)) for i in units]
    x = [jnp.where(left, sc[i][:ch], eye_right) for i in units]
    npow = 1
    while npow < ch:
        xb = [x[i].astype(BF16) for i in units]
        x = [_nn(xb[i], jnp.concatenate([xb[i], zeros_b], axis=0)) + jnp.where(left, 0.0, x[i]) for i in units]
        npow *= 2
    z_b = [jnp.where(own_u[i], at_b[i], aakv[i].astype(BF16)) for i in units]
    wu = [_nn(x[i].astype(BF16), jnp.concatenate([zeros_b, z_b[i]], axis=0)) for i in units]
    rhs2 = [jnp.concatenate([wu[i].astype(BF16), vx[i]], axis=0) for i in units]
    qy = [_nn(bot_b[i], rhs2[i]) for i in units]
    pg = [_tn(rhs2[i], bk[i]) for i in units]

    s_cur = [s_scr[h] for h in range(nh)]
    y_sw = []
    qhat_b = [jnp.where(own_u[i], rt_own[i] + qy[i], 0.0).astype(BF16) for i in units]
    pg_b = [pg[i].astype(BF16) for i in units]
    for g in range(ng):
        hs = range(nh)
        s_b = [s_cur[h].astype(BF16) for h in hs]
        s_pad = [jnp.concatenate([zeros_b, s_b[h]] if h % 2 == 0 else [s_b[h], zeros_b], axis=0) for h in hs]
        s_phi = [_nn(s_b[h], pg_b[g * nh + h]) for h in hs]
        y_s = [_nt(qhat_b[g * nh + h], s_pad[h]) for h in hs]
        y_u = [jnp.where(own_of[h], 0.0, qy[g * nh + h] + y_s[h]) for h in hs]
        for h in hs:
            i = g * nh + h
            g0 = pg[i][ch:] if h % 2 == 0 else pg[i][:ch]
            s_cur[h] = s_cur[h] * pc[i] + s_phi[h] + g0
        y_sw.append(jnp.concatenate([y_u[(2 * q - 1) % nh] + y_u[2 * q] for q in range(ntile)], axis=1))
    for h in range(nh):
        s_scr[h] = s_cur[h]
    y = pltpu.roll(jnp.concatenate(y_sw, axis=0), rw - hd, 1)
    mean = _head_sum(y) * (1.0 / hd)
    dev = y - mean
    var = _head_sum(dev * dev) * (1.0 / hd)
    y = dev * lax.rsqrt(var + GN_EPS) * lw_ref[...] + lb_ref[...] + bonus_ref[...]
    y_ref[...] = y if finish is None else finish(y)

    @pl.when(c == nc - 1)
    def _():
        for h in range(nh):
            sout_ref[h] = s_scr[h, :, (h % 2) * hd:(h % 2 + 1) * hd]


def _rwkv_fused_kernel(*refs, rw, n_in, rider_body, rider_in, rider_out, post_in, final_norm):
    own_in, rest = refs[:n_in], refs[n_in:]
    r_in, rest = rest[:rider_in], rest[rider_in:]
    p_in, rest = rest[:post_in], rest[post_in:]
    own_out, rest = rest[:2], rest[2:]
    r_out, scratch = rest[:rider_out], rest[rider_out:]
    rider = (lambda: rider_body(*r_in, *r_out)) if rider_body is not None else None
    finish = None
    if post_in:
        x_ref, att_ref, g_ref, gr_ref, gate_ref, w_ref, fnw_ref = p_in
        finish = lambda y: _gated_outproj(x_ref, att_ref, g_ref, y, gr_ref, gate_ref, w_ref, fnw_ref, final_norm)
    _rwkv_prompt_kernel(*own_in, *own_out, *scratch, rw=rw, rider=rider, finish=finish)


def _rwkv_prompt(feats, s0, p, rider=None, post=None):
    b, t, rw = feats[0].shape
    nh = rw // HEAD_DIM
    hd = HEAD_DIM
    rows = RWKV_CHUNK * RWKV_CHUNKS_PER_STEP
    vec = lambda a: pl.BlockSpec(a.shape, lambda i, c: (0,) * a.ndim)
    state_spec = pl.BlockSpec((None, nh, hd, hd), lambda i, c: (i, 0, 0, 0))
    args = list(feats) + [s0, p["lw"], p["lb"]]
    in_specs = ([pl.BlockSpec((None, rows, rw), lambda i, c: (i, c, 0))] * len(feats)
                + [state_spec, vec(p["lw"]), vec(p["lb"])])
    n_in = len(args)
    out_w = rw
    rider_args, rider_in_specs, rider_out_specs, rider_out_shape, rider_body = [], [], [], [], None
    if rider is not None:
        rider_args, rider_in_specs = rider["args"], rider["in_specs"]
        rider_out_specs, rider_out_shape, rider_body = rider["out_specs"], rider["out_shape"], rider["body"]
    post_args, post_specs, final_norm = [], [], False
    if post is not None:
        x, att, g_att, g_rwkv, gate, w_out_bf16, final_norm_w, final_norm = post
        d = x.shape[-1]
        npair = att.shape[1]
        out_w = d
        pair_spec = pl.BlockSpec((None, npair, rows, LANES), lambda i, c: (i, 0, c, 0))
        post_args = [x, att, g_att, g_rwkv, gate, w_out_bf16, final_norm_w.reshape(1, d)]
        post_specs = [pl.BlockSpec((None, rows, d), lambda i, c: (i, c, 0)), pair_spec, pair_spec,
                      pl.BlockSpec((None, rows, rw), lambda i, c: (i, c, 0)),
                      pl.BlockSpec((None, 1, d), lambda i, c: (i, 0, 0)),
                      vec(w_out_bf16), pl.BlockSpec((1, d), lambda i, c: (0, 0))]
    body = functools.partial(_rwkv_fused_kernel, rw=rw, n_in=n_in, rider_body=rider_body,
                             rider_in=len(rider_args), rider_out=len(rider_out_specs),
                             post_in=len(post_args), final_norm=final_norm)
    return pl.pallas_call(
        body,
        grid=(b, t // rows),
        in_specs=in_specs + rider_in_specs + post_specs,
        out_specs=[pl.BlockSpec((None, rows, out_w), lambda i, c: (i, c, 0)), state_spec] + rider_out_specs,
        out_shape=[jax.ShapeDtypeStruct((b, t, out_w), F32), jax.ShapeDtypeStruct((b, nh, hd, hd), F32)]
        + rider_out_shape,
        scratch_shapes=[pltpu.VMEM((nh, hd, LANES), F32)],
        compiler_params=_cparams("arbitrary", "arbitrary"),
        name="rwkv_prompt",
    )(*args, *rider_args, *post_args)


def _rwkv_feat_kernel(u_ref, prev_ref, mu_ref, w0_ref, wlb_ref, a0_ref, alb_ref, kk_ref, ka_ref,
                      r_ref, k_ref, v_ref, w_ref, kkr_ref, a_ref, *, rw):
    r, k2, v, logw, kk_raw, a = _rwkv_features(u_ref[...], prev_ref[...], mu_ref[...], w0_ref[...], wlb_ref[...],
                                               a0_ref[...], alb_ref[...], kk_ref[...], ka_ref[...], rw)
    r_ref[...] = r.T
    k_ref[...] = k2.T
    v_ref[...] = v.T
    w_ref[...] = jnp.exp(logw).T
    kkr_ref[...] = kk_raw.T
    a_ref[...] = a.T


def _rwkv_step_kernel(s_ref, r_ref, k_ref, v_ref, w_ref, kkr_ref, a_ref, rk_ref, lw_ref, lb_ref,
                      y_ref, sout_ref, y_scr):
    hd = s_ref.shape[0]
    kk_raw = kkr_ref[...]
    kk = kk_raw / jnp.maximum(jnp.sqrt(jnp.sum(kk_raw * kk_raw, axis=0, keepdims=True)), 1e-12)
    beta = kk * a_ref[...]
    w, k2, r, v = w_ref[...], k_ref[...], r_ref[...], v_ref[...]

    def value_row(i, carry):
        s = s_ref[i]
        sa = jnp.sum(s * kk, axis=0, keepdims=True)
        s_new = s * w - sa * beta + v_ref[pl.ds(i, 1), :] * k2
        sout_ref[i] = s_new
        y_scr[pl.ds(i, 1), :] = jnp.sum(s_new * r, axis=0, keepdims=True)
        return carry

    lax.fori_loop(0, hd, value_row, 0, unroll=8)
    y = y_scr[...]
    mean = jnp.mean(y, axis=0, keepdims=True)
    var = jnp.mean(jnp.square(y - mean), axis=0, keepdims=True)
    yn = (y - mean) * lax.rsqrt(var + GN_EPS) * lw_ref[...] + lb_ref[...]
    y_ref[...] = yn + jnp.sum(r * k2 * rk_ref[...], axis=0, keepdims=True) * v


def _rwkv_sample(u, prev, s0_t, p):
    b, sw = u.shape
    rw = p["w0"].shape[-1]
    nh = rw // HEAD_DIM
    hd = HEAD_DIM
    full = lambda a: pl.BlockSpec(a.shape, lambda i: (0,) * a.ndim)
    fparams = [p["mu"], p["w0"], p["wlb"], p["a0"], p["alb"], p["kk"], p["ka"]]
    feats = pl.pallas_call(
        functools.partial(_rwkv_feat_kernel, rw=rw),
        grid=(1,),
        in_specs=[full(u), full(prev)] + [full(a) for a in fparams],
        out_specs=[pl.BlockSpec((rw, b), lambda i: (0, 0))] * 6,
        out_shape=[jax.ShapeDtypeStruct((rw, b), F32)] * 6,
        compiler_params=_cparams("arbitrary"),
        name="rwkv_sample_features",
    )(u, prev, *fparams)
    chan = pl.BlockSpec((hd, b), lambda h: (h, 0))
    col = pl.BlockSpec((hd, 1), lambda h: (h, 0))
    state_spec = pl.BlockSpec((None, hd, hd, b), lambda h: (h, 0, 0, 0))
    cols = [p[n].reshape(rw, 1) for n in ("rk", "lw", "lb")]
    y_t, s_new = pl.pallas_call(
        _rwkv_step_kernel,
        grid=(nh,),
        in_specs=[state_spec] + [chan] * 6 + [col] * 3,
        out_specs=[chan, state_spec],
        out_shape=[jax.ShapeDtypeStruct((rw, b), F32), jax.ShapeDtypeStruct((nh, hd, hd, b), F32)],
        scratch_shapes=[pltpu.VMEM((hd, b), F32)],
        compiler_params=_cparams("arbitrary"),
        name="rwkv_sample_step",
    )(s0_t, *feats, *cols)
    return y_t.T, s_new


def _outproj_kernel(x_ref, att_ref, g_ref, rw_ref, gr_ref, gate_ref, w_ref, fnw_ref, o_ref, *, final_norm):
    o_ref[...] = _gated_outproj(x_ref, att_ref, g_ref, rw_ref[...], gr_ref, gate_ref, w_ref, fnw_ref, final_norm)


def _outproj(x, att, g_att, rwk, g_rwkv, gate, w_out_bf16, final_norm_w, final_norm):
    b, t, d = x.shape
    npair = att.shape[1]
    rw = rwk.shape[-1]
    tm = min(ROW_TILE, t)
    per_row = gate.shape[1] != 1
    gate_spec = (pl.BlockSpec((None, tm, d), lambda i, j: (i, j, 0)) if per_row
                 else pl.BlockSpec((None, 1, d), lambda i, j: (i, 0, 0)))
    pair_spec = pl.BlockSpec((None, npair, tm, LANES), lambda i, j: (i, 0, j, 0))
    return pl.pallas_call(
        functools.partial(_outproj_kernel, final_norm=final_norm),
        grid=(b, t // tm),
        in_specs=[pl.BlockSpec((None, tm, d), lambda i, j: (i, j, 0)), pair_spec, pair_spec,
                  pl.BlockSpec((None, tm, rw), lambda i, j: (i, j, 0)),
                  pl.BlockSpec((None, tm, rw), lambda i, j: (i, j, 0)),
                  gate_spec,
                  pl.BlockSpec(w_out_bf16.shape, lambda i, j: (0, 0)),
                  pl.BlockSpec((1, d), lambda i, j: (0, 0))],
        out_specs=pl.BlockSpec((None, tm, d), lambda i, j: (i, j, 0)),
        out_shape=jax.ShapeDtypeStruct((b, t, d), F32),
        compiler_params=_cparams("arbitrary", "arbitrary"),
        name="out_proj",
    )(x, att, g_att, rwk, g_rwkv, gate, w_out_bf16, final_norm_w.reshape(1, d))


def _pairs_to_heads(a):
    b, npair, t, _ = a.shape
    return a.transpose(0, 2, 1, 3).reshape(b, t, npair * (LANES // HEAD_DIM), HEAD_DIM)


def kernel(x_prompt, x_sample, cache_win_k, cache_win_v, state_wkv, state_shift, c_prompt, c_sample, rel_bias, norm_w, ada_w, ada_b, w_in, mu_shift, w0, w_lora_b, a0, a_lora_b, k_k, k_a, r_k, ln_x_w, ln_x_b, w_out, final_norm_w):
    depth = norm_w.shape[0]
    bp, tp, d = x_prompt.shape
    bs, ts, _ = x_sample.shape
    assert ts == 1, "the sample group decodes one token per sequence"
    rw = w0.shape[-1]
    att_w = w_out.shape[1] - rw
    shift_w = mu_shift.shape[-1]
    nh_att = att_w // HEAD_DIM
    nh_rw = rw // HEAD_DIM
    assert tp % ATT_SUPER == 0 and tp % (RWKV_CHUNK * RWKV_CHUNKS_PER_STEP) == 0 and bs % 8 == 0
    assert cache_win_k.shape[2] == MAX_WINDOW
    keep = min(MAX_WINDOW, tp)

    bias_tiles = _prompt_bias_tiles(rel_bias)
    c_all = jnp.concatenate([c_prompt, c_sample], axis=0)
    npad = -c_all.shape[0] % 8
    c_all = jnp.pad(c_all, ((0, npad), (0, 0)))

    xp = x_prompt
    xs = x_sample.reshape(1, bs, d)
    outs = [[] for _ in range(8)]
    for l in range(depth):
        mod = _ada_mod(c_all, ada_w[l], ada_b[l])
        shift, scale, gate = jnp.split(mod, 3, axis=-1)
        w_in_b = w_in[l].astype(BF16)
        w_out_b = w_out[l].astype(BF16)
        p = dict(mu=mu_shift[l].reshape(1, -1), w0=w0[l].reshape(1, -1), wlb=w_lora_b[l].astype(BF16),
                 a0=a0[l].reshape(1, -1), alb=a_lora_b[l].astype(BF16), kk=k_k[l].reshape(1, -1),
                 ka=k_a[l].reshape(1, -1), rk=r_k[l].reshape(1, -1), lw=ln_x_w[l].reshape(1, -1),
                 lb=ln_x_b[l].reshape(1, -1))
        last = l == depth - 1

        sm = lambda a: a[bp:bp + bs].reshape(1, bs, d)
        q_s, k_s, v_s, g_att_s, g_rwkv_s, u_s = _inproj(xs, sm(shift), sm(scale), norm_w[l], w_in_b, att_w, shift_w)
        nat = lambda a: a[0].transpose(1, 0, 2).reshape(bs, nh_att, HEAD_DIM)
        k_new, v_new = nat(k_s), nat(v_s)
        att_s_args = (nat(q_s), k_new, v_new, cache_win_k[l].transpose(0, 2, 3, 1),
                      cache_win_v[l].transpose(0, 2, 3, 1), rel_bias)

        pm = lambda a: a[:bp].reshape(bp, 1, d)
        prev0 = jnp.zeros((bp, 1, shift_w), F32)
        q, k, v, g_att, g_rwkv, feats, u_last = _inproj(xp, pm(shift), pm(scale), norm_w[l], w_in_b, att_w, shift_w,
                                                        rwkv=(p, prev0))
        att = _att_prompt(q, k, v, bias_tiles)
        s0 = jnp.zeros((bp, nh_rw, HEAD_DIM, HEAD_DIM), F32)
        steps = tp // (RWKV_CHUNK * RWKV_CHUNKS_PER_STEP)
        post = (xp, att, g_att, g_rwkv, pm(gate), w_out_b, final_norm_w, last)
        if bs == bp * steps:
            rider = _att_sample_job(*att_s_args, lambda i, c: i * steps + c)
            xp, s_p, att_s = _rwkv_prompt(feats, s0, p, rider, post)
        else:
            xp, s_p = _rwkv_prompt(feats, s0, p, None, post)
            att_s = _att_sample(*att_s_args)
        outs[0].append(_pairs_to_heads(k[:, :, tp - keep:, :]))
        outs[1].append(_pairs_to_heads(v[:, :, tp - keep:, :]))
        outs[4].append(s_p)
        outs[6].append(u_last[:, 0])

        att_s = att_s.reshape(bs, att_w // LANES, LANES).transpose(1, 0, 2)[None]
        y_rw_s, s_s = _rwkv_sample(u_s[0], state_shift[l], state_wkv[l].transpose(1, 2, 3, 0), p)
        xs = _outproj(xs, att_s, g_att_s, y_rw_s[None], g_rwkv_s, sm(gate), w_out_b, final_norm_w, last)
        outs[2].append(k_new.reshape(bs, 1, nh_att, HEAD_DIM))
        outs[3].append(v_new.reshape(bs, 1, nh_att, HEAD_DIM))
        outs[5].append(s_s.transpose(3, 0, 1, 2))
        outs[7].append(u_s[0])

    stack = lambda i: jnp.stack(outs[i])
    return (xp, xs.reshape(bs, 1, d), stack(0), stack(1), stack(2), stack(3), stack(4), stack(5), stack(6),
            stack(7))
```

```python
import functools
import math

import numpy as np
import jax
import jax.numpy as jnp
from jax import lax
from jax.experimental import pallas as pl
from jax.experimental.pallas import tpu as pltpu

F32 = jnp.float32
BF16 = jnp.bfloat16
HIGHEST = lax.Precision.HIGHEST

HEAD_DIM = 64
DILATED_BRANCHES = ((128, 1), (512, 4), (2048, 16))
MAX_WINDOW = max(w for w, _ in DILATED_BRANCHES)
KEYS_PER_BRANCH = 128
N_BUCKETS = 32
BUCKET_MAX_DIST = MAX_WINDOW
LORA_W = 64
NORM_EPS = 1e-6
GN_EPS = HEAD_DIM * 1e-5
LOG2E = math.log2(math.e)

LANES = 128
VMEM_LIMIT_BYTES = 56 * 1024 * 1024

ROW_TILE = 256
INPROJ_ROW_TILE = 512
ATT_BLOCK = 128
ATT_SUPER = MAX_WINDOW
ATT_TILES_PER_ITER = {1: 16, 4: 16, 16: 4}
RWKV_CHUNK = 64
RWKV_CHUNKS_PER_STEP = 4


def _cparams(*sem):
    return pltpu.CompilerParams(dimension_semantics=sem, vmem_limit_bytes=VMEM_LIMIT_BYTES)


def _silu(x):
    return x * jax.nn.sigmoid(x)


def _nt(a, b):
    return lax.dot_general(a, b, (((1,), (1,)), ((), ())), preferred_element_type=F32)


def _tn(a, b):
    return lax.dot_general(a, b, (((0,), (0,)), ((), ())), preferred_element_type=F32)


def _nn(a, b, precision=None):
    return jnp.dot(a, b, precision=precision, preferred_element_type=F32)


def _ada_kernel(c_ref, w_ref, b_ref, o_ref):
    s = _silu(c_ref[...])
    o_ref[...] = _nn(s.astype(BF16), w_ref[...].astype(BF16)) + b_ref[...]


def _ada_mod(c_all, ada_w, ada_b):
    n, d = c_all.shape
    e = ada_w.shape[1]
    tn = 512
    return pl.pallas_call(
        _ada_kernel,
        grid=(e // tn,),
        in_specs=[pl.BlockSpec((n, d), lambda j: (0, 0)),
                  pl.BlockSpec((d, tn), lambda j: (0, j)),
                  pl.BlockSpec((1, tn), lambda j: (0, j))],
        out_specs=pl.BlockSpec((n, tn), lambda j: (0, j)),
        out_shape=jax.ShapeDtypeStruct((n, e), F32),
        compiler_params=_cparams("arbitrary"),
        name="ada_mod",
    )(c_all, ada_w, ada_b.reshape(1, e))


def _inproj_kernel(*refs, att_w, shift_w, rw, features):
    x_ref, shift_ref, scale_ref, nw_ref, w_ref = refs[:5]
    if features:
        prev0_ref, mu_ref, w0_ref, wlb_ref, a0_ref, alb_ref, kk_ref, ka_ref, rk_ref = refs[5:14]
        q_ref, k_ref, v_ref, g_ref, gr_ref = refs[14:19]
        feat_refs, ulast_ref, carry_scr = refs[19:26], refs[26], refs[27]
    else:
        q_ref, k_ref, v_ref, g_ref, gr_ref, u_ref = refs[5:11]
    if features:
        @pl.when(pl.program_id(1) == 0)
        def _():
            carry_scr[...] = prev0_ref[...]

    x = x_ref[...]
    xn = x * lax.rsqrt(jnp.mean(x * x, axis=-1, keepdims=True) + NORM_EPS) * nw_ref[...]
    h = (xn * (1.0 + scale_ref[...]) + shift_ref[...]).astype(BF16)
    u = _nn(h, w_ref[:, 4 * att_w:4 * att_w + shift_w])
    if features:
        rows = u.shape[0]
        shifted = pltpu.roll(u, 1, 0)
        row8 = lax.broadcasted_iota(jnp.int32, (8, 1), 0)
        u_prev = jnp.concatenate([jnp.where(row8 == 0, carry_scr[...], shifted[:8]), shifted[8:]], axis=0)
        carry_scr[...] = u[rows - 1:rows, :]
        ulast_ref[...] = u[rows - 1:rows, :]
        r, k2, v, logw, kk_raw, a = _rwkv_features(u, u_prev, mu_ref[...], w0_ref[...], wlb_ref[...], a0_ref[...],
                                                   alb_ref[...], kk_ref[...], ka_ref[...], rw)
        kk = kk_raw * lax.rsqrt(jnp.maximum(_head_sum(kk_raw * kk_raw), 1e-24))
        bonus = _head_sum(r * k2 * rk_ref[...]) * v
        for ref, val in zip(feat_refs, (r, k2, v, logw, kk, kk * a, bonus)):
            ref[...] = val
    else:
        u_ref[...] = u
    z = _nn(h, w_ref[:, :4 * att_w])
    npair = att_w // LANES
    for p in range(npair):
        q_ref[p] = z[:, p * LANES:(p + 1) * LANES] * (HEAD_DIM ** -0.5 * LOG2E)
        k_ref[p] = z[:, att_w + p * LANES:att_w + (p + 1) * LANES]
        v_ref[p] = z[:, 2 * att_w + p * LANES:2 * att_w + (p + 1) * LANES]
        g_ref[p] = z[:, 3 * att_w + p * LANES:3 * att_w + (p + 1) * LANES]
    gr_ref[...] = _nn(h, w_ref[:, 4 * att_w + shift_w:])


def _inproj(x, shift, scale, norm_w, w_in_bf16, att_w, shift_w, rwkv=None):
    b, t, d = x.shape
    in_w = w_in_bf16.shape[1]
    rw = in_w - 4 * att_w - shift_w
    npair = att_w // LANES
    tm = min(INPROJ_ROW_TILE, t)
    per_row = shift.shape[1] != 1
    mod_spec = (pl.BlockSpec((None, tm, d), lambda i, j: (i, j, 0)) if per_row
                else pl.BlockSpec((None, 1, d), lambda i, j: (i, 0, 0)))
    pair_spec = pl.BlockSpec((None, npair, tm, LANES), lambda i, j: (i, 0, j, 0))
    pair_shape = jax.ShapeDtypeStruct((b, npair, t, LANES), F32)
    row_spec = lambda w: pl.BlockSpec((None, tm, w), lambda i, j: (i, j, 0))
    row_shape = lambda w: jax.ShapeDtypeStruct((b, t, w), F32)
    args = [x, shift, scale, norm_w.reshape(1, d), w_in_bf16]
    in_specs = [row_spec(d), mod_spec, mod_spec, pl.BlockSpec((1, d), lambda i, j: (0, 0)),
                pl.BlockSpec((d, in_w), lambda i, j: (0, 0), pipeline_mode=pl.Buffered(1))]
    out_specs = [pair_spec] * 4 + [row_spec(rw)]
    out_shape = [pair_shape] * 4 + [row_shape(rw)]
    scratch = []
    if rwkv is None:
        out_specs.append(row_spec(shift_w))
        out_shape.append(row_shape(shift_w))
    else:
        p, prev0 = rwkv
        consts = [p[n] for n in ("mu", "w0", "wlb", "a0", "alb", "kk", "ka", "rk")]
        args += [prev0] + consts
        in_specs += [pl.BlockSpec((None, 1, shift_w), lambda i, j: (i, 0, 0))]
        in_specs += [pl.BlockSpec(c.shape, lambda i, j: (0, 0)) for c in consts]
        out_specs += [row_spec(rw)] * 7 + [pl.BlockSpec((None, 1, shift_w), lambda i, j: (i, 0, 0))]
        out_shape += [row_shape(rw)] * 7 + [jax.ShapeDtypeStruct((b, 1, shift_w), F32)]
        scratch = [pltpu.VMEM((1, shift_w), F32)]
    outs = pl.pallas_call(
        functools.partial(_inproj_kernel, att_w=att_w, shift_w=shift_w, rw=rw, features=rwkv is not None),
        grid=(b, t // tm),
        in_specs=in_specs,
        out_specs=out_specs,
        out_shape=out_shape,
        scratch_shapes=scratch,
        compiler_params=_cparams("arbitrary", "arbitrary"),
        name="in_proj",
    )(*args)
    if rwkv is None:
        return outs
    return list(outs[:5]) + [list(outs[5:12]), outs[12]]


def _t5_bucket_np(dist):
    max_exact = N_BUCKETS // 2
    nf = np.maximum(dist, max_exact).astype(np.float32)
    large = max_exact + (np.log(nf / np.float32(max_exact)) / np.float32(math.log(BUCKET_MAX_DIST / max_exact))
                         * np.float32(N_BUCKETS - max_exact)).astype(np.int32)
    large = np.minimum(large, N_BUCKETS - 1)
    return np.where(dist < max_exact, dist, large)


def _branch_bias(rel_bias):
    out = []
    for window, dil in DILATED_BRANCHES:
        dist = dil * np.arange(window // dil + 1, dtype=np.int32)
        out.append(rel_bias[_t5_bucket_np(dist)].T.astype(F32) * LOG2E)
    return jnp.stack(out)


def _prompt_bias_tiles(rel_bias):
    bias = _branch_bias(rel_bias)
    nb, nh, _ = bias.shape
    blk, width = ATT_BLOCK, 2 * ATT_BLOCK
    pad = jnp.full((nb, nh, blk - 1), -jnp.inf, F32)
    strip = jnp.flip(jnp.concatenate([pad, bias, pad], axis=-1), axis=-1)
    length = 3 * blk - 1
    rows = jnp.broadcast_to(jnp.pad(strip, ((0, 0), (0, 0), (0, 1)))[:, :, None, :], (nb, nh, blk, length + 1))
    skew = rows.reshape(nb, nh, blk * (length + 1))[:, :, :blk * length].reshape(nb, nh, blk, length)
    tile = skew[:, :, :, blk - 1:blk - 1 + width]
    first = jnp.where((np.arange(width) >= blk)[None, None, None, :], tile, -jnp.inf)
    return jnp.stack([tile, first], axis=1)


def _att_prompt_kernel(q_ref, k_ref, v_ref, bias_ref, o_ref, m_ref, l_ref, acc_ref, *, super_rows):
    sb = pl.program_id(2)
    lane = lax.broadcasted_iota(jnp.int32, (1, LANES), 1)
    head0 = lane < HEAD_DIM
    ones_b = jnp.ones((2 * ATT_BLOCK, LANES), BF16)

    for bi, (_, dil) in enumerate(DILATED_BRANCHES):
        nblk = super_rows // (ATT_BLOCK * dil)
        group = ATT_TILES_PER_ITER[dil]
        rows = lambda s, dil=dil: (pl.ds(s, ATT_BLOCK, stride=dil) if dil > 1 else pl.ds(s, ATT_BLOCK))

        def tiles(it, carry, bi=bi, dil=dil, nblk=nblk, rows=rows, group=group):
            locs, firsts, q2, kcat, vcat = [], [], [], [], []
            for j in range(group):
                idx = it * group + j
                res = idx // nblk
                loc = res + dil * ATT_BLOCK * (idx - res * nblk)
                glob = sb * super_rows + loc
                prev = glob - dil * ATT_BLOCK
                first = prev < 0
                pstart = jnp.where(first, glob, prev)
                qt = q_ref[rows(loc), :]
                locs.append(loc)
                firsts.append(first.astype(jnp.int32))
                q2.append(jnp.concatenate([jnp.where(head0, qt, 0.0), jnp.where(head0, 0.0, qt)],
                                          axis=0).astype(BF16))
                kcat.append(jnp.concatenate([k_ref[rows(pstart), :], k_ref[rows(glob), :]], axis=0).astype(BF16))
                vcat.append(jnp.concatenate([v_ref[rows(pstart), :], v_ref[rows(glob), :]], axis=0).astype(BF16))
            s = [_nt(q2[j], kcat[j]) + bias_ref[bi, firsts[j]].reshape(2 * ATT_BLOCK, 2 * ATT_BLOCK)
                 for j in range(group)]
            mt = [jnp.max(x, axis=-1, keepdims=True) for x in s]
            p = [jnp.exp2(s[j] - mt[j]).astype(BF16) for j in range(group)]
            pv = [_nn(p[j], jnp.concatenate([vcat[j], ones_b], axis=1)) for j in range(group)]
            for j in range(group):
                r = rows(locs[j])
                m_ref[bi, r, :] = jnp.where(head0, mt[j][:ATT_BLOCK], mt[j][ATT_BLOCK:])
                l_ref[bi, r, :] = jnp.where(head0, pv[j][:ATT_BLOCK, LANES:], pv[j][ATT_BLOCK:, LANES:])
                acc_ref[bi, r, :] = jnp.where(head0, pv[j][:ATT_BLOCK, :LANES], pv[j][ATT_BLOCK:, :LANES])
            return carry

        lax.fori_loop(0, dil * nblk // group, tiles, 0)

    nb = len(DILATED_BRANCHES)
    chunk = 2 * ATT_BLOCK

    def merge(c, carry):
        r = pl.ds(pl.multiple_of(c * chunk, chunk), chunk)
        ms = [m_ref[n, r, :] for n in range(nb)]
        m = functools.reduce(jnp.maximum, ms)
        ws = [jnp.exp2(x - m) for x in ms]
        den = functools.reduce(jnp.add, [l_ref[n, r, :] * ws[n] for n in range(nb)])
        num = functools.reduce(jnp.add, [acc_ref[n, r, :] * ws[n] for n in range(nb)])
        o_ref[r, :] = num / den
        return carry

    lax.fori_loop(0, super_rows // chunk, merge, 0)


def _att_prompt(q, k, v, bias_tiles):
    b, npair, t, _ = q.shape
    sr = ATT_SUPER
    nb = len(DILATED_BRANCHES)
    return pl.pallas_call(
        functools.partial(_att_prompt_kernel, super_rows=sr),
        grid=(b, npair, t // sr),
        in_specs=[pl.BlockSpec((None, None, sr, LANES), lambda i, p, s: (i, p, s, 0)),
                  pl.BlockSpec((None, None, t, LANES), lambda i, p, s: (i, p, 0, 0)),
                  pl.BlockSpec((None, None, t, LANES), lambda i, p, s: (i, p, 0, 0)),
                  pl.BlockSpec((nb, 2, 2, ATT_BLOCK, 2 * ATT_BLOCK), lambda i, p, s: (0, 0, p, 0, 0))],
        out_specs=pl.BlockSpec((None, None, sr, LANES), lambda i, p, s: (i, p, s, 0)),
        out_shape=jax.ShapeDtypeStruct((b, npair, t, LANES), F32),
        scratch_shapes=[pltpu.VMEM((nb, sr, LANES), F32)] * 3,
        compiler_params=_cparams("arbitrary", "arbitrary", "arbitrary"),
        name="att_prompt",
    )(q, k, v, bias_tiles)


def _att_sample_kernel(q_ref, kn_ref, vn_ref, kt_ref, vt_ref, bias_ref, b0_ref, o_ref):
    nh, hd, l = kt_ref.shape
    nb = bias_ref.shape[0]
    q = q_ref[...]
    qb = q.astype(BF16)
    head = lax.broadcasted_iota(jnp.int32, (nh, 1), 0)
    logit = jnp.zeros((nh, l), F32)
    for h in range(nh):
        s = _nn(qb, kt_ref[h].astype(BF16))
        logit = jnp.where(head == h, s, logit)
    s0 = jnp.sum(q * kn_ref[...], axis=-1, keepdims=True)
    ls = [logit + bias_ref[n] for n in range(nb)]
    s_self = [s0 + b0_ref[n] for n in range(nb)]
    m = functools.reduce(jnp.maximum, [jnp.max(x, axis=-1, keepdims=True) for x in ls] + s_self)
    p = functools.reduce(jnp.add, [jnp.exp2(x - m) for x in ls])
    p0 = functools.reduce(jnp.add, [jnp.exp2(x - m) for x in s_self])
    den = jnp.sum(p, axis=-1, keepdims=True) + p0
    pb = p.astype(BF16)
    out = p0 * vn_ref[...]
    for h in range(nh):
        out = out + jnp.where(head == h, _nt(pb, vt_ref[h].astype(BF16)), 0.0)
    o_ref[...] = out / den


def _att_sample_job(q, k_new, v_new, cache_kt, cache_vt, rel_bias, seq_of_step):
    b, nh, hd = q.shape
    l = cache_kt.shape[-1]
    bias = _branch_bias(rel_bias)
    back = l - np.arange(l)
    tabs = []
    for n, (window, dil) in enumerate(DILATED_BRANCHES):
        used = (back % dil == 0) & (back <= window)
        tabs.append(jnp.where(used[None, :], bias[n][:, np.where(used, back // dil, 0)], -jnp.inf))
    bias_pos = jnp.stack(tabs)
    bias0 = bias[:, :, 0][..., None]
    row = pl.BlockSpec((None, nh, hd), lambda *g: (seq_of_step(*g), 0, 0))
    cache = pl.BlockSpec((None, nh, hd, l), lambda *g: (seq_of_step(*g), 0, 0, 0))
    full = lambda a: pl.BlockSpec(a.shape, lambda *g: (0,) * a.ndim)
    return dict(body=_att_sample_kernel, args=[q, k_new, v_new, cache_kt, cache_vt, bias_pos, bias0],
                in_specs=[row, row, row, cache, cache, full(bias_pos), full(bias0)],
                out_specs=[row], out_shape=[jax.ShapeDtypeStruct((b, nh, hd), F32)])


def _att_sample(q, k_new, v_new, cache_kt, cache_vt, rel_bias):
    job = _att_sample_job(q, k_new, v_new, cache_kt, cache_vt, rel_bias, lambda i: i)
    return pl.pallas_call(
        job["body"],
        grid=(q.shape[0],),
        in_specs=job["in_specs"],
        out_specs=job["out_specs"][0],
        out_shape=job["out_shape"][0],
        compiler_params=_cparams("arbitrary"),
        name="att_sample",
    )(*job["args"])


def _gated_outproj(x_ref, att_ref, g_ref, rwk, gr_ref, gate_ref, w_ref, fnw_ref, final_norm):
    npair = att_ref.shape[0]
    mixed = [(att_ref[p] * _silu(g_ref[p])).astype(BF16) for p in range(npair)]
    mixed.append((rwk * _silu(gr_ref[...])).astype(BF16))
    acc = _nn(jnp.concatenate(mixed, axis=1), w_ref[...])
    xo = x_ref[...] + gate_ref[...] * acc
    if final_norm:
        xo = xo * lax.rsqrt(jnp.mean(xo * xo, axis=-1, keepdims=True) + NORM_EPS) * fnw_ref[...]
    return xo


def _head_sum(x):
    left = lax.broadcasted_iota(jnp.int32, (1, LANES), 1) < HEAD_DIM
    tiles = []
    for q in range(x.shape[1] // LANES):
        t = x[:, q * LANES:(q + 1) * LANES]
        s_left = jnp.sum(jnp.where(left, t, 0.0), axis=-1, keepdims=True)
        s_right = jnp.sum(jnp.where(left, 0.0, t), axis=-1, keepdims=True)
        tiles.append(jnp.where(left, s_left, s_right))
    return jnp.concatenate(tiles, axis=1)


def _rwkv_features(u, u_prev, mu, w0, wlb, a0, alb, kk_scale, ka, rw):
    um = u + (u_prev - u) * mu
    r = um[:, :rw]
    k = um[:, rw:2 * rw]
    v = um[:, 2 * rw:3 * rw]
    xw = um[:, 3 * rw:3 * rw + LORA_W]
    xa = um[:, 3 * rw + LORA_W:3 * rw + 2 * LORA_W]
    wl = w0 + _nn(jnp.tanh(xw).astype(BF16), wlb)
    logw = -math.exp(-0.5) * jax.nn.sigmoid(wl)
    a = jax.nn.sigmoid(a0 + _nn(xa.astype(BF16), alb))
    kk_raw = k * kk_scale
    k2 = k * (1.0 + (a - 1.0) * ka)
    return r, k2, v, logw, kk_raw, a


def _rwkv_prompt_kernel(r_ref, k2_ref, v_ref, logw_ref, kkn_ref, beta_ref, bonus_ref, s0_ref, lw_ref, lb_ref,
                        y_ref, sout_ref, s_scr, *, rw, rider=None, finish=None):
    c = pl.program_id(1)
    nc = pl.num_programs(1)
    rows = r_ref.shape[0]
    ch = RWKV_CHUNK
    ng = rows // ch
    nh = rw // HEAD_DIM
    hd = HEAD_DIM
    ntile = rw // LANES

    @pl.when(c == 0)
    def _():
        s_scr[...] = jnp.zeros(s_scr.shape, F32)
        for h in range(nh):
            s_scr[h, :, (h % 2) * hd:(h % 2 + 1) * hd] = s0_ref[h]

    if rider is not None:
        rider()

    r, k2, v, logw, kk, beta = r_ref[...], k2_ref[...], v_ref[...], logw_ref[...], kkn_ref[...], beta_ref[...]
    left = lax.broadcasted_iota(jnp.int32, (1, LANES), 1) < hd
    v_x = pltpu.roll(v, hd, 1)

    ti = lax.broadcasted_iota(jnp.int32, (ch, ch), 0)
    si = lax.broadcasted_iota(jnp.int32, (ch, ch), 1)
    tri_b = (ti >= si).astype(BF16)
    row_c = lax.broadcasted_iota(jnp.int32, (ch, 1), 0)
    t2 = lax.broadcasted_iota(jnp.int32, (2 * ch, 2 * ch), 0)
    s2 = lax.broadcasted_iota(jnp.int32, (2 * ch, 2 * ch), 1)
    tt = jnp.where(t2 >= ch, t2 - ch, t2)
    ss = jnp.where(s2 >= ch, s2 - ch, s2)
    score_mask = (tt - ss) >= jnp.where(t2 >= ch, 0, 1)
    tcol = lax.broadcasted_iota(jnp.int32, (ch, LANES), 1)
    trow = lax.broadcasted_iota(jnp.int32, (ch, LANES), 0)
    eye_right = (tcol == trow + ch).astype(F32)
    zeros_b = jnp.zeros((ch, LANES), BF16)
    own_of = [left if h % 2 == 0 else jnp.logical_not(left) for h in range(nh)]
    tile_of = [slice((h // 2) * LANES, (h // 2 + 1) * LANES) for h in range(nh)]
    xtile_of = [slice((((h + 1) // 2) % ntile) * LANES, (((h + 1) // 2) % ntile + 1) * LANES) for h in range(nh)]

    nu = ng * nh
    lhs, rhs, bk, vx, at_b, rt_own, pc = [], [], [], [], [], [], []
    for g in range(ng):
        rs = slice(g * ch, (g + 1) * ch)
        l1 = logw[rs].astype(BF16)
        rem = logw[rs] - l1.astype(F32)
        l2 = rem.astype(BF16)
        l3 = (rem - l2.astype(F32)).astype(BF16)
        cum3 = _nn(tri_b, jnp.concatenate([l1, l2, l3], axis=1))
        cum = cum3[:, :rw] + (cum3[:, rw:2 * rw] + cum3[:, 2 * rw:])
        ctot = cum[ch - 1:ch, :]
        pc_g = jnp.exp(ctot)
        e_cur = jnp.exp(cum)
        e_neg = jnp.exp(-cum)
        e_prev = jnp.where(row_c == 0, 1.0, pltpu.roll(e_cur, 1, 0))
        e_end = pc_g * e_neg
        rt_f = r[rs] * e_cur
        at_g = (-kk[rs] * e_prev).astype(BF16)
        rt_g = rt_f.astype(BF16)
        bt_g = (beta[rs] * e_neg).astype(BF16)
        kt_g = (k2[rs] * e_neg).astype(BF16)
        bh_g = (beta[rs] * e_end).astype(BF16)
        kh_g = (k2[rs] * e_end).astype(BF16)
        vx_g = v_x[rs].astype(BF16)
        for h in range(nh):
            own, tl = own_of[h], tile_of[h]
            zb = jnp.zeros((), BF16)
            lhs.append(jnp.concatenate([jnp.where(own, at_g[:, tl], zb), jnp.where(own, rt_g[:, tl], zb)], axis=0))
            rhs.append(jnp.concatenate([bt_g[:, tl], kt_g[:, tl]], axis=0))
            bk.append(jnp.concatenate([jnp.where(own, bh_g[:, tl], zb), jnp.where(own, kh_g[:, tl], zb)], axis=0))
            vx.append(jnp.where(own, zb, vx_g[:, xtile_of[h]]))
            at_b.append(at_g[:, tl])
            rt_own.append(jnp.where(own, rt_f[:, tl], 0.0))
            pc.append(pc_g[:, tl])

    units = range(nu)
    own_u = [own_of[i % nh] for i in units]
    sc = [jnp.where(score_mask, _nt(lhs[i], rhs[i]), 0.0) for i in units]
    top_b = [sc[i][:ch].astype(BF16) for i in units]
    bot_b = [sc[i][ch:].astype(BF16) for i in units]
    aakv = [_nn(top_b[i], jnp.concatenate([zeros_b, vx[i]], axis=0)) for i in units]
    x = [jnp.where(left, sc[i][:ch], eye_right) for i in units]
    npow = 1
    while npow < ch:
        xb = [x[i].astype(BF16) for i in units]
        x = [_nn(xb[i], jnp.concatenate([xb[i], zeros_b], axis=0)) + jnp.where(left, 0.0, x[i]) for i in units]
        npow *= 2
    z_b = [jnp.where(own_u[i], at_b[i], aakv[i].astype(BF16)) for i in units]
    wu = [_nn(x[i].astype(BF16), jnp.concatenate([zeros_b, z_b[i]], axis=0)) for i in units]
    rhs2 = [jnp.concatenate([wu[i].astype(BF16), vx[i]], axis=0) for i in units]
    qy = [_nn(bot_b[i], rhs2[i]) for i in units]
    pg = [_tn(rhs2[i], bk[i]) for i in units]

    s_cur = [s_scr[h] for h in range(nh)]
    y_sw = []
    qhat_b = [jnp.where(own_u[i], rt_own[i] + qy[i], 0.0).astype(BF16) for i in units]
    pg_b = [pg[i].astype(BF16) for i in units]
    for g in range(ng):
        hs = range(nh)
        s_b = [s_cur[h].astype(BF16) for h in hs]
        s_pad = [jnp.concatenate([zeros_b, s_b[h]] if h % 2 == 0 else [s_b[h], zeros_b], axis=0) for h in hs]
        s_phi = [_nn(s_b[h], pg_b[g * nh + h]) for h in hs]
        y_s = [_nt(qhat_b[g * nh + h], s_pad[h]) for h in hs]
        y_u = [jnp.where(own_of[h], 0.0, qy[g * nh + h] + y_s[h]) for h in hs]
        for h in hs:
            i = g * nh + h
            g0 = pg[i][ch:] if h % 2 == 0 else pg[i][:ch]
            s_cur[h] = s_cur[h] * pc[i] + s_phi[h] + g0
        y_sw.append(jnp.concatenate([y_u[(2 * q - 1) % nh] + y_u[2 * q] for q in range(ntile)], axis=1))
    for h in range(nh):
        s_scr[h] = s_cur[h]
    y = pltpu.roll(jnp.concatenate(y_sw, axis=0), rw - hd, 1)
    mean = _head_sum(y) * (1.0 / hd)
    dev = y - mean
    var = _head_sum(dev * dev) * (1.0 / hd)
    y = dev * lax.rsqrt(var + GN_EPS) * lw_ref[...] + lb_ref[...] + bonus_ref[...]
    y_ref[...] = y if finish is None else finish(y)

    @pl.when(c == nc - 1)
    def _():
        for h in range(nh):
            sout_ref[h] = s_scr[h, :, (h % 2) * hd:(h % 2 + 1) * hd]


def _rwkv_fused_kernel(*refs, rw, n_in, rider_body, rider_in, rider_out, post_in, final_norm):
    own_in, rest = refs[:n_in], refs[n_in:]
    r_in, rest = rest[:rider_in], rest[rider_in:]
    p_in, rest = rest[:post_in], rest[post_in:]
    own_out, rest = rest[:2], rest[2:]
    r_out, scratch = rest[:rider_out], rest[rider_out:]
    rider = (lambda: rider_body(*r_in, *r_out)) if rider_body is not None else None
    finish = None
    if post_in:
        x_ref, att_ref, g_ref, gr_ref, gate_ref, w_ref, fnw_ref = p_in
        finish = lambda y: _gated_outproj(x_ref, att_ref, g_ref, y, gr_ref, gate_ref, w_ref, fnw_ref, final_norm)
    _rwkv_prompt_kernel(*own_in, *own_out, *scratch, rw=rw, rider=rider, finish=finish)


def _rwkv_prompt(feats, s0, p, rider=None, post=None):
    b, t, rw = feats[0].shape
    nh = rw // HEAD_DIM
    hd = HEAD_DIM
    rows = RWKV_CHUNK * RWKV_CHUNKS_PER_STEP
    vec = lambda a: pl.BlockSpec(a.shape, lambda i, c: (0,) * a.ndim)
    state_spec = pl.BlockSpec((None, nh, hd, hd), lambda i, c: (i, 0, 0, 0))
    args = list(feats) + [s0, p["lw"], p["lb"]]
    in_specs = ([pl.BlockSpec((None, rows, rw), lambda i, c: (i, c, 0))] * len(feats)
                + [state_spec, vec(p["lw"]), vec(p["lb"])])
    n_in = len(args)
    out_w = rw
    rider_args, rider_in_specs, rider_out_specs, rider_out_shape, rider_body = [], [], [], [], None
    if rider is not None:
        rider_args, rider_in_specs = rider["args"], rider["in_specs"]
        rider_out_specs, rider_out_shape, rider_body = rider["out_specs"], rider["out_shape"], rider["body"]
    post_args, post_specs, final_norm = [], [], False
    if post is not None:
        x, att, g_att, g_rwkv, gate, w_out_bf16, final_norm_w, final_norm = post
        d = x.shape[-1]
        npair = att.shape[1]
        out_w = d
        pair_spec = pl.BlockSpec((None, npair, rows, LANES), lambda i, c: (i, 0, c, 0))
        post_args = [x, att, g_att, g_rwkv, gate, w_out_bf16, final_norm_w.reshape(1, d)]
        post_specs = [pl.BlockSpec((None, rows, d), lambda i, c: (i, c, 0)), pair_spec, pair_spec,
                      pl.BlockSpec((None, rows, rw), lambda i, c: (i, c, 0)),
                      pl.BlockSpec((None, 1, d), lambda i, c: (i, 0, 0)),
                      vec(w_out_bf16), pl.BlockSpec((1, d), lambda i, c: (0, 0))]
    body = functools.partial(_rwkv_fused_kernel, rw=rw, n_in=n_in, rider_body=rider_body,
                             rider_in=len(rider_args), rider_out=len(rider_out_specs),
                             post_in=len(post_args), final_norm=final_norm)
    return pl.pallas_call(
        body,
        grid=(b, t // rows),
        in_specs=in_specs + rider_in_specs + post_specs,
        out_specs=[pl.BlockSpec((None, rows, out_w), lambda i, c: (i, c, 0)), state_spec] + rider_out_specs,
        out_shape=[jax.ShapeDtypeStruct((b, t, out_w), F32), jax.ShapeDtypeStruct((b, nh, hd, hd), F32)]
        + rider_out_shape,
        scratch_shapes=[pltpu.VMEM((nh, hd, LANES), F32)],
        compiler_params=_cparams("arbitrary", "arbitrary"),
        name="rwkv_prompt",
    )(*args, *rider_args, *post_args)


def _rwkv_feat_kernel(u_ref, prev_ref, mu_ref, w0_ref, wlb_ref, a0_ref, alb_ref, kk_ref, ka_ref,
                      r_ref, k_ref, v_ref, w_ref, kkr_ref, a_ref, *, rw):
    r, k2, v, logw, kk_raw, a = _rwkv_features(u_ref[...], prev_ref[...], mu_ref[...], w0_ref[...], wlb_ref[...],
                                               a0_ref[...], alb_ref[...], kk_ref[...], ka_ref[...], rw)
    r_ref[...] = r.T
    k_ref[...] = k2.T
    v_ref[...] = v.T
    w_ref[...] = jnp.exp(logw).T
    kkr_ref[...] = kk_raw.T
    a_ref[...] = a.T


def _rwkv_step_kernel(s_ref, r_ref, k_ref, v_ref, w_ref, kkr_ref, a_ref, rk_ref, lw_ref, lb_ref,
                      y_ref, sout_ref, y_scr):
    hd = s_ref.shape[0]
    kk_raw = kkr_ref[...]
    kk = kk_raw / jnp.maximum(jnp.sqrt(jnp.sum(kk_raw * kk_raw, axis=0, keepdims=True)), 1e-12)
    beta = kk * a_ref[...]
    w, k2, r, v = w_ref[...], k_ref[...], r_ref[...], v_ref[...]

    def value_row(i, carry):
        s = s_ref[i]
        sa = jnp.sum(s * kk, axis=0, keepdims=True)
        s_new = s * w - sa * beta + v_ref[pl.ds(i, 1), :] * k2
        sout_ref[i] = s_new
        y_scr[pl.ds(i, 1), :] = jnp.sum(s_new * r, axis=0, keepdims=True)
        return carry

    lax.fori_loop(0, hd, value_row, 0, unroll=8)
    y = y_scr[...]
    mean = jnp.mean(y, axis=0, keepdims=True)
    var = jnp.mean(jnp.square(y - mean), axis=0, keepdims=True)
    yn = (y - mean) * lax.rsqrt(var + GN_EPS) * lw_ref[...] + lb_ref[...]
    y_ref[...] = yn + jnp.sum(r * k2 * rk_ref[...], axis=0, keepdims=True) * v


def _rwkv_sample(u, prev, s0_t, p):
    b, sw = u.shape
    rw = p["w0"].shape[-1]
    nh = rw // HEAD_DIM
    hd = HEAD_DIM
    full = lambda a: pl.BlockSpec(a.shape, lambda i: (0,) * a.ndim)
    fparams = [p["mu"], p["w0"], p["wlb"], p["a0"], p["alb"], p["kk"], p["ka"]]
    feats = pl.pallas_call(
        functools.partial(_rwkv_feat_kernel, rw=rw),
        grid=(1,),
        in_specs=[full(u), full(prev)] + [full(a) for a in fparams],
        out_specs=[pl.BlockSpec((rw, b), lambda i: (0, 0))] * 6,
        out_shape=[jax.ShapeDtypeStruct((rw, b), F32)] * 6,
        compiler_params=_cparams("arbitrary"),
        name="rwkv_sample_features",
    )(u, prev, *fparams)
    chan = pl.BlockSpec((hd, b), lambda h: (h, 0))
    col = pl.BlockSpec((hd, 1), lambda h: (h, 0))
    state_spec = pl.BlockSpec((None, hd, hd, b), lambda h: (h, 0, 0, 0))
    cols = [p[n].reshape(rw, 1) for n in ("rk", "lw", "lb")]
    y_t, s_new = pl.pallas_call(
        _rwkv_step_kernel,
        grid=(nh,),
        in_specs=[state_spec] + [chan] * 6 + [col] * 3,
        out_specs=[chan, state_spec],
        out_shape=[jax.ShapeDtypeStruct((rw, b), F32), jax.ShapeDtypeStruct((nh, hd, hd, b), F32)],
        scratch_shapes=[pltpu.VMEM((hd, b), F32)],
        compiler_params=_cparams("arbitrary"),
        name="rwkv_sample_step",
    )(s0_t, *feats, *cols)
    return y_t.T, s_new


def _outproj_kernel(x_ref, att_ref, g_ref, rw_ref, gr_ref, gate_ref, w_ref, fnw_ref, o_ref, *, final_norm):
    o_ref[...] = _gated_outproj(x_ref, att_ref, g_ref, rw_ref[...], gr_ref, gate_ref, w_ref, fnw_ref, final_norm)


def _outproj(x, att, g_att, rwk, g_rwkv, gate, w_out_bf16, final_norm_w, final_norm):
    b, t, d = x.shape
    npair = att.shape[1]
    rw = rwk.shape[-1]
    tm = min(ROW_TILE, t)
    per_row = gate.shape[1] != 1
    gate_spec = (pl.BlockSpec((None, tm, d), lambda i, j: (i, j, 0)) if per_row
                 else pl.BlockSpec((None, 1, d), lambda i, j: (i, 0, 0)))
    pair_spec = pl.BlockSpec((None, npair, tm, LANES), lambda i, j: (i, 0, j, 0))
    return pl.pallas_call(
        functools.partial(_outproj_kernel, final_norm=final_norm),
        grid=(b, t // tm),
        in_specs=[pl.BlockSpec((None, tm, d), lambda i, j: (i, j, 0)), pair_spec, pair_spec,
                  pl.BlockSpec((None, tm, rw), lambda i, j: (i, j, 0)),
                  pl.BlockSpec((None, tm, rw), lambda i, j: (i, j, 0)),
                  gate_spec,
                  pl.BlockSpec(w_out_bf16.shape, lambda i, j: (0, 0)),
                  pl.BlockSpec((1, d), lambda i, j: (0, 0))],
        out_specs=pl.BlockSpec((None, tm, d), lambda i, j: (i, j, 0)),
        out_shape=jax.ShapeDtypeStruct((b, t, d), F32),
        compiler_params=_cparams("arbitrary", "arbitrary"),
        name="out_proj",
    )(x, att, g_att, rwk, g_rwkv, gate, w_out_bf16, final_norm_w.reshape(1, d))


def _pairs_to_heads(a):
    b, npair, t, _ = a.shape
    return a.transpose(0, 2, 1, 3).reshape(b, t, npair * (LANES // HEAD_DIM), HEAD_DIM)


def kernel(x_prompt, x_sample, cache_win_k, cache_win_v, state_wkv, state_shift, c_prompt, c_sample, rel_bias, norm_w, ada_w, ada_b, w_in, mu_shift, w0, w_lora_b, a0, a_lora_b, k_k, k_a, r_k, ln_x_w, ln_x_b, w_out, final_norm_w):
    depth = norm_w.shape[0]
    bp, tp, d = x_prompt.shape
    bs, ts, _ = x_sample.shape
    assert ts == 1, "the sample group decodes one token per sequence"
    rw = w0.shape[-1]
    att_w = w_out.shape[1] - rw
    shift_w = mu_shift.shape[-1]
    nh_att = att_w // HEAD_DIM
    nh_rw = rw // HEAD_DIM
    assert tp % ATT_SUPER == 0 and tp % (RWKV_CHUNK * RWKV_CHUNKS_PER_STEP) == 0 and bs % 8 == 0
    assert cache_win_k.shape[2] == MAX_WINDOW
    keep = min(MAX_WINDOW, tp)

    bias_tiles = _prompt_bias_tiles(rel_bias)
    c_all = jnp.concatenate([c_prompt, c_sample], axis=0)
    npad = -c_all.shape[0] % 8
    c_all = jnp.pad(c_all, ((0, npad), (0, 0)))

    xp = x_prompt
    xs = x_sample.reshape(1, bs, d)
    outs = [[] for _ in range(8)]
    for l in range(depth):
        mod = _ada_mod(c_all, ada_w[l], ada_b[l])
        shift, scale, gate = jnp.split(mod, 3, axis=-1)
        w_in_b = w_in[l].astype(BF16)
        w_out_b = w_out[l].astype(BF16)
        p = dict(mu=mu_shift[l].reshape(1, -1), w0=w0[l].reshape(1, -1), wlb=w_lora_b[l].astype(BF16),
                 a0=a0[l].reshape(1, -1), alb=a_lora_b[l].astype(BF16), kk=k_k[l].reshape(1, -1),
                 ka=k_a[l].reshape(1, -1), rk=r_k[l].reshape(1, -1), lw=ln_x_w[l].reshape(1, -1),
                 lb=ln_x_b[l].reshape(1, -1))
        last = l == depth - 1

        sm = lambda a: a[bp:bp + bs].reshape(1, bs, d)
        q_s, k_s, v_s, g_att_s, g_rwkv_s, u_s = _inproj(xs, sm(shift), sm(scale), norm_w[l], w_in_b, att_w, shift_w)
        nat = lambda a: a[0].transpose(1, 0, 2).reshape(bs, nh_att, HEAD_DIM)
        k_new, v_new = nat(k_s), nat(v_s)
        att_s_args = (nat(q_s), k_new, v_new, cache_win_k[l].transpose(0, 2, 3, 1),
                      cache_win_v[l].transpose(0, 2, 3, 1), rel_bias)

        pm = lambda a: a[:bp].reshape(bp, 1, d)
        prev0 = jnp.zeros((bp, 1, shift_w), F32)
        q, k, v, g_att, g_rwkv, feats, u_last = _inproj(xp, pm(shift), pm(scale), norm_w[l], w_in_b, att_w, shift_w,
                                                        rwkv=(p, prev0))
        att = _att_prompt(q, k, v, bias_tiles)
        s0 = jnp.zeros((bp, nh_rw, HEAD_DIM, HEAD_DIM), F32)
        steps = tp // (RWKV_CHUNK * RWKV_CHUNKS_PER_STEP)
        post = (xp, att, g_att, g_rwkv, pm(gate), w_out_b, final_norm_w, last)
        if bs == bp * steps:
            rider = _att_sample_job(*att_s_args, lambda i, c: i * steps + c)
            xp, s_p, att_s = _rwkv_prompt(feats, s0, p, rider, post)
        else:
            xp, s_p = _rwkv_prompt(feats, s0, p, None, post)
            att_s = _att_sample(*att_s_args)
        outs[0].append(_pairs_to_heads(k[:, :, tp - keep:, :]))
        outs[1].append(_pairs_to_heads(v[:, :, tp - keep:, :]))
        outs[4].append(s_p)
        outs[6].append(u_last[:, 0])

        att_s = att_s.reshape(bs, att_w // LANES, LANES).transpose(1, 0, 2)[None]
        y_rw_s, s_s = _rwkv_sample(u_s[0], state_shift[l], state_wkv[l].transpose(1, 2, 3, 0), p)
        xs = _outproj(xs, att_s, g_att_s, y_rw_s[None], g_rwkv_s, sm(gate), w_out_b, final_norm_w, last)
        outs[2].append(k_new.reshape(bs, 1, nh_att, HEAD_DIM))
        outs[3].append(v_new.reshape(bs, 1, nh_att, HEAD_DIM))
        outs[5].append(s_s.transpose(3, 0, 1, 2))
        outs[7].append(u_s[0])

    stack = lambda i: jnp.stack(outs[i])
    return (xp, xs.reshape(bs, 1, d), stack(0), stack(1), stack(2), stack(3), stack(4), stack(5), stack(6),
            stack(7))
```
